```python
import math
import jax
import jax.numpy as jnp
from jax import lax
import numpy as np

D_MODEL = 1024
BATCH = 2
SEQ = 8192
DEPTH = 4
DEC_BATCH = 128
DEC_SEQ = 4
PAST_LEN = 2048
PAGE_SIZE = 128

N_EVEN = (DEPTH + 1) // 2
N_ODD = DEPTH // 2
MIX_W = D_MODEL
HALF = MIX_W // 2
D_FF = 2816
PLE_DIM = 256
RMS_EPS = 1e-6
H_A = 4
DK_A = HALF // H_A // 2
DV_A = HALF // H_A
GLA_GK_RANK = 16
GLA_GK_NORM = 16.0
GLA_CHUNK = 32
H_B = 4
EXP_B = 128
DV_B = HALF // H_B
H_C = 8
DH_C = HALF // H_C
SB_BLOCK = 128
SB_BIAS_INIT = -8.0
H_D = 8
DH_D = HALF // H_D
DM_D = H_D * DH_D
R_W = 64
R_A = 64
R_G = 128
RWKV_LN_EPS = 64e-5

EVEN_SPLITS = (H_A * DK_A, H_A * DK_A, H_A * DV_A, GLA_GK_RANK, H_A * DV_A,
               H_B * EXP_B, H_B * EXP_B, H_B * DV_B, H_B * DV_B)
RWKV_SPLITS = (DM_D, DM_D, DM_D, R_W, R_A, R_G)
N_COLS_RWKV = sum(RWKV_SPLITS)
ODD_SPLITS = (H_C * DH_C, H_C * DH_C, H_C * DH_C, N_COLS_RWKV)
N_COLS_EVEN = sum(EVEN_SPLITS)
N_COLS_ODD = sum(ODD_SPLITS)

kernel_name = 'hybrid_gla_hgrn2_stickbreak_rwkv7_macaron_step'


def split_cols(z, sizes):
    return jnp.split(z, np.cumsum(sizes)[:-1].tolist(), axis=-1)


def rmsnorm(x, g):
    xf = x.astype(jnp.float32)
    y = xf * lax.rsqrt(jnp.mean(xf * xf, axis=-1, keepdims=True) + RMS_EPS)
    return (y * g.astype(jnp.float32)).astype(x.dtype)


def head_rmsnorm(o, g):
    return o * lax.rsqrt(jnp.mean(o * o, axis=-1, keepdims=True) + RMS_EPS) * g.astype(jnp.float32)


def swiglu(x, wg, wu, wd):
    return (jax.nn.silu(x @ wg) * (x @ wu)) @ wd


def gated_linear_chunk(q, k, v, logf, S0):
    f32 = jnp.float32
    B, T, H, K = q.shape
    V = v.shape[-1]
    C = math.gcd(T, GLA_CHUNK)
    n = T // C

    def to_chunks(a):
        return a.astype(f32).reshape(B, n, C, H, a.shape[-1]).swapaxes(0, 1)

    causal = jnp.tril(jnp.ones((C, C), dtype=bool))

    def step(S, inp):
        qi, ki, vi, gi = inp
        b = jnp.cumsum(gi, axis=1)
        b_last = b[:, -1]
        o_inter = jnp.einsum('bthk,bhkv->bthv', qi * jnp.exp(b), S)
        diff = b[:, :, None] - b[:, None, :]
        decay = jnp.exp(jnp.where(causal[None, :, :, None, None], diff, -jnp.inf))
        scores = jnp.einsum('bthk,bshk,btshk->bhts', qi, ki, decay)
        o_intra = jnp.einsum('bhts,bshv->bthv', scores, vi)
        S_new = S * jnp.exp(b_last)[..., None] + jnp.einsum(
            'bshk,bshv->bhkv', ki * jnp.exp(b_last[:, None] - b), vi)
        return S_new, o_inter + o_intra

    S, o = lax.scan(step, S0.astype(f32), tuple(map(to_chunks, (q, k, v, logf))))
    return o.swapaxes(0, 1).reshape(B, T, H, V), S


def stick_breaking_attention(q, k, v, bias, q_offset):
    f32 = jnp.float32
    B, Tq, H, Dh = q.shape
    Tk = k.shape[1]
    blk = math.gcd(Tq, SB_BLOCK)
    nb = Tq // blk
    qb = q.astype(f32).reshape(B, nb, blk, H, Dh).swapaxes(0, 1)
    kf = k.astype(f32)
    vf = v.astype(f32)
    bias_f = bias.astype(f32)[None, :, None, None]
    k_pos = jnp.arange(Tk)
    scale = Dh ** -0.5

    def one_block(args):
        qi, bi = args
        q_pos = q_offset + bi * blk + jnp.arange(blk)
        z = jnp.einsum('bqhd,bkhd->bhqk', qi, kf) * scale + bias_f
        strict = k_pos[None, :] < q_pos[:, None]
        log_beta = jnp.where(strict, jax.nn.log_sigmoid(z), -jnp.inf)
        log_keep = jnp.where(strict, jax.nn.log_sigmoid(-z), 0.0)
        after = lax.cumsum(log_keep, axis=3, reverse=True) - log_keep
        w = jnp.exp(log_beta + after)
        return jnp.einsum('bhqk,bkhd->bqhd', w, vf)

    o = lax.map(one_block, (qb, jnp.arange(nb)))
    return o.swapaxes(0, 1).reshape(B, Tq, H, Dh).astype(v.dtype)


def rwkv7_mix(zd, z_prev, S0, mu, w0, w2, a0, a2, g2, k_k, k_a, r_k, ln_w, ln_b):
    f32 = jnp.float32
    B, T, _ = zd.shape
    z = zd.astype(f32)
    prev = jnp.concatenate([z_prev.astype(f32)[:, None], z[:, :-1]], axis=1)
    zs = z + (prev - z) * mu
    r, k, v, w_in, a_in, g_in = split_cols(zs, RWKV_SPLITS)
    w = w0 + jnp.tanh(w_in) @ w2
    log_decay = -jnp.exp(-jax.nn.softplus(-w) - 0.5)
    a = jax.nn.sigmoid(a0 + a_in @ a2)
    g = jax.nn.sigmoid(g_in) @ g2

    def hd(t):
        return t.reshape(B, T, H_D, DH_D)

    r, k, v, a, log_decay = map(hd, (r, k, v, a, log_decay))
    kk = k * k_k.reshape(H_D, DH_D)
    kk = kk * lax.rsqrt(jnp.maximum(jnp.sum(kk * kk, axis=-1, keepdims=True), 1e-24))
    k = k * (1.0 + (a - 1.0) * k_a.reshape(H_D, DH_D))

    def step(S, inp):
        r_t, k_t, v_t, kk_t, a_t, ld_t = inp
        s_kk = jnp.einsum('bhvk,bhk->bhv', S, kk_t)
        S = (S * jnp.exp(ld_t)[:, :, None, :]
             - s_kk[..., None] * (kk_t * a_t)[:, :, None, :]
             + v_t[..., None] * k_t[:, :, None, :])
        return S, jnp.einsum('bhvk,bhk->bhv', S, r_t)

    xs = tuple(t.swapaxes(0, 1) for t in (r, k, v, kk, a, log_decay))
    S, o = lax.scan(step, S0.astype(f32), xs)
    o = o.swapaxes(0, 1)
    mean = jnp.mean(o, axis=-1, keepdims=True)
    var = jnp.mean(jnp.square(o - mean), axis=-1, keepdims=True)
    o = (o - mean) * lax.rsqrt(var + RWKV_LN_EPS) * ln_w.reshape(H_D, DH_D) + ln_b.reshape(H_D, DH_D)
    bonus = jnp.sum(r * k * r_k, axis=-1, keepdims=True) * v
    out = (o + bonus).reshape(B, T, DM_D) * g
    return out, S, zd[:, -1]


def even_mixer(h, S_a, S_b, w_in, w_out, w_gk, b_gk, norm_a, lb, norm_b):
    f32 = jnp.float32
    B, T, _ = h.shape
    z = h @ w_in
    qa, ka, va, gk_in, ga, qb, fb, ib, gb = split_cols(z, EVEN_SPLITS)

    def heads(t, n):
        return t.reshape(B, T, n, -1)

    gk = jax.nn.log_sigmoid((gk_in @ w_gk + b_gk).astype(f32)) / GLA_GK_NORM
    o_a, S_a_new = gated_linear_chunk(heads(qa, H_A) * DK_A ** -0.5, heads(ka, H_A),
                                      heads(va, H_A), heads(gk, H_A), S_a)
    o_a = head_rmsnorm(o_a, norm_a).reshape(B, T, -1) * jax.nn.silu(ga.astype(f32))
    fb32 = fb.astype(f32)
    log_f = jnp.logaddexp(jnp.log(lb), jnp.log1p(-lb) + jax.nn.log_sigmoid(fb32))
    key = (1.0 - lb) * jax.nn.sigmoid(-fb32)
    o_b, S_b_new = gated_linear_chunk(heads(qb, H_B), heads(key, H_B), heads(ib, H_B),
                                      heads(log_f, H_B), S_b)
    o_b = head_rmsnorm(o_b, norm_b).reshape(B, T, -1) * jax.nn.silu(gb.astype(f32))
    out = jnp.concatenate([o_a, o_b], axis=-1).astype(h.dtype) @ w_out
    return out, (S_a_new, S_b_new)


def odd_mixer(h, k_past, v_past, S_d, z_prev, w_in, w_out, sb_bias, mu, w0, w2, a0, a2, g2,
              k_k, k_a, r_k, ln_w, ln_b):
    B, T, _ = h.shape
    z = h @ w_in
    qc, kc, vc, zd = split_cols(z, ODD_SPLITS)
    qc = qc.reshape(B, T, H_C, DH_C)
    kc = kc.reshape(B, T, H_C, DH_C)
    vc = vc.reshape(B, T, H_C, DH_C)
    k_all = jnp.concatenate([k_past.astype(kc.dtype), kc], axis=1)
    v_all = jnp.concatenate([v_past.astype(vc.dtype), vc], axis=1)
    o_c = stick_breaking_attention(qc, k_all, v_all, sb_bias, k_past.shape[1]).reshape(B, T, -1)
    o_d, S_d_new, z_last = rwkv7_mix(zd, z_prev, S_d, mu, w0, w2, a0, a2, g2,
                                     k_k, k_a, r_k, ln_w, ln_b)
    out = jnp.concatenate([o_c.astype(jnp.float32), o_d], axis=-1).astype(h.dtype) @ w_out
    return out, (kc, vc, S_d_new, z_last)


def setup_inputs(seed: int = 0) -> dict:
    key = jax.random.key(seed)
    ks = iter(jax.random.split(key, 64))

    def nrm(shape, scale=1.0):
        return scale * jax.random.normal(next(ks), shape, jnp.float32)

    def gain(shape):
        return 1.0 + 0.02 * jax.random.normal(next(ks), shape, jnp.float32)

    n_pages = PAST_LEN // PAGE_SIZE
    n_used = DEC_BATCH * n_pages
    n_phys = n_used + n_used // 4
    x_prompt = nrm((BATCH, SEQ, D_MODEL))
    x_sample = nrm((DEC_BATCH, DEC_SEQ, D_MODEL))
    p_prompt = nrm((DEPTH, BATCH, SEQ, PLE_DIM))
    p_sample = nrm((DEPTH, DEC_BATCH, DEC_SEQ, PLE_DIM))
    cache_k = nrm((N_ODD, n_phys, PAGE_SIZE, H_C, DH_C))
    cache_v = nrm((N_ODD, n_phys, PAGE_SIZE, H_C, DH_C))
    page_table = jax.random.permutation(next(ks), n_phys)[:n_used].reshape(
        DEC_BATCH, n_pages).astype(jnp.int32)
    return {
        'x_prompt': x_prompt,
        'x_sample': x_sample,
        'p_prompt': p_prompt,
        'p_sample': p_sample,
        'cache_k': cache_k,
        'cache_v': cache_v,
        'page_table': page_table,
        'state_gla': nrm((N_EVEN, DEC_BATCH, H_A, DK_A, DV_A), 0.1),
        'state_hgrn': nrm((N_EVEN, DEC_BATCH, H_B, EXP_B, DV_B), 0.3),
        'state_rwkv': nrm((N_ODD, DEC_BATCH, H_D, DH_D, DH_D), 0.1),
        'state_rwkv_shift': nrm((N_ODD, DEC_BATCH, N_COLS_RWKV)),
        'ln_ffn1': gain((DEPTH, D_MODEL)),
        'ffn1_w_gate': nrm((DEPTH, D_MODEL, D_FF), D_MODEL ** -0.5),
        'ffn1_w_up': nrm((DEPTH, D_MODEL, D_FF), D_MODEL ** -0.5),
        'ffn1_w_down': nrm((DEPTH, D_FF, D_MODEL), D_FF ** -0.5),
        'ln_mix': gain((DEPTH, D_MODEL)),
        'ln_ffn2': gain((DEPTH, D_MODEL)),
        'ffn2_w_gate': nrm((DEPTH, D_MODEL, D_FF), D_MODEL ** -0.5),
        'ffn2_w_up': nrm((DEPTH, D_MODEL, D_FF), D_MODEL ** -0.5),
        'ffn2_w_down': nrm((DEPTH, D_FF, D_MODEL), D_FF ** -0.5),
        'ln_ple': gain((DEPTH, D_MODEL)),
        'ple_w_gate': nrm((DEPTH, D_MODEL, D_MODEL), D_MODEL ** -0.5),
        'ple_w_proj': nrm((DEPTH, PLE_DIM, D_MODEL), PLE_DIM ** -0.5),
        'w_in_even': nrm((N_EVEN, D_MODEL, N_COLS_EVEN), D_MODEL ** -0.5),
        'w_out_even': nrm((N_EVEN, MIX_W, D_MODEL), MIX_W ** -0.5),
        'gla_w_gk': nrm((N_EVEN, GLA_GK_RANK, H_A * DK_A), GLA_GK_RANK ** -0.5),
        'gla_b_gk': nrm((N_EVEN, H_A * DK_A), 0.1),
        'gla_norm': gain((N_EVEN, DV_A)),
        'hgrn_lb_logits': nrm((N_EVEN, H_B * EXP_B), 0.1),
        'hgrn_norm': gain((N_EVEN, DV_B)),
        'w_in_odd': nrm((N_ODD, D_MODEL, N_COLS_ODD), D_MODEL ** -0.5),
        'w_out_odd': nrm((N_ODD, MIX_W, D_MODEL), MIX_W ** -0.5),
        'sb_bias': SB_BIAS_INIT + nrm((N_ODD, H_C), 0.1),
        'rwkv_mu': jax.random.uniform(next(ks), (N_ODD, N_COLS_RWKV), jnp.float32),
        'rwkv_w0': nrm((N_ODD, DM_D), 0.5),
        'rwkv_w2': nrm((N_ODD, R_W, DM_D), 0.5 * R_W ** -0.5),
        'rwkv_a0': nrm((N_ODD, DM_D), 0.1),
        'rwkv_a2': nrm((N_ODD, R_A, DM_D), 0.5 * R_A ** -0.5),
        'rwkv_g2': nrm((N_ODD, R_G, DM_D), R_G ** -0.5),
        'rwkv_k_k': gain((N_ODD, DM_D)),
        'rwkv_k_a': gain((N_ODD, DM_D)),
        'rwkv_r_k': nrm((N_ODD, H_D, DH_D), 0.1),
        'rwkv_ln_w': gain((N_ODD, DM_D)),
        'rwkv_ln_b': nrm((N_ODD, DM_D), 0.01),
        'final_norm': gain((D_MODEL,)),
    }


def reference(x_prompt, x_sample, p_prompt, p_sample, cache_k, cache_v, page_table,
              state_gla, state_hgrn, state_rwkv, state_rwkv_shift,
              ln_ffn1, ffn1_w_gate, ffn1_w_up, ffn1_w_down, ln_mix,
              ln_ffn2, ffn2_w_gate, ffn2_w_up, ffn2_w_down,
              ln_ple, ple_w_gate, ple_w_proj,
              w_in_even, w_out_even, gla_w_gk, gla_b_gk, gla_norm, hgrn_lb_logits, hgrn_norm,
              w_in_odd, w_out_odd, sb_bias, rwkv_mu, rwkv_w0, rwkv_w2, rwkv_a0, rwkv_a2, rwkv_g2,
              rwkv_k_k, rwkv_k_a, rwkv_r_k, rwkv_ln_w, rwkv_ln_b, final_norm):
    f32 = jnp.float32
    bp = x_prompt.shape[0]
    bs = x_sample.shape[0]
    sm = jax.nn.softmax(hgrn_lb_logits.astype(f32), axis=0)
    lower_bounds = jnp.concatenate([jnp.zeros_like(sm[:1]), jnp.cumsum(sm[1:], axis=0)], axis=0)

    def layer(x, p_i, i, mixer):
        x = x + 0.5 * swiglu(rmsnorm(x, ln_ffn1[i]), ffn1_w_gate[i], ffn1_w_up[i], ffn1_w_down[i])
        mix, new_state = mixer(rmsnorm(x, ln_mix[i]))
        x = x + mix
        x = x + 0.5 * swiglu(rmsnorm(x, ln_ffn2[i]), ffn2_w_gate[i], ffn2_w_up[i], ffn2_w_down[i])
        gate = jax.nn.sigmoid(rmsnorm(x, ln_ple[i]) @ ple_w_gate[i])
        x = x + gate * (p_i.astype(x.dtype) @ ple_w_proj[i])
        return x, new_state

    xp, xs = x_prompt, x_sample
    kp_rows, vp_rows, ks_rows, vs_rows = [], [], [], []
    gla_p, gla_s, hgrn_p, hgrn_s = [], [], [], []
    rwkv_p, rwkv_s, shift_p, shift_s = [], [], [], []
    for i in range(DEPTH):
        j = i // 2
        if i % 2 == 0:
            ew = (w_in_even[j], w_out_even[j], gla_w_gk[j], gla_b_gk[j], gla_norm[j],
                  lower_bounds[j], hgrn_norm[j])
            sa0 = jnp.zeros((bp, H_A, DK_A, DV_A), f32)
            sb0 = jnp.zeros((bp, H_B, EXP_B, DV_B), f32)
            xp, (sa_p, sb_p) = layer(xp, p_prompt[i], i, lambda h: even_mixer(h, sa0, sb0, *ew))
            xs, (sa_s, sb_s) = layer(xs, p_sample[i], i,
                                     lambda h: even_mixer(h, state_gla[j], state_hgrn[j], *ew))
            gla_p.append(sa_p)
            gla_s.append(sa_s)
            hgrn_p.append(sb_p)
            hgrn_s.append(sb_s)
        else:
            ow = (w_in_odd[j], w_out_odd[j], sb_bias[j], rwkv_mu[j], rwkv_w0[j], rwkv_w2[j],
                  rwkv_a0[j], rwkv_a2[j], rwkv_g2[j], rwkv_k_k[j], rwkv_k_a[j], rwkv_r_k[j],
                  rwkv_ln_w[j], rwkv_ln_b[j])
            k_empty = jnp.zeros((bp, 0, H_C, DH_C), xp.dtype)
            sd0 = jnp.zeros((bp, H_D, DH_D, DH_D), f32)
            zprev0 = jnp.zeros((bp, N_COLS_RWKV), xp.dtype)
            k_past = cache_k[j][page_table].reshape(bs, -1, H_C, DH_C)
            v_past = cache_v[j][page_table].reshape(bs, -1, H_C, DH_C)
            xp, (kc_p, vc_p, sd_p, zl_p) = layer(
                xp, p_prompt[i], i, lambda h: odd_mixer(h, k_empty, k_empty, sd0, zprev0, *ow))
            xs, (kc_s, vc_s, sd_s, zl_s) = layer(
                xs, p_sample[i], i,
                lambda h: odd_mixer(h, k_past, v_past, state_rwkv[j], state_rwkv_shift[j], *ow))
            kp_rows.append(kc_p)
            vp_rows.append(vc_p)
            ks_rows.append(kc_s)
            vs_rows.append(vc_s)
            rwkv_p.append(sd_p)
            rwkv_s.append(sd_s)
            shift_p.append(zl_p)
            shift_s.append(zl_s)

    y_prompt = rmsnorm(xp, final_norm)
    y_sample = rmsnorm(xs, final_norm)
    return (y_prompt, y_sample,
            jnp.stack(kp_rows), jnp.stack(vp_rows), jnp.stack(ks_rows), jnp.stack(vs_rows),
            jnp.stack(gla_p), jnp.stack(gla_s), jnp.stack(hgrn_p), jnp.stack(hgrn_s),
            jnp.stack(rwkv_p), jnp.stack(rwkv_s), jnp.stack(shift_p), jnp.stack(shift_s))
```

```python
import functools

import jax
import jax.numpy as jnp
from jax import lax
from jax.experimental import pallas as pl
from jax.experimental.pallas import tpu as pltpu

F32 = jnp.float32
BF16 = jnp.bfloat16

D_MODEL = 1024
DEPTH = 4
PAGE_SIZE = 128
HALF = D_MODEL // 2
D_FF = 2816
PLE_DIM = 256
RMS_EPS = 1e-6
H_A = 4
DK_A = HALF // H_A // 2
DV_A = HALF // H_A
GLA_GK_RANK = 16
GLA_GK_NORM = 16.0
H_B = 4
EXP_B = 128
DV_B = HALF // H_B
H_C = 8
DH_C = HALF // H_C
H_D = 8
DH_D = HALF // H_D
DM_D = H_D * DH_D
R_W = 64
R_A = 64
R_G = 128
RWKV_LN_EPS = 64e-5
N_COLS_RWKV = 3 * DM_D + R_W + R_A + R_G
N_COLS_ODD = 3 * HALF + N_COLS_RWKV

LANES = 128
SUBLANES = 8
MXU_DIM = 256
VMEM_LIMIT_BYTES = 56 * 1024 * 1024

TOKEN_TILE = 512
N_EVEN_HEADS = H_A + H_B
N_COLS_EVEN_PAD = (4 * N_EVEN_HEADS + 1) * LANES


def _params(*semantics):
    return pltpu.CompilerParams(dimension_semantics=semantics,
                                vmem_limit_bytes=VMEM_LIMIT_BYTES)


def _resident(shape):
    return pl.BlockSpec(shape, lambda *_: (0,) * len(shape), pipeline_mode=pl.Buffered(1))


def _rms(x, g):
    return x * lax.rsqrt(jnp.mean(x * x, axis=-1, keepdims=True) + RMS_EPS) * g


def _dot(a, b):
    return jnp.dot(a, b, preferred_element_type=F32)


def _ffn_kernel(x_ref, ln_ref, wg_ref, wu_ref, wd_ref, o_ref):
    x = x_ref[...]
    h = _rms(x, ln_ref[...]).astype(BF16)
    acc = jnp.zeros(x.shape, F32)
    for f0 in range(0, D_FF, MXU_DIM):
        g = _dot(h, wg_ref[:, f0:f0 + MXU_DIM])
        u = _dot(h, wu_ref[:, f0:f0 + MXU_DIM])
        a = (g * jax.nn.sigmoid(g) * u).astype(BF16)
        acc = acc + _dot(a, wd_ref[f0:f0 + MXU_DIM, :])
    o_ref[...] = x + 0.5 * acc


def ffn(x, ln, wg, wu, wd):
    nt = x.shape[0]
    tok = pl.BlockSpec((TOKEN_TILE, D_MODEL), lambda i: (i, 0))
    return pl.pallas_call(
        _ffn_kernel,
        grid=(nt // TOKEN_TILE,),
        in_specs=[tok, _resident((1, D_MODEL)), _resident((D_MODEL, D_FF)),
                  _resident((D_MODEL, D_FF)), _resident((D_FF, D_MODEL))],
        out_specs=tok,
        out_shape=jax.ShapeDtypeStruct(x.shape, F32),
        compiler_params=_params("parallel"),
        name="ffn",
    )(x, ln, wg, wu, wd)


def _rms_matmul_kernel(x_ref, ln_ref, w_ref, o_ref, *, n_chunk):
    h = _rms(x_ref[...], ln_ref[...]).astype(BF16)
    n = w_ref.shape[1]
    for n0 in range(0, n, n_chunk):
        n1 = min(n0 + n_chunk, n)
        o_ref[:, n0:n1] = _dot(h, w_ref[:, n0:n1])


def rms_matmul(x, ln, w):
    nt, n = x.shape[0], w.shape[1]
    return pl.pallas_call(
        functools.partial(_rms_matmul_kernel, n_chunk=2 * MXU_DIM),
        grid=(nt // TOKEN_TILE,),
        in_specs=[pl.BlockSpec((TOKEN_TILE, D_MODEL), lambda i: (i, 0)),
                  _resident((1, D_MODEL)), _resident((D_MODEL, n))],
        out_specs=pl.BlockSpec((TOKEN_TILE, n), lambda i: (i, 0)),
        out_shape=jax.ShapeDtypeStruct((nt, n), F32),
        compiler_params=_params("parallel"),
        name="rms_matmul",
    )(x, ln, w)


def _matmul_residual_kernel(x_ref, a_ref, w_ref, o_ref):
    o_ref[...] = x_ref[...] + _dot(a_ref[...].astype(BF16), w_ref[...])


def matmul_residual(x, a, w):
    nt, k = a.shape
    return pl.pallas_call(
        _matmul_residual_kernel,
        grid=(nt // TOKEN_TILE,),
        in_specs=[pl.BlockSpec((TOKEN_TILE, D_MODEL), lambda i: (i, 0)),
                  pl.BlockSpec((TOKEN_TILE, k), lambda i: (i, 0)),
                  _resident((k, D_MODEL))],
        out_specs=pl.BlockSpec((TOKEN_TILE, D_MODEL), lambda i: (i, 0)),
        out_shape=jax.ShapeDtypeStruct(x.shape, F32),
        compiler_params=_params("parallel"),
        name="matmul_residual",
    )(x, a, w)


def _ple_kernel(x_ref, p_ref, ln_ref, wg_ref, wp_ref, o_ref):
    x = x_ref[...]
    h = _rms(x, ln_ref[...]).astype(BF16)
    gate = jax.nn.sigmoid(_dot(h, wg_ref[...]))
    o_ref[...] = x + gate * _dot(p_ref[...].astype(BF16), wp_ref[...])


def ple(x, p, ln, wg, wp):
    nt = x.shape[0]
    tok = pl.BlockSpec((TOKEN_TILE, D_MODEL), lambda i: (i, 0))
    return pl.pallas_call(
        _ple_kernel,
        grid=(nt // TOKEN_TILE,),
        in_specs=[tok, pl.BlockSpec((TOKEN_TILE, PLE_DIM), lambda i: (i, 0)),
                  _resident((1, D_MODEL)), _resident((D_MODEL, D_MODEL)),
                  _resident((PLE_DIM, D_MODEL))],
        out_specs=tok,
        out_shape=jax.ShapeDtypeStruct(x.shape, F32),
        compiler_params=_params("parallel"),
        name="ple",
    )(x, p, ln, wg, wp)


def _final_norm_kernel(x_ref, ln_ref, o_ref):
    o_ref[...] = _rms(x_ref[...], ln_ref[...])


def final_rmsnorm(x, ln):
    nt = x.shape[0]
    tok = pl.BlockSpec((TOKEN_TILE, D_MODEL), lambda i: (i, 0))
    return pl.pallas_call(
        _final_norm_kernel,
        grid=(nt // TOKEN_TILE,),
        in_specs=[tok, _resident((1, D_MODEL))],
        out_specs=tok,
        out_shape=jax.ShapeDtypeStruct(x.shape, F32),
        compiler_params=_params("parallel"),
        name="final_norm",
    )(x, ln)


def _log_sigmoid(x):
    return jnp.minimum(x, 0.0) - jnp.log1p(jnp.exp(-jnp.abs(x)))


def _split_bf16(x):
    hi = x.astype(BF16)
    lo = (x - hi.astype(F32)).astype(BF16)
    return hi, lo


def _gated_chunk(q, k, v, g, st_ref, head, tri):
    c = q.shape[0]
    g_hi, g_lo = _split_bf16(g)
    b = _dot(tri, g_hi) + _dot(tri, g_lo)
    b_last = b[c - 1:c, :]
    st = st_ref[head]
    qe = (q * jnp.exp(b)).astype(BF16)
    o = lax.dot_general(qe, st.astype(BF16), (((1,), (1,)), ((), ())),
                        preferred_element_type=F32)
    row = lax.broadcasted_iota(jnp.int32, (c, 1), 0)
    for s in range(c):
        d = jnp.where(row >= s, b - b[s:s + 1, :], -1e30)
        a_s = jnp.sum(q * k[s:s + 1, :] * jnp.exp(d), axis=-1, keepdims=True)
        o = o + a_s * v[s:s + 1, :]
    ke = (k * jnp.exp(b_last - b)).astype(BF16)
    st_ref[head] = st * jnp.exp(b_last) + lax.dot_general(
        v.astype(BF16), ke, (((0,), (0,)), ((), ())), preferred_element_type=F32)
    return o


def _even_mixer_kernel(z_ref, s0_ref, wgk_ref, bgk_ref, lb_ref, na_ref, nb_ref,
                       o_ref, sout_ref, st_ref, *, t_valid):
    c = z_ref.shape[0]
    ci = pl.program_id(1)

    @pl.when(ci == 0)
    def _():
        st_ref[...] = s0_ref[0]

    def blk(i):
        return z_ref[:, i * LANES:(i + 1) * LANES]

    row = lax.broadcasted_iota(jnp.int32, (c, 1), 0)
    tri = (lax.broadcasted_iota(jnp.int32, (c, c), 0)
           >= lax.broadcasted_iota(jnp.int32, (c, c), 1)).astype(BF16)
    valid = row < t_valid

    def finish(o, gain, gate, head):
        o = o * lax.rsqrt(jnp.mean(o * o, axis=-1, keepdims=True) + RMS_EPS) * gain
        o_ref[:, head * LANES:(head + 1) * LANES] = o * (gate * jax.nn.sigmoid(gate))

    gk = _log_sigmoid(_dot(blk(4 * N_EVEN_HEADS).astype(BF16), wgk_ref[...]) + bgk_ref[...])
    gk = gk * (1.0 / GLA_GK_NORM)
    for h in range(H_A):
        g = gk[:, h * LANES:(h + 1) * LANES]
        k = blk(H_A + h)
        if t_valid < c:
            g = jnp.where(valid, g, 0.0)
            k = jnp.where(valid, k, 0.0)
        o = _gated_chunk(blk(h) * DK_A ** -0.5, k, blk(2 * H_A + h), g, st_ref, h, tri)
        finish(o, na_ref[...], blk(3 * H_A + h), h)

    base = 4 * H_A
    for h in range(H_B):
        lb = lb_ref[:, h * LANES:(h + 1) * LANES]
        fb = blk(base + H_B + h)
        a = jnp.log1p(-lb) + _log_sigmoid(fb)
        log_lb = jnp.log(lb)
        g = jnp.maximum(a, log_lb) + jnp.log1p(jnp.exp(-jnp.abs(a - log_lb)))
        k = (1.0 - lb) * jax.nn.sigmoid(-fb)
        if t_valid < c:
            g = jnp.where(valid, g, 0.0)
            k = jnp.where(valid, k, 0.0)
        o = _gated_chunk(blk(base + h), k, blk(base + 2 * H_B + h), g, st_ref, H_A + h, tri)
        finish(o, nb_ref[...], blk(base + 3 * H_B + h), H_A + h)

    @pl.when(ci == pl.num_programs(1) - 1)
    def _():
        sout_ref[0] = st_ref[...]


def even_mixer(z, s0, wgk, bgk, lb, norm_a, norm_b, *, n_seq, t, chunk, t_valid=None):
    ncol = z.shape[1]
    t_valid = chunk if t_valid is None else t_valid
    n_chunks = t // chunk
    st_shape = (N_EVEN_HEADS, LANES, LANES)
    return pl.pallas_call(
        functools.partial(_even_mixer_kernel, t_valid=t_valid),
        grid=(n_seq, n_chunks),
        in_specs=[pl.BlockSpec((chunk, ncol), lambda b, c: (b * n_chunks + c, 0)),
                  pl.BlockSpec((1,) + st_shape, lambda b, c: (b, 0, 0, 0)),
                  _resident(wgk.shape), _resident(bgk.shape), _resident(lb.shape),
                  _resident(norm_a.shape), _resident(norm_b.shape)],
        out_specs=[pl.BlockSpec((chunk, D_MODEL), lambda b, c: (b * n_chunks + c, 0)),
                   pl.BlockSpec((1,) + st_shape, lambda b, c: (b, 0, 0, 0))],
        out_shape=[jax.ShapeDtypeStruct((n_seq * t, D_MODEL), F32),
                   jax.ShapeDtypeStruct((n_seq,) + st_shape, F32)],
        scratch_shapes=[pltpu.VMEM(st_shape, F32)],
        compiler_params=_params("parallel", "arbitrary"),
        name="even_mixer",
    )(z, s0, wgk, bgk, lb, norm_a, norm_b)


def _pad_heads(w, n_heads):
    d = w.shape[-1] // n_heads
    w = w.reshape(w.shape[:-1] + (n_heads, d))
    w = jnp.pad(w, [(0, 0)] * (w.ndim - 1) + [(0, LANES - d)])
    return w.reshape(w.shape[:-2] + (n_heads * LANES,))


def _split_cols(w, sizes):
    out, o = [], 0
    for s in sizes:
        out.append(w[..., o:o + s])
        o += s
    return out


def even_weights(w_in, w_gk, b_gk):
    ka = H_A * DK_A
    qa, kk, va, gk_in, ga, qb, fb, ib, gb = _split_cols(
        w_in, (ka, ka, HALF, GLA_GK_RANK, HALF, HALF, HALF, HALF, HALF))
    gk_in = jnp.pad(gk_in, ((0, 0), (0, LANES - GLA_GK_RANK)))
    w = jnp.concatenate([_pad_heads(qa, H_A), _pad_heads(kk, H_A), va, ga, qb, fb, ib, gb, gk_in],
                        axis=-1).astype(BF16)
    wgk = jnp.pad(_pad_heads(w_gk, H_A), ((0, LANES - GLA_GK_RANK), (0, 0))).astype(BF16)
    bgk = _pad_heads(b_gk[None, :], H_A)
    return w, wgk, bgk


def run_even_mixer(z, n_p, t_p, n_s, t_s, state_gla, state_hgrn, wgk, bgk, lb, norm_a, norm_b,
                   *, chunk):
    ncol = z.shape[1]
    args = (wgk, bgk, lb[None, :], norm_a[None, :], norm_b[None, :])
    zero = jnp.zeros((n_p, N_EVEN_HEADS, LANES, LANES), F32)
    o_p, st_p = even_mixer(z, zero, *args, n_seq=n_p, t=t_p, chunk=chunk)
    zs = z[n_p * t_p:].reshape(n_s, t_s, ncol)
    zs = jnp.pad(zs, ((0, 0), (0, SUBLANES - t_s), (0, 0))).reshape(n_s * SUBLANES, ncol)
    s0 = jnp.concatenate([
        jnp.pad(jnp.swapaxes(state_gla, -1, -2), ((0, 0), (0, 0), (0, 0), (0, LANES - DK_A))),
        jnp.swapaxes(state_hgrn, -1, -2)], axis=1)
    o_s, st_s = even_mixer(zs, s0, *args, n_seq=n_s, t=SUBLANES, chunk=SUBLANES, t_valid=t_s)
    o_s = o_s.reshape(n_s, SUBLANES, D_MODEL)[:, :t_s].reshape(n_s * t_s, D_MODEL)

    def unpack(st):
        gla = jnp.swapaxes(st[:, :H_A], -1, -2)[:, :, :DK_A, :]
        return gla, jnp.swapaxes(st[:, H_A:], -1, -2)

    return jnp.concatenate([o_p, o_s], axis=0), unpack(st_p), unpack(st_s)


_NT_DIMS = (((1,), (1,)), ((), ()))


def _neg_softplus(z):
    return -(jnp.maximum(z, 0.0) + jnp.log1p(jnp.exp(-jnp.abs(z))))


def _suffix_tri(n):
    return (lax.broadcasted_iota(jnp.int32, (n, n), 0)
            >= lax.broadcasted_iota(jnp.int32, (n, n), 1)).astype(BF16)


def _sb_prompt_kernel(bias_ref, q_ref, k_ref, v_ref, o_ref, *, blk):
    pair = pl.program_id(1)
    qi = pl.program_id(2)
    lane = lax.broadcasted_iota(jnp.int32, (1, LANES), 1)
    in_head = [lane < DH_C, lane >= DH_C]
    q2 = q_ref[...] * DH_C ** -0.5
    qh = [jnp.where(m, q2, 0.0).astype(BF16) for m in in_head]
    bias = [bias_ref[2 * pair], bias_ref[2 * pair + 1]]
    tri = _suffix_tri(blk)
    strict = (lax.broadcasted_iota(jnp.int32, (blk, blk), 1)
              < lax.broadcasted_iota(jnp.int32, (blk, blk), 0))

    def tile(j, state, diagonal):
        kb = k_ref[pl.ds(pl.multiple_of(j * blk, blk), blk), :].astype(BF16)
        vb = v_ref[pl.ds(pl.multiple_of(j * blk, blk), blk), :].astype(BF16)
        new = []
        for hh in range(2):
            carry, acc = state[hh]
            z = lax.dot_general(qh[hh], kb, _NT_DIMS, preferred_element_type=F32) + bias[hh]
            lk = _neg_softplus(z)
            if diagonal:
                lk = jnp.where(strict, lk, 0.0)
            incl = _dot(lk.astype(BF16), tri)
            w = jnp.exp(z + incl + carry)
            if diagonal:
                w = jnp.where(strict, w, 0.0)
            new.append((carry + incl[:, 0:1], acc + _dot(w.astype(BF16), vb)))
        return tuple(new)

    zero = (jnp.zeros((blk, 1), F32), jnp.zeros((blk, LANES), F32))
    state = tile(qi, (zero, zero), True)
    state = lax.fori_loop(0, qi, lambda it, st: tile(qi - 1 - it, st, False), state)
    o_ref[...] = jnp.where(in_head[0], state[0][1], state[1][1])


def sb_attention_prompt(z, bias, *, n_seq, t, q_col, blk):
    nq = t // blk
    n_pairs = H_C // 2
    return pl.pallas_call(
        functools.partial(_sb_prompt_kernel, blk=blk),
        grid=(n_seq, n_pairs, nq),
        in_specs=[pl.BlockSpec(memory_space=pltpu.SMEM),
                  pl.BlockSpec((blk, LANES), lambda b, p, i: (b * nq + i, q_col + p)),
                  pl.BlockSpec((t, LANES), lambda b, p, i: (b, q_col + n_pairs + p)),
                  pl.BlockSpec((t, LANES), lambda b, p, i: (b, q_col + 2 * n_pairs + p))],
        out_specs=pl.BlockSpec((blk, LANES), lambda b, p, i: (b * nq + i, p)),
        out_shape=jax.ShapeDtypeStruct((n_seq * t, HALF), F32),
        compiler_params=_params("parallel", "parallel", "arbitrary"),
        name="sb_attention_prompt",
    )(bias, z, z, z)


def _sb_sample_kernel(pt_ref, q_ref, kn_ref, vn_ref, bias_ref, *refs, n_pages, t_s):
    del pt_ref
    k_pages, v_pages, o_ref = refs[:n_pages], refs[n_pages:2 * n_pages], refs[2 * n_pages]
    rows = t_s * H_C
    q = q_ref[0] * DH_C ** -0.5
    qx = jnp.concatenate([jnp.broadcast_to(q[i:i + 1, :], (H_C, HALF)) for i in range(t_s)], axis=0)
    r_id = lax.broadcasted_iota(jnp.int32, (rows, HALF), 0)
    l_id = lax.broadcasted_iota(jnp.int32, (rows, HALF), 1)
    head_lanes = (l_id // DH_C) == (r_id % H_C)
    qx = jnp.where(head_lanes, qx, 0.0)
    bias = bias_ref[...]
    q_idx = lax.broadcasted_iota(jnp.int32, (rows, 1), 0) // H_C

    carry = jnp.zeros((rows, 1), F32)
    acc = jnp.zeros((rows, HALF), F32)
    kn, vn = kn_ref[0], vn_ref[0]
    for j in reversed(range(t_s)):
        visible = q_idx > j
        z = jnp.sum(qx * kn[j:j + 1, :], axis=-1, keepdims=True) + bias[:, 0:1]
        lk = jnp.where(visible, _neg_softplus(z), 0.0)
        w = jnp.where(visible, jnp.exp(z + lk + carry), 0.0)
        acc = acc + w * vn[j:j + 1, :]
        carry = carry + lk

    qx = qx.astype(BF16)
    tri = _suffix_tri(PAGE_SIZE)
    for p in reversed(range(n_pages)):
        kb = k_pages[p][...].astype(BF16)
        vb = v_pages[p][...].astype(BF16)
        z = lax.dot_general(qx, kb, _NT_DIMS, preferred_element_type=F32) + bias
        incl = _dot(_neg_softplus(z).astype(BF16), tri)
        w = jnp.exp(z + incl + carry)
        acc = acc + _dot(w.astype(BF16), vb)
        carry = carry + incl[:, 0:1]

    acc = jnp.where(head_lanes, acc, 0.0)
    for i in range(t_s):
        o_ref[0, i:i + 1, :] = jnp.sum(acc[i * H_C:(i + 1) * H_C, :], axis=0, keepdims=True)


def sb_attention_sample(q, k_new, v_new, cache_k, cache_v, layer, page_table, bias_rows):
    n_seq, t_s, _ = q.shape
    n_pages = page_table.shape[0] // n_seq
    tok = pl.BlockSpec((1, t_s, HALF), lambda b, pt: (b, 0, 0))

    def page_spec(p):
        return pl.BlockSpec((None, None, PAGE_SIZE, HALF),
                            lambda b, pt: (layer, pt[b * n_pages + p], 0, 0))

    pages = [page_spec(p) for p in range(n_pages)]
    return pl.pallas_call(
        functools.partial(_sb_sample_kernel, n_pages=n_pages, t_s=t_s),
        grid_spec=pltpu.PrefetchScalarGridSpec(
            num_scalar_prefetch=1,
            grid=(n_seq,),
            in_specs=[tok, tok, tok,
                      pl.BlockSpec(bias_rows.shape, lambda b, pt: (0, 0))] + pages + pages,
            out_specs=tok),
        out_shape=jax.ShapeDtypeStruct((n_seq, t_s, HALF), F32),
        compiler_params=_params("parallel"),
        name="sb_attention_sample",
    )(page_table, q, k_new, v_new, bias_rows, *([cache_k] * n_pages), *([cache_v] * n_pages))


N_SCAN_ROWS = 7
N_PAIRS_D = H_D // 2


def _head_sum(x, seg):
    hi, lo = _split_bf16(x)
    return _dot(hi, seg) + _dot(lo, seg)


def _rwkv_prep_kernel(zd_ref, zb_ref, prev_s_ref, mu_ref, w0_ref, w2_ref, a0_ref, a2_ref, g2_ref,
                      kk_ref, ka_ref, rk_ref, seg_ref, rows_ref, v_ref, gb_ref,
                      *, tiles_per_seq, n_prompt_tiles):
    i = pl.program_id(0)
    z = zd_ref[...]
    tm = z.shape[0]
    boundary = jnp.where(i % tiles_per_seq == 0, 0.0, 1.0) * zb_ref[SUBLANES - 1:SUBLANES, :]
    row = lax.broadcasted_iota(jnp.int32, (tm, 1), 0)
    prev = jnp.where(row == 0, boundary, pltpu.roll(z, 1, axis=0))
    prev = jnp.where(i >= n_prompt_tiles, prev_s_ref[...], prev)
    zs = z + (prev - z) * mu_ref[...]
    r, k, v = zs[:, 0:DM_D], zs[:, DM_D:2 * DM_D], zs[:, 2 * DM_D:3 * DM_D]
    u = zs[:, 3 * DM_D:3 * DM_D + R_W + R_A]
    g_in = zs[:, 3 * DM_D + R_W + R_A:]
    seg = seg_ref[...]
    w = w0_ref[...] + _dot(jnp.tanh(u).astype(BF16), w2_ref[...])
    decay = jnp.exp(-jnp.exp(_log_sigmoid(w) - 0.5))
    a = jax.nn.sigmoid(a0_ref[...] + _dot(u.astype(BF16), a2_ref[...]))
    g = _dot(jax.nn.sigmoid(g_in).astype(BF16), g2_ref[...])
    kk = k * kk_ref[...]
    kk = kk * lax.rsqrt(jnp.maximum(_head_sum(kk * kk, seg), 1e-24))
    kp = k * (1.0 + (a - 1.0) * ka_ref[...])
    beta = kk * a
    outs = (decay, kk, beta, kp, decay * r, _head_sum(beta * r, seg), _head_sum(kp * r, seg))
    for n, val in enumerate(outs):
        rows_ref[:, n * DM_D:(n + 1) * DM_D] = val
    v_ref[...] = v
    gb_ref[:, 0:DM_D] = g
    gb_ref[:, DM_D:] = _head_sum(r * kp * rk_ref[...], seg) * v


def rwkv_prep(z, prev_s, mu, w0, w2, a0, a2, g2, k_k, k_a, r_k, seg, *, n_prompt_tok, t_prompt):
    nt = z.shape[0]
    tm = TOKEN_TILE
    vec = _resident((1, DM_D))
    tok = lambda n: pl.BlockSpec((tm, n), lambda i: (i, 0))
    return pl.pallas_call(
        functools.partial(_rwkv_prep_kernel, tiles_per_seq=t_prompt // tm,
                          n_prompt_tiles=n_prompt_tok // tm),
        grid=(nt // tm,),
        in_specs=[tok(N_COLS_RWKV),
                  pl.BlockSpec((SUBLANES, N_COLS_RWKV),
                               lambda i: (jnp.maximum(i * (tm // SUBLANES) - 1, 0), 0)),
                  _resident(prev_s.shape), _resident((1, N_COLS_RWKV)),
                  vec, _resident(w2.shape), vec, _resident(a2.shape), _resident(g2.shape),
                  vec, vec, vec, _resident(seg.shape)],
        out_specs=[tok(N_SCAN_ROWS * DM_D), tok(DM_D), tok(2 * DM_D)],
        out_shape=[jax.ShapeDtypeStruct((nt, N_SCAN_ROWS * DM_D), F32),
                   jax.ShapeDtypeStruct((nt, DM_D), F32),
                   jax.ShapeDtypeStruct((nt, 2 * DM_D), F32)],
        compiler_params=_params("parallel"),
        name="rwkv_prep",
    )(z, z, prev_s, mu, w0, w2, a0, a2, g2, k_k, k_a, r_k, seg)


def _rwkv_scan_kernel(*refs, nb, t_c, rows_per_seq):
    n_rows = nb if rows_per_seq else 1
    rows_refs = refs[:n_rows]
    vt_ref, s0_ref, ot_ref, sout_ref, s_scr = refs[n_rows:]
    ci = pl.program_id(1)

    @pl.when(ci == 0)
    def _():
        s_scr[...] = s0_ref[...]

    ot_ref[...] = jnp.zeros(ot_ref.shape, F32)
    lane = lax.broadcasted_iota(jnp.int32, (1, LANES), 1)
    ones_bd = ((lax.broadcasted_iota(jnp.int32, (LANES, LANES), 0) // DH_D)
               == (lax.broadcasted_iota(jnp.int32, (LANES, LANES), 1) // DH_D)).astype(BF16)

    def body(t, carry):
        here = (lane % DH_D) == t
        here_f = here.astype(F32)
        for kq in range(nb):
            if rows_per_seq:
                row = rows_refs[kq][pl.ds(t, 1), :]
            else:
                row = rows_refs[0][kq, pl.ds(t, 1), :]
            for p in range(N_PAIRS_D):
                w, kap, beta, kp, wr, c1, c2 = (
                    row[:, n * DM_D + p * LANES:n * DM_D + (p + 1) * LANES]
                    for n in range(N_SCAN_ROWS))
                sp = s_scr[kq, p]
                lhs = jnp.concatenate([sp * kap, sp * wr, vt_ref[kq, 0, p] * here_f], axis=0)
                red = _dot(lhs.astype(BF16), ones_bd)
                skk, pr, vcol = red[0:DH_D], red[DH_D:2 * DH_D], red[2 * DH_D:3 * DH_D]
                s_scr[kq, p] = sp * w - skk * beta + vcol * kp
                o = pr - skk * c1 + vcol * c2
                ot_ref[kq, 0, p] = jnp.where(here, o, ot_ref[kq, 0, p])
        return carry

    lax.fori_loop(0, t_c, body, 0)

    @pl.when(ci == pl.num_programs(1) - 1)
    def _():
        sout_ref[...] = s_scr[...]


def rwkv_scan(rows, vt, s0, *, nb):
    n_seq, n_chunks = vt.shape[:2]
    rows_per_seq = rows.ndim == 2
    t_c = DH_D if rows_per_seq else rows.shape[1]
    ncol = rows.shape[-1]
    if rows_per_seq:
        row_specs = [pl.BlockSpec((t_c, ncol),
                                  lambda g, c, kq=kq: ((g * nb + kq) * n_chunks + c, 0))
                     for kq in range(nb)]
        row_args = [rows] * nb
    else:
        row_specs = [pl.BlockSpec((nb, t_c, ncol), lambda g, c: (g, 0, 0))]
        row_args = [rows]
    vt_spec = pl.BlockSpec((nb, 1) + vt.shape[2:], lambda g, c: (g, c, 0, 0, 0))
    st_spec = pl.BlockSpec((nb,) + s0.shape[1:], lambda g, c: (g, 0, 0, 0))
    return pl.pallas_call(
        functools.partial(_rwkv_scan_kernel, nb=nb, t_c=t_c, rows_per_seq=rows_per_seq),
        grid=(n_seq // nb, n_chunks),
        in_specs=row_specs + [vt_spec, st_spec],
        out_specs=[vt_spec, st_spec],
        out_shape=[jax.ShapeDtypeStruct(vt.shape, F32), jax.ShapeDtypeStruct(s0.shape, F32)],
        scratch_shapes=[pltpu.VMEM((nb,) + s0.shape[1:], F32)],
        compiler_params=_params("parallel", "arbitrary"),
        name="rwkv_scan",
    )(*row_args, vt, s0)


def _odd_out_kernel(x_ref, oc_ref, od_ref, gb_ref, lnw_ref, lnb_ref, seg_ref, w_ref, o_ref):
    seg = seg_ref[...]
    o = od_ref[...]
    d = o - _head_sum(o, seg) * (1.0 / DH_D)
    var = _head_sum(d * d, seg) * (1.0 / DH_D)
    od = d * lax.rsqrt(var + RWKV_LN_EPS) * lnw_ref[...] + lnb_ref[...]
    od = (od + gb_ref[:, DM_D:]) * gb_ref[:, 0:DM_D]
    a = jnp.concatenate([oc_ref[...], od], axis=-1).astype(BF16)
    o_ref[...] = x_ref[...] + _dot(a, w_ref[...])


def odd_out(x, o_c, o_d, gb, ln_w, ln_b, seg, w_out):
    nt = x.shape[0]
    tok = lambda n: pl.BlockSpec((TOKEN_TILE, n), lambda i: (i, 0))
    vec = _resident((1, DM_D))
    return pl.pallas_call(
        _odd_out_kernel,
        grid=(nt // TOKEN_TILE,),
        in_specs=[tok(D_MODEL), tok(HALF), tok(DM_D), tok(2 * DM_D), vec, vec,
                  _resident(seg.shape), _resident(w_out.shape)],
        out_specs=tok(D_MODEL),
        out_shape=jax.ShapeDtypeStruct(x.shape, F32),
        compiler_params=_params("parallel"),
        name="odd_out",
    )(x, o_c, o_d, gb, ln_w, ln_b, seg, w_out)


def _to_pair_lanes(a, t_c):
    n_seq, t, _ = a.shape
    a = a.reshape(n_seq, t // t_c, t_c, N_PAIRS_D, 2, DH_D)
    a = jnp.transpose(a, (0, 1, 3, 5, 4, 2))
    a = jnp.pad(a, [(0, 0)] * 5 + [(0, DH_D - t_c)])
    return a.reshape(n_seq, t // t_c, N_PAIRS_D, DH_D, LANES)


def _from_pair_lanes(a, t_c):
    n_seq, n_chunks = a.shape[:2]
    a = a.reshape(n_seq, n_chunks, N_PAIRS_D, DH_D, 2, DH_D)[..., :t_c]
    a = jnp.transpose(a, (0, 1, 5, 2, 4, 3))
    return a.reshape(n_seq * n_chunks * t_c, DM_D)


def _state_to_pairs(s):
    n_seq = s.shape[0]
    s = s.reshape(n_seq, N_PAIRS_D, 2, DH_D, DH_D)
    return jnp.transpose(s, (0, 1, 3, 2, 4)).reshape(n_seq, N_PAIRS_D, DH_D, LANES)


def _state_from_pairs(s):
    n_seq = s.shape[0]
    s = s.reshape(n_seq, N_PAIRS_D, DH_D, 2, DH_D)
    return jnp.transpose(s, (0, 1, 3, 2, 4)).reshape(n_seq, H_D, DH_D, DH_D)


def run_rwkv(z, n_p, t_p, n_s, t_s, state, shift, mu, w0, w2, a0, a2, g2, k_k, k_a, r_k, seg,
             *, nb_sample):
    np_tok = n_p * t_p
    zd_s = z[np_tok:, :N_COLS_RWKV].reshape(n_s, t_s, N_COLS_RWKV)
    prev_s = jnp.concatenate([shift[:, None, :], zd_s[:, :-1]], axis=1).reshape(n_s * t_s, N_COLS_RWKV)
    w2p = jnp.pad(w2, ((0, R_A), (0, 0))).astype(BF16)
    a2p = jnp.pad(a2, ((R_W, 0), (0, 0))).astype(BF16)
    row = lambda a: a.reshape(1, -1)
    rows, v, gb = rwkv_prep(z, prev_s, row(mu), row(w0), w2p, row(a0), a2p, g2.astype(BF16),
                            row(k_k), row(k_a), row(r_k), seg, n_prompt_tok=np_tok, t_prompt=t_p)
    vt_p = _to_pair_lanes(v[:np_tok].reshape(n_p, t_p, DM_D), DH_D)
    zero = jnp.zeros((n_p, N_PAIRS_D, DH_D, LANES), F32)
    ot_p, st_p = rwkv_scan(rows, vt_p, zero, nb=n_p)
    vt_s = _to_pair_lanes(v[np_tok:].reshape(n_s, t_s, DM_D), t_s)
    rows_s = rows[np_tok:].reshape(n_s, t_s, N_SCAN_ROWS * DM_D)
    ot_s, st_s = rwkv_scan(rows_s, vt_s, _state_to_pairs(state), nb=nb_sample)
    o = jnp.concatenate([_from_pair_lanes(ot_p, DH_D), _from_pair_lanes(ot_s, t_s)], axis=0)
    return o, gb, _state_from_pairs(st_p), _state_from_pairs(st_s)


EVEN_CHUNK = 16
SB_BLOCK = 256
RWKV_SAMPLE_GROUP = 8


def kernel(x_prompt, x_sample, p_prompt, p_sample, cache_k, cache_v, page_table, state_gla, state_hgrn, state_rwkv, state_rwkv_shift, ln_ffn1, ffn1_w_gate, ffn1_w_up, ffn1_w_down, ln_mix, ln_ffn2, ffn2_w_gate, ffn2_w_up, ffn2_w_down, ln_ple, ple_w_gate, ple_w_proj, w_in_even, w_out_even, gla_w_gk, gla_b_gk, gla_norm, hgrn_lb_logits, hgrn_norm, w_in_odd, w_out_odd, sb_bias, rwkv_mu, rwkv_w0, rwkv_w2, rwkv_a0, rwkv_a2, rwkv_g2, rwkv_k_k, rwkv_k_a, rwkv_r_k, rwkv_ln_w, rwkv_ln_b, final_norm):
    n_p, t_p, _ = x_prompt.shape
    n_s, t_s, _ = x_sample.shape
    np_tok, ns_tok = n_p * t_p, n_s * t_s
    assert ns_tok == TOKEN_TILE and t_p % TOKEN_TILE == 0 and t_s < SUBLANES
    bf = lambda w: w.astype(BF16)
    row = lambda a: a.reshape(1, -1)

    x = jnp.concatenate([x_prompt.reshape(np_tok, D_MODEL), x_sample.reshape(ns_tok, D_MODEL)], axis=0)
    p_all = jnp.concatenate([p_prompt.reshape(DEPTH, np_tok, PLE_DIM),
                             p_sample.reshape(DEPTH, ns_tok, PLE_DIM)], axis=1)
    sm = jax.nn.softmax(hgrn_lb_logits.astype(F32), axis=0)
    lower_bounds = jnp.concatenate([jnp.zeros_like(sm[:1]), jnp.cumsum(sm[1:], axis=0)], axis=0)
    head_id = jnp.arange(DM_D) // DH_D
    seg = (head_id[:, None] == head_id[None, :]).astype(BF16)
    n_phys = cache_k.shape[1]
    ck = cache_k.reshape(cache_k.shape[0], n_phys, PAGE_SIZE, HALF)
    cv = cache_v.reshape(cache_v.shape[0], n_phys, PAGE_SIZE, HALF)
    pt_flat = page_table.reshape(-1)
    q0 = N_COLS_RWKV
    k0, v0 = q0 + HALF, q0 + 2 * HALF

    k_rows_p, v_rows_p, k_rows_s, v_rows_s = [], [], [], []
    gla_p, gla_s, hgrn_p, hgrn_s = [], [], [], []
    rwkv_p, rwkv_s, shift_p, shift_s = [], [], [], []
    for i in range(DEPTH):
        j = i // 2
        x = ffn(x, row(ln_ffn1[i]), bf(ffn1_w_gate[i]), bf(ffn1_w_up[i]), bf(ffn1_w_down[i]))
        if i % 2 == 0:
            w_in, wgk, bgk = even_weights(w_in_even[j], gla_w_gk[j], gla_b_gk[j])
            z = rms_matmul(x, row(ln_mix[i]), w_in)
            o, (ga_p, hb_p), (ga_s, hb_s) = run_even_mixer(
                z, n_p, t_p, n_s, t_s, state_gla[j], state_hgrn[j], wgk, bgk,
                lower_bounds[j], gla_norm[j], hgrn_norm[j], chunk=EVEN_CHUNK)
            x = matmul_residual(x, o, bf(w_out_even[j]))
            gla_p.append(ga_p)
            gla_s.append(ga_s)
            hgrn_p.append(hb_p)
            hgrn_s.append(hb_s)
        else:
            w = w_in_odd[j]
            z = rms_matmul(x, row(ln_mix[i]), bf(jnp.concatenate([w[:, 3 * HALF:], w[:, :3 * HALF]], axis=1)))
            oc_p = sb_attention_prompt(z, sb_bias[j], n_seq=n_p, t=t_p, q_col=q0 // LANES, blk=SB_BLOCK)
            z_s = z[np_tok:]
            q_s, kc_s, vc_s = (z_s[:, c:c + HALF].reshape(n_s, t_s, HALF) for c in (q0, k0, v0))
            bias_rows = jnp.broadcast_to(jnp.tile(sb_bias[j], t_s)[:, None], (t_s * H_C, PAGE_SIZE))
            oc_s = sb_attention_sample(q_s, kc_s, vc_s, ck, cv, j, pt_flat, bias_rows)
            o_c = jnp.concatenate([oc_p, oc_s.reshape(ns_tok, HALF)], axis=0)
            o_d, gb, sd_p, sd_s = run_rwkv(
                z, n_p, t_p, n_s, t_s, state_rwkv[j], state_rwkv_shift[j], rwkv_mu[j], rwkv_w0[j],
                rwkv_w2[j], rwkv_a0[j], rwkv_a2[j], rwkv_g2[j], rwkv_k_k[j], rwkv_k_a[j],
                rwkv_r_k[j], seg, nb_sample=RWKV_SAMPLE_GROUP)
            x = odd_out(x, o_c, o_d, gb, row(rwkv_ln_w[j]), row(rwkv_ln_b[j]), seg, bf(w_out_odd[j]))
            z_p = z[:np_tok].reshape(n_p, t_p, -1)
            k_rows_p.append(z_p[:, :, k0:k0 + HALF].reshape(n_p, t_p, H_C, DH_C))
            v_rows_p.append(z_p[:, :, v0:v0 + HALF].reshape(n_p, t_p, H_C, DH_C))
            k_rows_s.append(kc_s.reshape(n_s, t_s, H_C, DH_C))
            v_rows_s.append(vc_s.reshape(n_s, t_s, H_C, DH_C))
            rwkv_p.append(sd_p)
            rwkv_s.append(sd_s)
            shift_p.append(z_p[:, -1, :N_COLS_RWKV])
            shift_s.append(z_s.reshape(n_s, t_s, -1)[:, -1, :N_COLS_RWKV])
        x = ffn(x, row(ln_ffn2[i]), bf(ffn2_w_gate[i]), bf(ffn2_w_up[i]), bf(ffn2_w_down[i]))
        x = ple(x, p_all[i], row(ln_ple[i]), bf(ple_w_gate[i]), bf(ple_w_proj[i]))

    y = final_rmsnorm(x, row(final_norm))
    return (y[:np_tok].reshape(n_p, t_p, D_MODEL), y[np_tok:].reshape(n_s, t_s, D_MODEL),
            jnp.stack(k_rows_p), jnp.stack(v_rows_p), jnp.stack(k_rows_s), jnp.stack(v_rows_s),
            jnp.stack(gla_p), jnp.stack(gla_s), jnp.stack(hgrn_p), jnp.stack(hgrn_s),
            jnp.stack(rwkv_p), jnp.stack(rwkv_s), jnp.stack(shift_p), jnp.stack(shift_s))
```

```python
import functools

import jax
import jax.numpy as jnp
from jax import lax
from jax.experimental import pallas as pl
from jax.experimental.pallas import tpu as pltpu

F32 = jnp.float32
BF16 = jnp.bfloat16

D_MODEL = 1024
DEPTH = 4
PAGE_SIZE = 128
HALF = D_MODEL // 2
D_FF = 2816
PLE_DIM = 256
RMS_EPS = 1e-6
H_A = 4
DK_A = HALF // H_A // 2
DV_A = HALF // H_A
GLA_GK_RANK = 16
GLA_GK_NORM = 16.0
H_B = 4
EXP_B = 128
DV_B = HALF // H_B
H_C = 8
DH_C = HALF // H_C
H_D = 8
DH_D = HALF // H_D
DM_D = H_D * DH_D
R_W = 64
R_A = 64
R_G = 128
RWKV_LN_EPS = 64e-5
N_COLS_RWKV = 3 * DM_D + R_W + R_A + R_G
N_COLS_ODD = 3 * HALF + N_COLS_RWKV

LANES = 128
SUBLANES = 8
MXU_DIM = 256
VMEM_LIMIT_BYTES = 56 * 1024 * 1024

TOKEN_TILE = 512
N_EVEN_HEADS = H_A + H_B
N_COLS_EVEN_PAD = (4 * N_EVEN_HEADS + 1) * LANES


def _params(*semantics):
    return pltpu.CompilerParams(dimension_semantics=semantics,
                                vmem_limit_bytes=VMEM_LIMIT_BYTES)


def _resident(shape):
    return pl.BlockSpec(shape, lambda *_: (0,) * len(shape), pipeline_mode=pl.Buffered(1))


def _rms(x, g):
    return x * lax.rsqrt(jnp.mean(x * x, axis=-1, keepdims=True) + RMS_EPS) * g


def _dot(a, b):
    return jnp.dot(a, b, preferred_element_type=F32)


def _ffn_kernel(x_ref, ln_ref, wg_ref, wu_ref, wd_ref, o_ref):
    x = x_ref[...]
    h = _rms(x, ln_ref[...]).astype(BF16)
    acc = jnp.zeros(x.shape, F32)
    for f0 in range(0, D_FF, MXU_DIM):
        g = _dot(h, wg_ref[:, f0:f0 + MXU_DIM])
        u = _dot(h, wu_ref[:, f0:f0 + MXU_DIM])
        a = (g * jax.nn.sigmoid(g) * u).astype(BF16)
        acc = acc + _dot(a, wd_ref[f0:f0 + MXU_DIM, :])
    o_ref[...] = x + 0.5 * acc


def ffn(x, ln, wg, wu, wd):
    nt = x.shape[0]
    tok = pl.BlockSpec((TOKEN_TILE, D_MODEL), lambda i: (i, 0))
    return pl.pallas_call(
        _ffn_kernel,
        grid=(nt // TOKEN_TILE,),
        in_specs=[tok, _resident((1, D_MODEL)), _resident((D_MODEL, D_FF)),
                  _resident((D_MODEL, D_FF)), _resident((D_FF, D_MODEL))],
        out_specs=tok,
        out_shape=jax.ShapeDtypeStruct(x.shape, F32),
        compiler_params=_params("parallel"),
        name="ffn",
    )(x, ln, wg, wu, wd)


def _rms_matmul_kernel(x_ref, ln_ref, w_ref, o_ref, *maybe_ob_ref, n_chunk, bf16_from):
    h = _rms(x_ref[...], ln_ref[...]).astype(BF16)
    n = w_ref.shape[1]
    for n0 in range(0, n, n_chunk):
        n1 = min(n0 + n_chunk, n)
        o_ref[:, n0:n1] = _dot(h, w_ref[:, n0:n1])
    if maybe_ob_ref:
        maybe_ob_ref[0][...] = o_ref[:, bf16_from:].astype(BF16)


def rms_matmul(x, ln, w, bf16_from=None):
    nt, n = x.shape[0], w.shape[1]
    out_specs = [pl.BlockSpec((TOKEN_TILE, n), lambda i: (i, 0))]
    out_shape = [jax.ShapeDtypeStruct((nt, n), F32)]
    if bf16_from is not None:
        out_specs.append(pl.BlockSpec((TOKEN_TILE, n - bf16_from), lambda i: (i, 0)))
        out_shape.append(jax.ShapeDtypeStruct((nt, n - bf16_from), BF16))
    out = pl.pallas_call(
        functools.partial(_rms_matmul_kernel, n_chunk=2 * MXU_DIM, bf16_from=bf16_from),
        grid=(nt // TOKEN_TILE,),
        in_specs=[pl.BlockSpec((TOKEN_TILE, D_MODEL), lambda i: (i, 0)),
                  _resident((1, D_MODEL)), _resident((D_MODEL, n))],
        out_specs=out_specs,
        out_shape=out_shape,
        compiler_params=_params("parallel"),
        name="rms_matmul",
    )(x, ln, w)
    return out[0] if bf16_from is None else out


def _matmul_residual_kernel(x_ref, a_ref, w_ref, o_ref):
    o_ref[...] = x_ref[...] + _dot(a_ref[...].astype(BF16), w_ref[...])


def matmul_residual(x, a, w):
    nt, k = a.shape
    return pl.pallas_call(
        _matmul_residual_kernel,
        grid=(nt // TOKEN_TILE,),
        in_specs=[pl.BlockSpec((TOKEN_TILE, D_MODEL), lambda i: (i, 0)),
                  pl.BlockSpec((TOKEN_TILE, k), lambda i: (i, 0)),
                  _resident((k, D_MODEL))],
        out_specs=pl.BlockSpec((TOKEN_TILE, D_MODEL), lambda i: (i, 0)),
        out_shape=jax.ShapeDtypeStruct(x.shape, F32),
        compiler_params=_params("parallel"),
        name="matmul_residual",
    )(x, a, w)


def _ple_kernel(x_ref, p_ref, ln_ref, wg_ref, wp_ref, o_ref):
    x = x_ref[...]
    h = _rms(x, ln_ref[...]).astype(BF16)
    gate = jax.nn.sigmoid(_dot(h, wg_ref[...]))
    o_ref[...] = x + gate * _dot(p_ref[...].astype(BF16), wp_ref[...])


def ple(x, p, ln, wg, wp):
    nt = x.shape[0]
    tok = pl.BlockSpec((TOKEN_TILE, D_MODEL), lambda i: (i, 0))
    return pl.pallas_call(
        _ple_kernel,
        grid=(nt // TOKEN_TILE,),
        in_specs=[tok, pl.BlockSpec((TOKEN_TILE, PLE_DIM), lambda i: (i, 0)),
                  _resident((1, D_MODEL)), _resident((D_MODEL, D_MODEL)),
                  _resident((PLE_DIM, D_MODEL))],
        out_specs=tok,
        out_shape=jax.ShapeDtypeStruct(x.shape, F32),
        compiler_params=_params("parallel"),
        name="ple",
    )(x, p, ln, wg, wp)


def _final_norm_kernel(x_ref, ln_ref, o_ref):
    o_ref[...] = _rms(x_ref[...], ln_ref[...])


def final_rmsnorm(x, ln):
    nt = x.shape[0]
    tok = pl.BlockSpec((TOKEN_TILE, D_MODEL), lambda i: (i, 0))
    return pl.pallas_call(
        _final_norm_kernel,
        grid=(nt // TOKEN_TILE,),
        in_specs=[tok, _resident((1, D_MODEL))],
        out_specs=tok,
        out_shape=jax.ShapeDtypeStruct(x.shape, F32),
        compiler_params=_params("parallel"),
        name="final_norm",
    )(x, ln)


def _log_sigmoid(x):
    return jnp.minimum(x, 0.0) - jnp.log1p(jnp.exp(-jnp.abs(x)))


def _split_bf16(x):
    hi = x.astype(BF16)
    lo = (x - hi.astype(F32)).astype(BF16)
    return hi, lo


def _gated_chunk(q, k, v, g, st_ref, head, tri):
    c = q.shape[0]
    g_hi, g_lo = _split_bf16(g)
    b = _dot(tri, g_hi) + _dot(tri, g_lo)
    b_last = b[c - 1:c, :]
    st = st_ref[head]
    qe = (q * jnp.exp(b)).astype(BF16)
    o = lax.dot_general(qe, st.astype(BF16), (((1,), (1,)), ((), ())),
                        preferred_element_type=F32)
    row = lax.broadcasted_iota(jnp.int32, (c, 1), 0)
    for s in range(c):
        d = jnp.where(row >= s, b - b[s:s + 1, :], -1e30)
        a_s = jnp.sum(q * k[s:s + 1, :] * jnp.exp(d), axis=-1, keepdims=True)
        o = o + a_s * v[s:s + 1, :]
    ke = (k * jnp.exp(b_last - b)).astype(BF16)
    st_ref[head] = st * jnp.exp(b_last) + lax.dot_general(
        v.astype(BF16), ke, (((0,), (0,)), ((), ())), preferred_element_type=F32)
    return o


def _even_mixer_kernel(z_ref, s0_ref, wgk_ref, bgk_ref, lb_ref, na_ref, nb_ref,
                       o_ref, sout_ref, st_ref, *, t_valid):
    c = z_ref.shape[0]
    ci = pl.program_id(1)

    @pl.when(ci == 0)
    def _():
        st_ref[...] = s0_ref[0]

    def blk(i):
        return z_ref[:, i * LANES:(i + 1) * LANES]

    row = lax.broadcasted_iota(jnp.int32, (c, 1), 0)
    tri = (lax.broadcasted_iota(jnp.int32, (c, c), 0)
           >= lax.broadcasted_iota(jnp.int32, (c, c), 1)).astype(BF16)
    valid = row < t_valid

    def finish(o, gain, gate, head):
        o = o * lax.rsqrt(jnp.mean(o * o, axis=-1, keepdims=True) + RMS_EPS) * gain
        o_ref[:, head * LANES:(head + 1) * LANES] = o * (gate * jax.nn.sigmoid(gate))

    gk = _log_sigmoid(_dot(blk(4 * N_EVEN_HEADS).astype(BF16), wgk_ref[...]) + bgk_ref[...])
    gk = gk * (1.0 / GLA_GK_NORM)
    for h in range(H_A):
        g = gk[:, h * LANES:(h + 1) * LANES]
        k = blk(H_A + h)
        if t_valid < c:
            g = jnp.where(valid, g, 0.0)
            k = jnp.where(valid, k, 0.0)
        o = _gated_chunk(blk(h) * DK_A ** -0.5, k, blk(2 * H_A + h), g, st_ref, h, tri)
        finish(o, na_ref[...], blk(3 * H_A + h), h)

    base = 4 * H_A
    for h in range(H_B):
        lb = lb_ref[:, h * LANES:(h + 1) * LANES]
        fb = blk(base + H_B + h)
        a = jnp.log1p(-lb) + _log_sigmoid(fb)
        log_lb = jnp.log(lb)
        g = jnp.maximum(a, log_lb) + jnp.log1p(jnp.exp(-jnp.abs(a - log_lb)))
        k = (1.0 - lb) * jax.nn.sigmoid(-fb)
        if t_valid < c:
            g = jnp.where(valid, g, 0.0)
            k = jnp.where(valid, k, 0.0)
        o = _gated_chunk(blk(base + h), k, blk(base + 2 * H_B + h), g, st_ref, H_A + h, tri)
        finish(o, nb_ref[...], blk(base + 3 * H_B + h), H_A + h)

    @pl.when(ci == pl.num_programs(1) - 1)
    def _():
        sout_ref[0] = st_ref[...]


def even_mixer(z, s0, wgk, bgk, lb, norm_a, norm_b, *, n_seq, t, chunk, t_valid=None):
    ncol = z.shape[1]
    t_valid = chunk if t_valid is None else t_valid
    n_chunks = t // chunk
    st_shape = (N_EVEN_HEADS, LANES, LANES)
    return pl.pallas_call(
        functools.partial(_even_mixer_kernel, t_valid=t_valid),
        grid=(n_seq, n_chunks),
        in_specs=[pl.BlockSpec((chunk, ncol), lambda b, c: (b * n_chunks + c, 0)),
                  pl.BlockSpec((1,) + st_shape, lambda b, c: (b, 0, 0, 0)),
                  _resident(wgk.shape), _resident(bgk.shape), _resident(lb.shape),
                  _resident(norm_a.shape), _resident(norm_b.shape)],
        out_specs=[pl.BlockSpec((chunk, D_MODEL), lambda b, c: (b * n_chunks + c, 0)),
                   pl.BlockSpec((1,) + st_shape, lambda b, c: (b, 0, 0, 0))],
        out_shape=[jax.ShapeDtypeStruct((n_seq * t, D_MODEL), F32),
                   jax.ShapeDtypeStruct((n_seq,) + st_shape, F32)],
        scratch_shapes=[pltpu.VMEM(st_shape, F32)],
        compiler_params=_params("parallel", "arbitrary"),
        name="even_mixer",
    )(z, s0, wgk, bgk, lb, norm_a, norm_b)


def _pad_heads(w, n_heads):
    d = w.shape[-1] // n_heads
    w = w.reshape(w.shape[:-1] + (n_heads, d))
    w = jnp.pad(w, [(0, 0)] * (w.ndim - 1) + [(0, LANES - d)])
    return w.reshape(w.shape[:-2] + (n_heads * LANES,))


def _split_cols(w, sizes):
    out, o = [], 0
    for s in sizes:
        out.append(w[..., o:o + s])
        o += s
    return out


def even_weights(w_in, w_gk, b_gk):
    ka = H_A * DK_A
    qa, kk, va, gk_in, ga, qb, fb, ib, gb = _split_cols(
        w_in, (ka, ka, HALF, GLA_GK_RANK, HALF, HALF, HALF, HALF, HALF))
    gk_in = jnp.pad(gk_in, ((0, 0), (0, LANES - GLA_GK_RANK)))
    w = jnp.concatenate([_pad_heads(qa, H_A), _pad_heads(kk, H_A), va, ga, qb, fb, ib, gb, gk_in],
                        axis=-1).astype(BF16)
    wgk = jnp.pad(_pad_heads(w_gk, H_A), ((0, LANES - GLA_GK_RANK), (0, 0))).astype(BF16)
    bgk = _pad_heads(b_gk[None, :], H_A)
    return w, wgk, bgk


def run_even_mixer(z, n_p, t_p, n_s, t_s, state_gla, state_hgrn, wgk, bgk, lb, norm_a, norm_b,
                   *, chunk):
    ncol = z.shape[1]
    args = (wgk, bgk, lb[None, :], norm_a[None, :], norm_b[None, :])
    zero = jnp.zeros((n_p, N_EVEN_HEADS, LANES, LANES), F32)
    o_p, st_p = even_mixer(z, zero, *args, n_seq=n_p, t=t_p, chunk=chunk)
    zs = z[n_p * t_p:].reshape(n_s, t_s, ncol)
    zs = jnp.pad(zs, ((0, 0), (0, SUBLANES - t_s), (0, 0))).reshape(n_s * SUBLANES, ncol)
    s0 = jnp.concatenate([
        jnp.pad(jnp.swapaxes(state_gla, -1, -2), ((0, 0), (0, 0), (0, 0), (0, LANES - DK_A))),
        jnp.swapaxes(state_hgrn, -1, -2)], axis=1)
    o_s, st_s = even_mixer(zs, s0, *args, n_seq=n_s, t=SUBLANES, chunk=SUBLANES, t_valid=t_s)
    o_s = o_s.reshape(n_s, SUBLANES, D_MODEL)[:, :t_s].reshape(n_s * t_s, D_MODEL)

    def unpack(st):
        gla = jnp.swapaxes(st[:, :H_A], -1, -2)[:, :, :DK_A, :]
        return gla, jnp.swapaxes(st[:, H_A:], -1, -2)

    return jnp.concatenate([o_p, o_s], axis=0), unpack(st_p), unpack(st_s)


_NT_DIMS = (((1,), (1,)), ((), ()))
LOG2E = 1.4426950408889634


def _neg_softplus(z):
    return -(jnp.maximum(z, 0.0) + jnp.log1p(jnp.exp(-jnp.abs(z))))


def _suffix_tri(n):
    return (lax.broadcasted_iota(jnp.int32, (n, n), 0)
            >= lax.broadcasted_iota(jnp.int32, (n, n), 1)).astype(BF16)


def _sb_prompt_kernel(bias_ref, q_ref, k_ref, v_ref, o_ref, u_ref, tot_ref, carry_ref, acc_ref,
                      *, blk, group):
    pair = pl.program_id(1)
    qi = pl.program_id(2)
    lane = lax.broadcasted_iota(jnp.int32, (1, LANES), 1)
    head_a = lane < DH_C
    q2 = q_ref[...].astype(F32) * (DH_C ** -0.5 * LOG2E)
    qs = jnp.concatenate([jnp.where(head_a, q2, 0.0), jnp.where(head_a, 0.0, q2)], axis=0).astype(BF16)
    row = lax.broadcasted_iota(jnp.int32, (2 * blk, 1), 0)
    bias2 = jnp.where(row < blk, bias_ref[2 * pair], bias_ref[2 * pair + 1]) * LOG2E
    neg_tri = -_suffix_tri(blk)
    strict = lax.broadcasted_iota(jnp.int32, (2 * blk, blk), 1) < (row % blk)

    def rows_of(ref, j):
        return ref[pl.ds(pl.multiple_of(j * blk, blk), blk), :]

    def scores(slot, j, diagonal=False):
        b2 = jnp.where(j >= 0, bias2, -1e30)
        z2 = lax.dot_general(qs, rows_of(k_ref, jnp.maximum(j, 0)), _NT_DIMS,
                             preferred_element_type=F32) + b2
        if diagonal:
            z2 = jnp.where(strict, z2, -1e30)
        sp2 = jnp.maximum(z2, 0.0) + jnp.log2(1.0 + jnp.exp2(-jnp.abs(z2)))
        incl = _dot(sp2.astype(BF16), neg_tri)
        u_ref[slot] = z2 + incl
        tot_ref[slot] = incl[:, 0:1]

    def accumulate(slot, j):
        w = jnp.exp2(u_ref[slot] + carry_ref[...])
        acc_ref[...] += _dot(w.astype(BF16), rows_of(v_ref, jnp.maximum(j, 0)))
        carry_ref[...] += tot_ref[slot]

    def body(k, _):
        j = qi - group * k
        for s in range(group):
            accumulate(s, j + group - s)
        for s in range(group):
            scores(s, j - s)
        return 0

    carry_ref[...] = jnp.zeros(carry_ref.shape, F32)
    acc_ref[...] = jnp.zeros(acc_ref.shape, F32)
    for s in range(group):
        scores(s, qi - s, diagonal=(s == 0))
    lax.fori_loop(1, qi // group + 1, body, 0)
    for s in range(group):
        accumulate(s, qi % group - s)
    o_ref[...] = jnp.where(head_a, acc_ref[:blk, :], acc_ref[blk:, :])


def sb_attention_prompt(z, bias, *, n_seq, t, q_col, blk, group):
    nq = t // blk
    n_pairs = H_C // 2
    return pl.pallas_call(
        functools.partial(_sb_prompt_kernel, blk=blk, group=group),
        grid=(n_seq, n_pairs, nq),
        in_specs=[pl.BlockSpec(memory_space=pltpu.SMEM),
                  pl.BlockSpec((blk, LANES), lambda b, p, i: (b * nq + i, q_col + p)),
                  pl.BlockSpec((t, LANES), lambda b, p, i: (b, q_col + n_pairs + p)),
                  pl.BlockSpec((t, LANES), lambda b, p, i: (b, q_col + 2 * n_pairs + p))],
        out_specs=pl.BlockSpec((blk, LANES), lambda b, p, i: (b * nq + i, p)),
        out_shape=jax.ShapeDtypeStruct((n_seq * t, HALF), F32),
        scratch_shapes=[pltpu.VMEM((group, 2 * blk, blk), F32), pltpu.VMEM((group, 2 * blk, 1), F32),
                        pltpu.VMEM((2 * blk, 1), F32), pltpu.VMEM((2 * blk, LANES), F32)],
        compiler_params=_params("parallel", "parallel", "arbitrary"),
        name="sb_attention_prompt",
    )(bias, z, z, z)


def _sb_sample_kernel(pt_ref, q_ref, kn_ref, vn_ref, bias_ref, *refs, n_pages, t_s):
    del pt_ref
    k_pages, v_pages, o_ref = refs[:n_pages], refs[n_pages:2 * n_pages], refs[2 * n_pages]
    rows = t_s * H_C
    q = q_ref[0] * DH_C ** -0.5
    qx = jnp.concatenate([jnp.broadcast_to(q[i:i + 1, :], (H_C, HALF)) for i in range(t_s)], axis=0)
    r_id = lax.broadcasted_iota(jnp.int32, (rows, HALF), 0)
    l_id = lax.broadcasted_iota(jnp.int32, (rows, HALF), 1)
    head_lanes = (l_id // DH_C) == (r_id % H_C)
    qx = jnp.where(head_lanes, qx, 0.0)
    bias = bias_ref[...]
    q_idx = lax.broadcasted_iota(jnp.int32, (rows, 1), 0) // H_C

    carry = jnp.zeros((rows, 1), F32)
    acc = jnp.zeros((rows, HALF), F32)
    kn, vn = kn_ref[0], vn_ref[0]
    for j in reversed(range(t_s)):
        visible = q_idx > j
        z = jnp.sum(qx * kn[j:j + 1, :], axis=-1, keepdims=True) + bias[:, 0:1]
        lk = jnp.where(visible, _neg_softplus(z), 0.0)
        w = jnp.where(visible, jnp.exp(z + lk + carry), 0.0)
        acc = acc + w * vn[j:j + 1, :]
        carry = carry + lk

    qx = qx.astype(BF16)
    tri = _suffix_tri(PAGE_SIZE)
    for p in reversed(range(n_pages)):
        kb = k_pages[p][...].astype(BF16)
        vb = v_pages[p][...].astype(BF16)
        z = lax.dot_general(qx, kb, _NT_DIMS, preferred_element_type=F32) + bias
        incl = _dot(_neg_softplus(z).astype(BF16), tri)
        w = jnp.exp(z + incl + carry)
        acc = acc + _dot(w.astype(BF16), vb)
        carry = carry + incl[:, 0:1]

    acc = jnp.where(head_lanes, acc, 0.0)
    for i in range(t_s):
        o_ref[0, i:i + 1, :] = jnp.sum(acc[i * H_C:(i + 1) * H_C, :], axis=0, keepdims=True)


def sb_attention_sample(q, k_new, v_new, cache_k, cache_v, layer, page_table, bias_rows):
    n_seq, t_s, _ = q.shape
    n_pages = page_table.shape[0] // n_seq
    tok = pl.BlockSpec((1, t_s, HALF), lambda b, pt: (b, 0, 0))

    def page_spec(p):
        return pl.BlockSpec((None, None, PAGE_SIZE, HALF),
                            lambda b, pt: (layer, pt[b * n_pages + p], 0, 0))

    pages = [page_spec(p) for p in range(n_pages)]
    return pl.pallas_call(
        functools.partial(_sb_sample_kernel, n_pages=n_pages, t_s=t_s),
        grid_spec=pltpu.PrefetchScalarGridSpec(
            num_scalar_prefetch=1,
            grid=(n_seq,),
            in_specs=[tok, tok, tok,
                      pl.BlockSpec(bias_rows.shape, lambda b, pt: (0, 0))] + pages + pages,
            out_specs=tok),
        out_shape=jax.ShapeDtypeStruct((n_seq, t_s, HALF), F32),
        compiler_params=_params("parallel"),
        name="sb_attention_sample",
    )(page_table, q, k_new, v_new, bias_rows, *([cache_k] * n_pages), *([cache_v] * n_pages))


N_SCAN_ROWS = 7
N_PAIRS_D = H_D // 2


def _head_sum(x, seg):
    hi, lo = _split_bf16(x)
    return _dot(hi, seg) + _dot(lo, seg)


def _rwkv_prep_kernel(zd_ref, zb_ref, prev_s_ref, mu_ref, w0_ref, w2_ref, a0_ref, a2_ref, g2_ref,
                      kk_ref, ka_ref, rk_ref, seg_ref, rows_ref, v_ref, gb_ref,
                      *, tiles_per_seq, n_prompt_tiles):
    i = pl.program_id(0)
    z = zd_ref[...]
    tm = z.shape[0]
    boundary = jnp.where(i % tiles_per_seq == 0, 0.0, 1.0) * zb_ref[SUBLANES - 1:SUBLANES, :]
    row = lax.broadcasted_iota(jnp.int32, (tm, 1), 0)
    prev = jnp.where(row == 0, boundary, pltpu.roll(z, 1, axis=0))
    prev = jnp.where(i >= n_prompt_tiles, prev_s_ref[...], prev)
    zs = z + (prev - z) * mu_ref[...]
    r, k, v = zs[:, 0:DM_D], zs[:, DM_D:2 * DM_D], zs[:, 2 * DM_D:3 * DM_D]
    u = zs[:, 3 * DM_D:3 * DM_D + R_W + R_A]
    g_in = zs[:, 3 * DM_D + R_W + R_A:]
    seg = seg_ref[...]
    w = w0_ref[...] + _dot(jnp.tanh(u).astype(BF16), w2_ref[...])
    decay = jnp.exp(-jnp.exp(_log_sigmoid(w) - 0.5))
    a = jax.nn.sigmoid(a0_ref[...] + _dot(u.astype(BF16), a2_ref[...]))
    g = _dot(jax.nn.sigmoid(g_in).astype(BF16), g2_ref[...])
    kk = k * kk_ref[...]
    kk = kk * lax.rsqrt(jnp.maximum(_head_sum(kk * kk, seg), 1e-24))
    kp = k * (1.0 + (a - 1.0) * ka_ref[...])
    beta = kk * a
    outs = (decay, kk, beta, kp, decay * r, _head_sum(beta * r, seg), _head_sum(kp * r, seg))
    for n, val in enumerate(outs):
        rows_ref[:, n * DM_D:(n + 1) * DM_D] = val
    v_ref[...] = v
    gb_ref[:, 0:DM_D] = g
    gb_ref[:, DM_D:] = _head_sum(r * kp * rk_ref[...], seg) * v


def rwkv_prep(z, prev_s, mu, w0, w2, a0, a2, g2, k_k, k_a, r_k, seg, *, n_prompt_tok, t_prompt):
    nt = z.shape[0]
    tm = TOKEN_TILE
    vec = _resident((1, DM_D))
    tok = lambda n: pl.BlockSpec((tm, n), lambda i: (i, 0))
    return pl.pallas_call(
        functools.partial(_rwkv_prep_kernel, tiles_per_seq=t_prompt // tm,
                          n_prompt_tiles=n_prompt_tok // tm),
        grid=(nt // tm,),
        in_specs=[tok(N_COLS_RWKV),
                  pl.BlockSpec((SUBLANES, N_COLS_RWKV),
                               lambda i: (jnp.maximum(i * (tm // SUBLANES) - 1, 0), 0)),
                  _resident(prev_s.shape), _resident((1, N_COLS_RWKV)),
                  vec, _resident(w2.shape), vec, _resident(a2.shape), _resident(g2.shape),
                  vec, vec, vec, _resident(seg.shape)],
        out_specs=[tok(N_SCAN_ROWS * DM_D), tok(DM_D), tok(2 * DM_D)],
        out_shape=[jax.ShapeDtypeStruct((nt, N_SCAN_ROWS * DM_D), F32),
                   jax.ShapeDtypeStruct((nt, DM_D), F32),
                   jax.ShapeDtypeStruct((nt, 2 * DM_D), F32)],
        compiler_params=_params("parallel"),
        name="rwkv_prep",
    )(z, z, prev_s, mu, w0, w2, a0, a2, g2, k_k, k_a, r_k, seg)


def _rwkv_scan_kernel(*refs, nb, t_c, rows_per_seq):
    n_in = nb if rows_per_seq else 1
    rows_refs, v_refs = refs[:n_in], refs[n_in:2 * n_in]
    s0_ref, o_ref, sout_ref, s_scr, vt_scr, sk_scr, pr_scr = refs[2 * n_in:]
    ci = pl.program_id(1)

    @pl.when(ci == 0)
    def _():
        s_scr[...] = s0_ref[...]

    def seq_rows(ref_list, kq, start, size):
        if rows_per_seq:
            return ref_list[kq][pl.ds(start, size), :]
        return ref_list[0][pl.ds(kq * t_c + start, size), :]

    lane = lax.broadcasted_iota(jnp.int32, (1, LANES), 1)
    head_a = lane < DH_D
    ones_bd = ((lax.broadcasted_iota(jnp.int32, (LANES, LANES), 0) // DH_D)
               == (lax.broadcasted_iota(jnp.int32, (LANES, LANES), 1) // DH_D)).astype(BF16)

    spb = 1 if rows_per_seq else nb
    n_sel = spb * t_c
    sel = ((lax.broadcasted_iota(jnp.int32, (n_sel, spb * LANES), 0) // t_c
            == lax.broadcasted_iota(jnp.int32, (n_sel, spb * LANES), 1) // LANES)
           & (lax.broadcasted_iota(jnp.int32, (n_sel, spb * LANES), 0) % t_c
              == lax.broadcasted_iota(jnp.int32, (n_sel, spb * LANES), 1) % DH_D)).astype(BF16)
    for blk_i in range(n_in):
        for p in range(N_PAIRS_D):
            vp = v_refs[blk_i][:, p * LANES:(p + 1) * LANES].astype(BF16)
            vt = lax.dot_general(vp, sel, (((0,), (0,)), ((), ())), preferred_element_type=F32)
            for q in range(spb):
                blk = vt[:, q * LANES:(q + 1) * LANES]
                vt_scr[blk_i * spb + q, p] = jnp.where(head_a, blk[:DH_D], blk[DH_D:]).astype(BF16)
    sk_scr[...] = jnp.zeros(sk_scr.shape, F32)
    pr_scr[...] = jnp.zeros(pr_scr.shape, F32)

    def body(t, carry):
        here = (lane % DH_D) == t
        here_b = here.astype(BF16)
        for kq in range(nb):
            row = seq_rows(rows_refs, kq, t, 1)
            for p in range(N_PAIRS_D):
                w, kap, beta, kp, wr = (
                    row[:, n * DM_D + p * LANES:n * DM_D + (p + 1) * LANES] for n in range(5))
                sp = s_scr[kq, p]
                spb16 = sp.astype(BF16)
                lhs = jnp.concatenate([spb16 * kap.astype(BF16), spb16 * wr.astype(BF16),
                                       vt_scr[kq, p] * here_b], axis=0)
                red = _dot(lhs, ones_bd)
                skk, pr, vcol = red[0:DH_D], red[DH_D:2 * DH_D], red[2 * DH_D:3 * DH_D]
                s_scr[kq, p] = sp * w - skk * beta + vcol * kp
                sk_scr[kq, p] = jnp.where(here, skk, sk_scr[kq, p])
                pr_scr[kq, p] = jnp.where(here, pr, pr_scr[kq, p])
        return carry

    lax.fori_loop(0, t_c, body, 0)

    def token_major(tile):
        x = tile.T
        return jnp.concatenate([x[0:t_c], x[DH_D:DH_D + t_c]], axis=1)

    for kq in range(nb):
        rows = seq_rows(rows_refs, kq, 0, t_c)
        v = seq_rows(v_refs, kq, 0, t_c)
        for p in range(N_PAIRS_D):
            c1 = rows[:, 5 * DM_D + p * LANES:5 * DM_D + (p + 1) * LANES]
            c2 = rows[:, 6 * DM_D + p * LANES:6 * DM_D + (p + 1) * LANES]
            o = (token_major(pr_scr[kq, p]) - token_major(sk_scr[kq, p]) * c1
                 + v[:, p * LANES:(p + 1) * LANES] * c2)
            o_ref[kq, :, p * LANES:(p + 1) * LANES] = o

    @pl.when(ci == pl.num_programs(1) - 1)
    def _():
        sout_ref[...] = s_scr[...]


def rwkv_scan(rows, v, s0, *, tok0, t, nb):
    n_seq = s0.shape[0]
    rows_per_seq = t >= DH_D
    t_c = DH_D if rows_per_seq else t
    n_chunks = t // t_c
    if rows_per_seq:
        def tok_spec(ncol):
            return [pl.BlockSpec((t_c, ncol),
                                 lambda g, c, kq=kq: (tok0 // t_c + (g * nb + kq) * n_chunks + c, 0))
                    for kq in range(nb)]
        n_in = nb
    else:
        def tok_spec(ncol):
            return [pl.BlockSpec((nb * t_c, ncol), lambda g, c: (tok0 // (nb * t_c) + g, 0))]
        n_in = 1
    st_shape = (nb,) + s0.shape[1:]
    st_spec = pl.BlockSpec(st_shape, lambda g, c: (g, 0, 0, 0))
    return pl.pallas_call(
        functools.partial(_rwkv_scan_kernel, nb=nb, t_c=t_c, rows_per_seq=rows_per_seq),
        grid=(n_seq // nb, n_chunks),
        in_specs=tok_spec(rows.shape[1]) + tok_spec(DM_D) + [st_spec],
        out_specs=[pl.BlockSpec((nb, t_c, DM_D), lambda g, c: (g, c, 0)), st_spec],
        out_shape=[jax.ShapeDtypeStruct((n_seq, t, DM_D), F32), jax.ShapeDtypeStruct(s0.shape, F32)],
        scratch_shapes=[pltpu.VMEM(st_shape, F32), pltpu.VMEM(st_shape, BF16),
                        pltpu.VMEM(st_shape, F32), pltpu.VMEM(st_shape, F32)],
        compiler_params=_params("parallel", "arbitrary"),
        name="rwkv_scan",
    )(*([rows] * n_in), *([v] * n_in), s0)


def _odd_out_kernel(x_ref, oc_ref, od_ref, gb_ref, lnw_ref, lnb_ref, seg_ref, w_ref, o_ref):
    seg = seg_ref[...]
    o = od_ref[...]
    d = o - _head_sum(o, seg) * (1.0 / DH_D)
    var = _head_sum(d * d, seg) * (1.0 / DH_D)
    od = d * lax.rsqrt(var + RWKV_LN_EPS) * lnw_ref[...] + lnb_ref[...]
    od = (od + gb_ref[:, DM_D:]) * gb_ref[:, 0:DM_D]
    a = jnp.concatenate([oc_ref[...], od], axis=-1).astype(BF16)
    o_ref[...] = x_ref[...] + _dot(a, w_ref[...])


def odd_out(x, o_c, o_d, gb, ln_w, ln_b, seg, w_out):
    nt = x.shape[0]
    tok = lambda n: pl.BlockSpec((TOKEN_TILE, n), lambda i: (i, 0))
    vec = _resident((1, DM_D))
    return pl.pallas_call(
        _odd_out_kernel,
        grid=(nt // TOKEN_TILE,),
        in_specs=[tok(D_MODEL), tok(HALF), tok(DM_D), tok(2 * DM_D), vec, vec,
                  _resident(seg.shape), _resident(w_out.shape)],
        out_specs=tok(D_MODEL),
        out_shape=jax.ShapeDtypeStruct(x.shape, F32),
        compiler_params=_params("parallel"),
        name="odd_out",
    )(x, o_c, o_d, gb, ln_w, ln_b, seg, w_out)


def _state_to_pairs(s):
    n_seq = s.shape[0]
    s = s.reshape(n_seq, N_PAIRS_D, 2, DH_D, DH_D)
    return jnp.transpose(s, (0, 1, 3, 2, 4)).reshape(n_seq, N_PAIRS_D, DH_D, LANES)


def _state_from_pairs(s):
    n_seq = s.shape[0]
    s = s.reshape(n_seq, N_PAIRS_D, DH_D, 2, DH_D)
    return jnp.transpose(s, (0, 1, 3, 2, 4)).reshape(n_seq, H_D, DH_D, DH_D)


def run_rwkv(z, n_p, t_p, n_s, t_s, state, shift, mu, w0, w2, a0, a2, g2, k_k, k_a, r_k, seg,
             *, nb_sample):
    np_tok = n_p * t_p
    zd_s = z[np_tok:, :N_COLS_RWKV].reshape(n_s, t_s, N_COLS_RWKV)
    prev_s = jnp.concatenate([shift[:, None, :], zd_s[:, :-1]], axis=1).reshape(n_s * t_s, N_COLS_RWKV)
    w2p = jnp.pad(w2, ((0, R_A), (0, 0))).astype(BF16)
    a2p = jnp.pad(a2, ((R_W, 0), (0, 0))).astype(BF16)
    row = lambda a: a.reshape(1, -1)
    rows, v, gb = rwkv_prep(z, prev_s, row(mu), row(w0), w2p, row(a0), a2p, g2.astype(BF16),
                            row(k_k), row(k_a), row(r_k), seg, n_prompt_tok=np_tok, t_prompt=t_p)
    zero = jnp.zeros((n_p, N_PAIRS_D, DH_D, LANES), F32)
    o_p, st_p = rwkv_scan(rows, v, zero, tok0=0, t=t_p, nb=n_p)
    o_s, st_s = rwkv_scan(rows, v, _state_to_pairs(state), tok0=np_tok, t=t_s, nb=nb_sample)
    o = jnp.concatenate([o_p.reshape(np_tok, DM_D), o_s.reshape(n_s * t_s, DM_D)], axis=0)
    return o, gb, _state_from_pairs(st_p), _state_from_pairs(st_s)


EVEN_CHUNK = 16
SB_BLOCK = 256
SB_GROUP = 4
RWKV_SAMPLE_GROUP = 8


def kernel(x_prompt, x_sample, p_prompt, p_sample, cache_k, cache_v, page_table, state_gla, state_hgrn, state_rwkv, state_rwkv_shift, ln_ffn1, ffn1_w_gate, ffn1_w_up, ffn1_w_down, ln_mix, ln_ffn2, ffn2_w_gate, ffn2_w_up, ffn2_w_down, ln_ple, ple_w_gate, ple_w_proj, w_in_even, w_out_even, gla_w_gk, gla_b_gk, gla_norm, hgrn_lb_logits, hgrn_norm, w_in_odd, w_out_odd, sb_bias, rwkv_mu, rwkv_w0, rwkv_w2, rwkv_a0, rwkv_a2, rwkv_g2, rwkv_k_k, rwkv_k_a, rwkv_r_k, rwkv_ln_w, rwkv_ln_b, final_norm):
    n_p, t_p, _ = x_prompt.shape
    n_s, t_s, _ = x_sample.shape
    np_tok, ns_tok = n_p * t_p, n_s * t_s
    assert ns_tok == TOKEN_TILE and t_p % TOKEN_TILE == 0 and t_s < SUBLANES
    bf = lambda w: w.astype(BF16)
    row = lambda a: a.reshape(1, -1)

    x = jnp.concatenate([x_prompt.reshape(np_tok, D_MODEL), x_sample.reshape(ns_tok, D_MODEL)], axis=0)
    p_all = jnp.concatenate([p_prompt.reshape(DEPTH, np_tok, PLE_DIM),
                             p_sample.reshape(DEPTH, ns_tok, PLE_DIM)], axis=1)
    sm = jax.nn.softmax(hgrn_lb_logits.astype(F32), axis=0)
    lower_bounds = jnp.concatenate([jnp.zeros_like(sm[:1]), jnp.cumsum(sm[1:], axis=0)], axis=0)
    head_id = jnp.arange(DM_D) // DH_D
    seg = (head_id[:, None] == head_id[None, :]).astype(BF16)
    n_phys = cache_k.shape[1]
    ck = cache_k.reshape(cache_k.shape[0], n_phys, PAGE_SIZE, HALF)
    cv = cache_v.reshape(cache_v.shape[0], n_phys, PAGE_SIZE, HALF)
    pt_flat = page_table.reshape(-1)
    q0 = N_COLS_RWKV
    k0, v0 = q0 + HALF, q0 + 2 * HALF

    k_rows_p, v_rows_p, k_rows_s, v_rows_s = [], [], [], []
    gla_p, gla_s, hgrn_p, hgrn_s = [], [], [], []
    rwkv_p, rwkv_s, shift_p, shift_s = [], [], [], []
    for i in range(DEPTH):
        j = i // 2
        x = ffn(x, row(ln_ffn1[i]), bf(ffn1_w_gate[i]), bf(ffn1_w_up[i]), bf(ffn1_w_down[i]))
        if i % 2 == 0:
            w_in, wgk, bgk = even_weights(w_in_even[j], gla_w_gk[j], gla_b_gk[j])
            z = rms_matmul(x, row(ln_mix[i]), w_in)
            o, (ga_p, hb_p), (ga_s, hb_s) = run_even_mixer(
                z, n_p, t_p, n_s, t_s, state_gla[j], state_hgrn[j], wgk, bgk,
                lower_bounds[j], gla_norm[j], hgrn_norm[j], chunk=EVEN_CHUNK)
            x = matmul_residual(x, o, bf(w_out_even[j]))
            gla_p.append(ga_p)
            gla_s.append(ga_s)
            hgrn_p.append(hb_p)
            hgrn_s.append(hb_s)
        else:
            w = w_in_odd[j]
            z, qkv = rms_matmul(x, row(ln_mix[i]),
                                bf(jnp.concatenate([w[:, 3 * HALF:], w[:, :3 * HALF]], axis=1)), bf16_from=q0)
            oc_p = sb_attention_prompt(qkv, sb_bias[j], n_seq=n_p, t=t_p, q_col=0, blk=SB_BLOCK,
                                       group=SB_GROUP)
            z_s = z[np_tok:]
            q_s, kc_s, vc_s = (z_s[:, c:c + HALF].reshape(n_s, t_s, HALF) for c in (q0, k0, v0))
            bias_rows = jnp.broadcast_to(jnp.tile(sb_bias[j], t_s)[:, None], (t_s * H_C, PAGE_SIZE))
            oc_s = sb_attention_sample(q_s, kc_s, vc_s, ck, cv, j, pt_flat, bias_rows)
            o_c = jnp.concatenate([oc_p, oc_s.reshape(ns_tok, HALF)], axis=0)
            o_d, gb, sd_p, sd_s = run_rwkv(
                z, n_p, t_p, n_s, t_s, state_rwkv[j], state_rwkv_shift[j], rwkv_mu[j], rwkv_w0[j],
                rwkv_w2[j], rwkv_a0[j], rwkv_a2[j], rwkv_g2[j], rwkv_k_k[j], rwkv_k_a[j],
                rwkv_r_k[j], seg, nb_sample=RWKV_SAMPLE_GROUP)
            x = odd_out(x, o_c, o_d, gb, row(rwkv_ln_w[j]), row(rwkv_ln_b[j]), seg, bf(w_out_odd[j]))
            z_p = z[:np_tok].reshape(n_p, t_p, -1)
            k_rows_p.append(z_p[:, :, k0:k0 + HALF].reshape(n_p, t_p, H_C, DH_C))
            v_rows_p.append(z_p[:, :, v0:v0 + HALF].reshape(n_p, t_p, H_C, DH_C))
            k_rows_s.append(kc_s.reshape(n_s, t_s, H_C, DH_C))
            v_rows_s.append(vc_s.reshape(n_s, t_s, H_C, DH_C))
            rwkv_p.append(sd_p)
            rwkv_s.append(sd_s)
            shift_p.append(z_p[:, -1, :N_COLS_RWKV])
            shift_s.append(z_s.reshape(n_s, t_s, -1)[:, -1, :N_COLS_RWKV])
        x = ffn(x, row(ln_ffn2[i]), bf(ffn2_w_gate[i]), bf(ffn2_w_up[i]), bf(ffn2_w_down[i]))
        x = ple(x, p_all[i], row(ln_ple[i]), bf(ple_w_gate[i]), bf(ple_w_proj[i]))

    y = final_rmsnorm(x, row(final_norm))
    return (y[:np_tok].reshape(n_p, t_p, D_MODEL), y[np_tok:].reshape(n_s, t_s, D_MODEL),
            jnp.stack(k_rows_p), jnp.stack(v_rows_p), jnp.stack(k_rows_s), jnp.stack(v_rows_s),
            jnp.stack(gla_p), jnp.stack(gla_s), jnp.stack(hgrn_p), jnp.stack(hgrn_s),
            jnp.stack(rwkv_p), jnp.stack(rwkv_s), jnp.stack(shift_p), jnp.stack(shift_s))
```

```python
import functools

import jax
import jax.numpy as jnp
from jax import lax
from jax.experimental import pallas as pl
from jax.experimental.pallas import tpu as pltpu

F32 = jnp.float32
BF16 = jnp.bfloat16

D_MODEL = 1024
DEPTH = 4
PAGE_SIZE = 128
HALF = D_MODEL // 2
D_FF = 2816
PLE_DIM = 256
RMS_EPS = 1e-6
H_A = 4
DK_A = HALF // H_A // 2
DV_A = HALF // H_A
GLA_GK_RANK = 16
GLA_GK_NORM = 16.0
H_B = 4
EXP_B = 128
DV_B = HALF // H_B
H_C = 8
DH_C = HALF // H_C
H_D = 8
DH_D = HALF // H_D
DM_D = H_D * DH_D
R_W = 64
R_A = 64
R_G = 128
RWKV_LN_EPS = 64e-5
N_COLS_RWKV = 3 * DM_D + R_W + R_A + R_G
N_COLS_ODD = 3 * HALF + N_COLS_RWKV

LANES = 128
SUBLANES = 8
MXU_DIM = 256
VMEM_LIMIT_BYTES = 56 * 1024 * 1024

TOKEN_TILE = 512
N_EVEN_HEADS = H_A + H_B
N_COLS_EVEN_PAD = (4 * N_EVEN_HEADS + 1) * LANES
GATED_SUB_BLOCK = 16


def _params(*semantics):
    return pltpu.CompilerParams(dimension_semantics=semantics,
                                vmem_limit_bytes=VMEM_LIMIT_BYTES)


def _resident(shape):
    return pl.BlockSpec(shape, lambda *_: (0,) * len(shape), pipeline_mode=pl.Buffered(1))


def _rms(x, g):
    return x * lax.rsqrt(jnp.mean(x * x, axis=-1, keepdims=True) + RMS_EPS) * g


def _dot(a, b):
    return jnp.dot(a, b, preferred_element_type=F32)


def _ffn_kernel(x_ref, ln_ref, wg_ref, wu_ref, wd_ref, o_ref):
    x = x_ref[...]
    h = _rms(x, ln_ref[...]).astype(BF16)
    acc = jnp.zeros(x.shape, F32)
    for f0 in range(0, D_FF, MXU_DIM):
        g = _dot(h, wg_ref[:, f0:f0 + MXU_DIM])
        u = _dot(h, wu_ref[:, f0:f0 + MXU_DIM])
        a = (g * jax.nn.sigmoid(g) * u).astype(BF16)
        acc = acc + _dot(a, wd_ref[f0:f0 + MXU_DIM, :])
    o_ref[...] = x + 0.5 * acc


def ffn(x, ln, wg, wu, wd):
    nt = x.shape[0]
    tok = pl.BlockSpec((TOKEN_TILE, D_MODEL), lambda i: (i, 0))
    return pl.pallas_call(
        _ffn_kernel,
        grid=(nt // TOKEN_TILE,),
        in_specs=[tok, _resident((1, D_MODEL)), _resident((D_MODEL, D_FF)),
                  _resident((D_MODEL, D_FF)), _resident((D_FF, D_MODEL))],
        out_specs=tok,
        out_shape=jax.ShapeDtypeStruct(x.shape, F32),
        compiler_params=_params("parallel"),
        name="ffn",
    )(x, ln, wg, wu, wd)


def _rms_matmul_kernel(x_ref, ln_ref, w_ref, o_ref, *maybe_ob_ref, n_chunk, bf16_from):
    h = _rms(x_ref[...], ln_ref[...]).astype(BF16)
    n = w_ref.shape[1]
    for n0 in range(0, n, n_chunk):
        n1 = min(n0 + n_chunk, n)
        o_ref[:, n0:n1] = _dot(h, w_ref[:, n0:n1])
    if maybe_ob_ref:
        maybe_ob_ref[0][...] = o_ref[:, bf16_from:].astype(BF16)


def rms_matmul(x, ln, w, bf16_from=None):
    nt, n = x.shape[0], w.shape[1]
    out_specs = [pl.BlockSpec((TOKEN_TILE, n), lambda i: (i, 0))]
    out_shape = [jax.ShapeDtypeStruct((nt, n), F32)]
    if bf16_from is not None:
        out_specs.append(pl.BlockSpec((TOKEN_TILE, n - bf16_from), lambda i: (i, 0)))
        out_shape.append(jax.ShapeDtypeStruct((nt, n - bf16_from), BF16))
    out = pl.pallas_call(
        functools.partial(_rms_matmul_kernel, n_chunk=2 * MXU_DIM, bf16_from=bf16_from),
        grid=(nt // TOKEN_TILE,),
        in_specs=[pl.BlockSpec((TOKEN_TILE, D_MODEL), lambda i: (i, 0)),
                  _resident((1, D_MODEL)), _resident((D_MODEL, n))],
        out_specs=out_specs,
        out_shape=out_shape,
        compiler_params=_params("parallel"),
        name="rms_matmul",
    )(x, ln, w)
    return out[0] if bf16_from is None else out


def _matmul_residual_kernel(x_ref, a_ref, w_ref, o_ref):
    o_ref[...] = x_ref[...] + _dot(a_ref[...].astype(BF16), w_ref[...])


def matmul_residual(x, a, w):
    nt, k = a.shape
    return pl.pallas_call(
        _matmul_residual_kernel,
        grid=(nt // TOKEN_TILE,),
        in_specs=[pl.BlockSpec((TOKEN_TILE, D_MODEL), lambda i: (i, 0)),
                  pl.BlockSpec((TOKEN_TILE, k), lambda i: (i, 0)),
                  _resident((k, D_MODEL))],
        out_specs=pl.BlockSpec((TOKEN_TILE, D_MODEL), lambda i: (i, 0)),
        out_shape=jax.ShapeDtypeStruct(x.shape, F32),
        compiler_params=_params("parallel"),
        name="matmul_residual",
    )(x, a, w)


def _ple_kernel(x_ref, p_ref, ln_ref, wg_ref, wp_ref, o_ref):
    x = x_ref[...]
    h = _rms(x, ln_ref[...]).astype(BF16)
    gate = jax.nn.sigmoid(_dot(h, wg_ref[...]))
    o_ref[...] = x + gate * _dot(p_ref[...].astype(BF16), wp_ref[...])


def ple(x, p, ln, wg, wp):
    nt = x.shape[0]
    tok = pl.BlockSpec((TOKEN_TILE, D_MODEL), lambda i: (i, 0))
    return pl.pallas_call(
        _ple_kernel,
        grid=(nt // TOKEN_TILE,),
        in_specs=[tok, pl.BlockSpec((TOKEN_TILE, PLE_DIM), lambda i: (i, 0)),
                  _resident((1, D_MODEL)), _resident((D_MODEL, D_MODEL)),
                  _resident((PLE_DIM, D_MODEL))],
        out_specs=tok,
        out_shape=jax.ShapeDtypeStruct(x.shape, F32),
        compiler_params=_params("parallel"),
        name="ple",
    )(x, p, ln, wg, wp)


def _final_norm_kernel(x_ref, ln_ref, o_ref):
    o_ref[...] = _rms(x_ref[...], ln_ref[...])


def final_rmsnorm(x, ln):
    nt = x.shape[0]
    tok = pl.BlockSpec((TOKEN_TILE, D_MODEL), lambda i: (i, 0))
    return pl.pallas_call(
        _final_norm_kernel,
        grid=(nt // TOKEN_TILE,),
        in_specs=[tok, _resident((1, D_MODEL))],
        out_specs=tok,
        out_shape=jax.ShapeDtypeStruct(x.shape, F32),
        compiler_params=_params("parallel"),
        name="final_norm",
    )(x, ln)


def _log_sigmoid(x):
    return jnp.minimum(x, 0.0) - jnp.log1p(jnp.exp(-jnp.abs(x)))


def _split_bf16(x):
    hi = x.astype(BF16)
    lo = (x - hi.astype(F32)).astype(BF16)
    return hi, lo


def _gated_chunk(q, k, v, g, st_ref, head, tri):
    c = q.shape[0]
    g_hi, g_lo = _split_bf16(g)
    b = _dot(tri, g_hi) + _dot(tri, g_lo)
    b_last = b[c - 1:c, :]
    st = st_ref[head]
    qe = (q * jnp.exp(b)).astype(BF16)
    o = lax.dot_general(qe, st.astype(BF16), (((1,), (1,)), ((), ())),
                        preferred_element_type=F32)
    sub = min(c, GATED_SUB_BLOCK)
    row = lax.broadcasted_iota(jnp.int32, (sub, 1), 0)
    parts = []
    for r0 in range(0, c, sub):
        qi, ki, vi, bi = (a[r0:r0 + sub, :] for a in (q, k, v, b))
        oi = o[r0:r0 + sub, :]
        for s in range(sub):
            d = jnp.where(row >= s, bi - bi[s:s + 1, :], -1e30)
            a_s = jnp.sum(qi * ki[s:s + 1, :] * jnp.exp(d), axis=-1, keepdims=True)
            oi = oi + a_s * vi[s:s + 1, :]
        if r0 > 0:
            b_ref = bi[0:1, :]
            q_rel = (qi * jnp.exp(bi - b_ref)).astype(BF16)
            k_rel = (k[:r0, :] * jnp.exp(b_ref - b[:r0, :])).astype(BF16)
            a = lax.dot_general(q_rel, k_rel, _NT_DIMS, preferred_element_type=F32)
            oi = oi + _dot(a.astype(BF16), v[:r0, :].astype(BF16))
        parts.append(oi)
    o = jnp.concatenate(parts, axis=0) if len(parts) > 1 else parts[0]
    ke = (k * jnp.exp(b_last - b)).astype(BF16)
    st_ref[head] = st * jnp.exp(b_last) + lax.dot_general(
        v.astype(BF16), ke, (((0,), (0,)), ((), ())), preferred_element_type=F32)
    return o


def _even_mixer_kernel(z_ref, s0_ref, wgk_ref, bgk_ref, lb_ref, na_ref, nb_ref,
                       o_ref, sout_ref, st_ref, *, t_valid):
    c = z_ref.shape[0]
    ci = pl.program_id(1)

    @pl.when(ci == 0)
    def _():
        st_ref[...] = s0_ref[0]

    def blk(i):
        return z_ref[:, i * LANES:(i + 1) * LANES]

    row = lax.broadcasted_iota(jnp.int32, (c, 1), 0)
    tri = (lax.broadcasted_iota(jnp.int32, (c, c), 0)
           >= lax.broadcasted_iota(jnp.int32, (c, c), 1)).astype(BF16)
    valid = row < t_valid

    def finish(o, gain, gate, head):
        o = o * lax.rsqrt(jnp.mean(o * o, axis=-1, keepdims=True) + RMS_EPS) * gain
        o_ref[:, head * LANES:(head + 1) * LANES] = o * (gate * jax.nn.sigmoid(gate))

    gk = _log_sigmoid(_dot(blk(4 * N_EVEN_HEADS).astype(BF16), wgk_ref[...]) + bgk_ref[...])
    gk = gk * (1.0 / GLA_GK_NORM)
    for h in range(H_A):
        g = gk[:, h * LANES:(h + 1) * LANES]
        k = blk(H_A + h)
        if t_valid < c:
            g = jnp.where(valid, g, 0.0)
            k = jnp.where(valid, k, 0.0)
        o = _gated_chunk(blk(h) * DK_A ** -0.5, k, blk(2 * H_A + h), g, st_ref, h, tri)
        finish(o, na_ref[...], blk(3 * H_A + h), h)

    base = 4 * H_A
    for h in range(H_B):
        lb = lb_ref[:, h * LANES:(h + 1) * LANES]
        fb = blk(base + H_B + h)
        a = jnp.log1p(-lb) + _log_sigmoid(fb)
        log_lb = jnp.log(lb)
        g = jnp.maximum(a, log_lb) + jnp.log1p(jnp.exp(-jnp.abs(a - log_lb)))
        k = (1.0 - lb) * jax.nn.sigmoid(-fb)
        if t_valid < c:
            g = jnp.where(valid, g, 0.0)
            k = jnp.where(valid, k, 0.0)
        o = _gated_chunk(blk(base + h), k, blk(base + 2 * H_B + h), g, st_ref, H_A + h, tri)
        finish(o, nb_ref[...], blk(base + 3 * H_B + h), H_A + h)

    @pl.when(ci == pl.num_programs(1) - 1)
    def _():
        sout_ref[0] = st_ref[...]


def even_mixer(z, s0, wgk, bgk, lb, norm_a, norm_b, *, n_seq, t, chunk, t_valid=None):
    ncol = z.shape[1]
    t_valid = chunk if t_valid is None else t_valid
    n_chunks = t // chunk
    st_shape = (N_EVEN_HEADS, LANES, LANES)
    return pl.pallas_call(
        functools.partial(_even_mixer_kernel, t_valid=t_valid),
        grid=(n_seq, n_chunks),
        in_specs=[pl.BlockSpec((chunk, ncol), lambda b, c: (b * n_chunks + c, 0)),
                  pl.BlockSpec((1,) + st_shape, lambda b, c: (b, 0, 0, 0)),
                  _resident(wgk.shape), _resident(bgk.shape), _resident(lb.shape),
                  _resident(norm_a.shape), _resident(norm_b.shape)],
        out_specs=[pl.BlockSpec((chunk, D_MODEL), lambda b, c: (b * n_chunks + c, 0)),
                   pl.BlockSpec((1,) + st_shape, lambda b, c: (b, 0, 0, 0))],
        out_shape=[jax.ShapeDtypeStruct((n_seq * t, D_MODEL), F32),
                   jax.ShapeDtypeStruct((n_seq,) + st_shape, F32)],
        scratch_shapes=[pltpu.VMEM(st_shape, F32)],
        compiler_params=_params("parallel", "arbitrary"),
        name="even_mixer",
    )(z, s0, wgk, bgk, lb, norm_a, norm_b)


def _pad_heads(w, n_heads):
    d = w.shape[-1] // n_heads
    w = w.reshape(w.shape[:-1] + (n_heads, d))
    w = jnp.pad(w, [(0, 0)] * (w.ndim - 1) + [(0, LANES - d)])
    return w.reshape(w.shape[:-2] + (n_heads * LANES,))


def _split_cols(w, sizes):
    out, o = [], 0
    for s in sizes:
        out.append(w[..., o:o + s])
        o += s
    return out


def even_weights(w_in, w_gk, b_gk):
    ka = H_A * DK_A
    qa, kk, va, gk_in, ga, qb, fb, ib, gb = _split_cols(
        w_in, (ka, ka, HALF, GLA_GK_RANK, HALF, HALF, HALF, HALF, HALF))
    gk_in = jnp.pad(gk_in, ((0, 0), (0, LANES - GLA_GK_RANK)))
    w = jnp.concatenate([_pad_heads(qa, H_A), _pad_heads(kk, H_A), va, ga, qb, fb, ib, gb, gk_in],
                        axis=-1).astype(BF16)
    wgk = jnp.pad(_pad_heads(w_gk, H_A), ((0, LANES - GLA_GK_RANK), (0, 0))).astype(BF16)
    bgk = _pad_heads(b_gk[None, :], H_A)
    return w, wgk, bgk


def run_even_mixer(z, n_p, t_p, n_s, t_s, state_gla, state_hgrn, wgk, bgk, lb, norm_a, norm_b,
                   *, chunk):
    ncol = z.shape[1]
    args = (wgk, bgk, lb[None, :], norm_a[None, :], norm_b[None, :])
    zero = jnp.zeros((n_p, N_EVEN_HEADS, LANES, LANES), F32)
    o_p, st_p = even_mixer(z, zero, *args, n_seq=n_p, t=t_p, chunk=chunk)
    zs = z[n_p * t_p:].reshape(n_s, t_s, ncol)
    zs = jnp.pad(zs, ((0, 0), (0, SUBLANES - t_s), (0, 0))).reshape(n_s * SUBLANES, ncol)
    s0 = jnp.concatenate([
        jnp.pad(jnp.swapaxes(state_gla, -1, -2), ((0, 0), (0, 0), (0, 0), (0, LANES - DK_A))),
        jnp.swapaxes(state_hgrn, -1, -2)], axis=1)
    o_s, st_s = even_mixer(zs, s0, *args, n_seq=n_s, t=SUBLANES, chunk=SUBLANES, t_valid=t_s)
    o_s = o_s.reshape(n_s, SUBLANES, D_MODEL)[:, :t_s].reshape(n_s * t_s, D_MODEL)

    def unpack(st):
        gla = jnp.swapaxes(st[:, :H_A], -1, -2)[:, :, :DK_A, :]
        return gla, jnp.swapaxes(st[:, H_A:], -1, -2)

    return jnp.concatenate([o_p, o_s], axis=0), unpack(st_p), unpack(st_s)


_NT_DIMS = (((1,), (1,)), ((), ()))
LOG2E = 1.4426950408889634


def _neg_softplus(z):
    return -(jnp.maximum(z, 0.0) + jnp.log1p(jnp.exp(-jnp.abs(z))))


def _suffix_tri(n):
    return (lax.broadcasted_iota(jnp.int32, (n, n), 0)
            >= lax.broadcasted_iota(jnp.int32, (n, n), 1)).astype(BF16)


def _sb_prompt_kernel(bias_ref, q_ref, k_ref, v_ref, o_ref, u_ref, tot_ref, carry_ref, acc_ref,
                      *, blk, group):
    pair = pl.program_id(1)
    qi = pl.program_id(2)
    lane = lax.broadcasted_iota(jnp.int32, (1, LANES), 1)
    head_a = lane < DH_C
    q2 = q_ref[...].astype(F32) * (DH_C ** -0.5 * LOG2E)
    qs = jnp.concatenate([jnp.where(head_a, q2, 0.0), jnp.where(head_a, 0.0, q2)], axis=0).astype(BF16)
    row = lax.broadcasted_iota(jnp.int32, (2 * blk, 1), 0)
    bias2 = jnp.where(row < blk, bias_ref[2 * pair], bias_ref[2 * pair + 1]) * LOG2E
    neg_tri = -_suffix_tri(blk)
    strict = lax.broadcasted_iota(jnp.int32, (2 * blk, blk), 1) < (row % blk)

    def rows_of(ref, j):
        return ref[pl.ds(pl.multiple_of(j * blk, blk), blk), :]

    def scores(slot, j, diagonal=False):
        b2 = jnp.where(j >= 0, bias2, -1e30)
        z2 = lax.dot_general(qs, rows_of(k_ref, jnp.maximum(j, 0)), _NT_DIMS,
                             preferred_element_type=F32) + b2
        if diagonal:
            z2 = jnp.where(strict, z2, -1e30)
        sp2 = jnp.maximum(z2, 0.0) + jnp.log2(1.0 + jnp.exp2(-jnp.abs(z2)))
        incl = _dot(sp2.astype(BF16), neg_tri)
        u_ref[slot] = z2 + incl
        tot_ref[slot] = incl[:, 0:1]

    def accumulate(slot, j):
        w = jnp.exp2(u_ref[slot] + carry_ref[...])
        acc_ref[...] += _dot(w.astype(BF16), rows_of(v_ref, jnp.maximum(j, 0)))
        carry_ref[...] += tot_ref[slot]

    def body(k, _):
        j = qi - group * k
        for s in range(group):
            accumulate(s, j + group - s)
        for s in range(group):
            scores(s, j - s)
        return 0

    carry_ref[...] = jnp.zeros(carry_ref.shape, F32)
    acc_ref[...] = jnp.zeros(acc_ref.shape, F32)
    for s in range(group):
        scores(s, qi - s, diagonal=(s == 0))
    lax.fori_loop(1, qi // group + 1, body, 0)
    for s in range(group):
        accumulate(s, qi % group - s)
    o_ref[...] = jnp.where(head_a, acc_ref[:blk, :], acc_ref[blk:, :])


def sb_attention_prompt(z, bias, *, n_seq, t, q_col, blk, group):
    nq = t // blk
    n_pairs = H_C // 2
    return pl.pallas_call(
        functools.partial(_sb_prompt_kernel, blk=blk, group=group),
        grid=(n_seq, n_pairs, nq),
        in_specs=[pl.BlockSpec(memory_space=pltpu.SMEM),
                  pl.BlockSpec((blk, LANES), lambda b, p, i: (b * nq + i, q_col + p)),
                  pl.BlockSpec((t, LANES), lambda b, p, i: (b, q_col + n_pairs + p)),
                  pl.BlockSpec((t, LANES), lambda b, p, i: (b, q_col + 2 * n_pairs + p))],
        out_specs=pl.BlockSpec((blk, LANES), lambda b, p, i: (b * nq + i, p)),
        out_shape=jax.ShapeDtypeStruct((n_seq * t, HALF), F32),
        scratch_shapes=[pltpu.VMEM((group, 2 * blk, blk), F32), pltpu.VMEM((group, 2 * blk, 1), F32),
                        pltpu.VMEM((2 * blk, 1), F32), pltpu.VMEM((2 * blk, LANES), F32)],
        compiler_params=_params("parallel", "parallel", "arbitrary"),
        name="sb_attention_prompt",
    )(bias, z, z, z)


def _sb_sample_kernel(pt_ref, q_ref, kn_ref, vn_ref, bias_ref, *refs, n_pages, t_s):
    del pt_ref
    k_pages, v_pages, o_ref = refs[:n_pages], refs[n_pages:2 * n_pages], refs[2 * n_pages]
    rows = t_s * H_C
    q = q_ref[0] * DH_C ** -0.5
    qx = jnp.concatenate([jnp.broadcast_to(q[i:i + 1, :], (H_C, HALF)) for i in range(t_s)], axis=0)
    r_id = lax.broadcasted_iota(jnp.int32, (rows, HALF), 0)
    l_id = lax.broadcasted_iota(jnp.int32, (rows, HALF), 1)
    head_lanes = (l_id // DH_C) == (r_id % H_C)
    qx = jnp.where(head_lanes, qx, 0.0)
    bias = bias_ref[...]
    q_idx = lax.broadcasted_iota(jnp.int32, (rows, 1), 0) // H_C

    carry = jnp.zeros((rows, 1), F32)
    acc = jnp.zeros((rows, HALF), F32)
    kn, vn = kn_ref[0], vn_ref[0]
    for j in reversed(range(t_s)):
        visible = q_idx > j
        z = jnp.sum(qx * kn[j:j + 1, :], axis=-1, keepdims=True) + bias[:, 0:1]
        lk = jnp.where(visible, _neg_softplus(z), 0.0)
        w = jnp.where(visible, jnp.exp(z + lk + carry), 0.0)
        acc = acc + w * vn[j:j + 1, :]
        carry = carry + lk

    qx = qx.astype(BF16)
    tri = _suffix_tri(PAGE_SIZE)
    for p in reversed(range(n_pages)):
        kb = k_pages[p][...].astype(BF16)
        vb = v_pages[p][...].astype(BF16)
        z = _dot(qx, kb) + bias
        incl = _dot(_neg_softplus(z).astype(BF16), tri)
        w = jnp.exp(z + incl + carry)
        acc = acc + lax.dot_general(w.astype(BF16), vb, _NT_DIMS, preferred_element_type=F32)
        carry = carry + incl[:, 0:1]

    acc = jnp.where(head_lanes, acc, 0.0)
    for i in range(t_s):
        o_ref[0, i:i + 1, :] = jnp.sum(acc[i * H_C:(i + 1) * H_C, :], axis=0, keepdims=True)


def sb_attention_sample(q, k_new, v_new, cache_k, cache_v, layer, page_table, bias_rows):
    n_seq, t_s, _ = q.shape
    n_pages = page_table.shape[0] // n_seq
    tok = pl.BlockSpec((1, t_s, HALF), lambda b, pt: (b, 0, 0))

    def page_spec(p):
        return pl.BlockSpec((None, None, HALF, PAGE_SIZE),
                            lambda b, pt: (layer, pt[b * n_pages + p], 0, 0))

    pages = [page_spec(p) for p in range(n_pages)]
    return pl.pallas_call(
        functools.partial(_sb_sample_kernel, n_pages=n_pages, t_s=t_s),
        grid_spec=pltpu.PrefetchScalarGridSpec(
            num_scalar_prefetch=1,
            grid=(n_seq,),
            in_specs=[tok, tok, tok,
                      pl.BlockSpec(bias_rows.shape, lambda b, pt: (0, 0))] + pages + pages,
            out_specs=tok),
        out_shape=jax.ShapeDtypeStruct((n_seq, t_s, HALF), F32),
        compiler_params=_params("parallel"),
        name="sb_attention_sample",
    )(page_table, q, k_new, v_new, bias_rows, *([cache_k] * n_pages), *([cache_v] * n_pages))


N_SCAN_ROWS = 13
N_SCAN_STEP_ROWS = 9
N_PAIRS_D = H_D // 2


def _head_sum(x, seg):
    hi, lo = _split_bf16(x)
    return _dot(hi, seg) + _dot(lo, seg)


def _rwkv_prep_kernel(zd_ref, zb_ref, prev_s_ref, mu_ref, w0_ref, w2_ref, a0_ref, a2_ref, g2_ref,
                      kk_ref, ka_ref, rk_ref, seg_ref, rows_ref, v_ref, gb_ref,
                      *, tiles_per_seq, n_prompt_tiles):
    i = pl.program_id(0)
    z = zd_ref[...]
    tm = z.shape[0]
    boundary = jnp.where(i % tiles_per_seq == 0, 0.0, 1.0) * zb_ref[SUBLANES - 1:SUBLANES, :]
    row = lax.broadcasted_iota(jnp.int32, (tm, 1), 0)
    prev = jnp.where(row == 0, boundary, pltpu.roll(z, 1, axis=0))
    prev = jnp.where(i >= n_prompt_tiles, prev_s_ref[...], prev)
    zs = z + (prev - z) * mu_ref[...]
    r, k, v = zs[:, 0:DM_D], zs[:, DM_D:2 * DM_D], zs[:, 2 * DM_D:3 * DM_D]
    u = zs[:, 3 * DM_D:3 * DM_D + R_W + R_A]
    g_in = zs[:, 3 * DM_D + R_W + R_A:]
    seg = seg_ref[...]
    w = w0_ref[...] + _dot(jnp.tanh(u).astype(BF16), w2_ref[...])
    decay = jnp.exp(-jnp.exp(_log_sigmoid(w) - 0.5))
    a = jax.nn.sigmoid(a0_ref[...] + _dot(u.astype(BF16), a2_ref[...]))
    g = _dot(jax.nn.sigmoid(g_in).astype(BF16), g2_ref[...])
    kk = k * kk_ref[...]
    kk = kk * lax.rsqrt(jnp.maximum(_head_sum(kk * kk, seg), 1e-24))
    kp = k * (1.0 + (a - 1.0) * ka_ref[...])
    beta = kk * a
    wr = decay * r
    odd = (row % 2) == 1
    w_a, beta_a, kp_a = (pltpu.roll(t, 1, axis=0) for t in (decay, beta, kp))
    outs = (jnp.where(odd, w_a * kk, kk), jnp.where(odd, w_a * wr, wr),
            w_a * decay, beta_a * decay, kp_a * decay, beta, kp,
            _head_sum(beta_a * kk, seg), _head_sum(kp_a * kk, seg),
            jnp.where(odd, _head_sum(beta_a * wr, seg), 0.0),
            jnp.where(odd, _head_sum(kp_a * wr, seg), 0.0),
            _head_sum(beta * r, seg), _head_sum(kp * r, seg))
    for n, val in enumerate(outs):
        rows_ref[:, n * DM_D:(n + 1) * DM_D] = val
    v_ref[...] = v
    gb_ref[:, 0:DM_D] = g
    gb_ref[:, DM_D:] = _head_sum(r * kp * rk_ref[...], seg) * v


def rwkv_prep(z, prev_s, mu, w0, w2, a0, a2, g2, k_k, k_a, r_k, seg, *, n_prompt_tok, t_prompt):
    nt = z.shape[0]
    tm = TOKEN_TILE
    vec = _resident((1, DM_D))
    tok = lambda n: pl.BlockSpec((tm, n), lambda i: (i, 0))
    return pl.pallas_call(
        functools.partial(_rwkv_prep_kernel, tiles_per_seq=t_prompt // tm,
                          n_prompt_tiles=n_prompt_tok // tm),
        grid=(nt // tm,),
        in_specs=[tok(N_COLS_RWKV),
                  pl.BlockSpec((SUBLANES, N_COLS_RWKV),
                               lambda i: (jnp.maximum(i * (tm // SUBLANES) - 1, 0), 0)),
                  _resident(prev_s.shape), _resident((1, N_COLS_RWKV)),
                  vec, _resident(w2.shape), vec, _resident(a2.shape), _resident(g2.shape),
                  vec, vec, vec, _resident(seg.shape)],
        out_specs=[tok(N_SCAN_ROWS * DM_D), tok(DM_D), tok(2 * DM_D)],
        out_shape=[jax.ShapeDtypeStruct((nt, N_SCAN_ROWS * DM_D), F32),
                   jax.ShapeDtypeStruct((nt, DM_D), F32),
                   jax.ShapeDtypeStruct((nt, 2 * DM_D), F32)],
        compiler_params=_params("parallel"),
        name="rwkv_prep",
    )(z, z, prev_s, mu, w0, w2, a0, a2, g2, k_k, k_a, r_k, seg)


def _rwkv_scan_kernel(*refs, nb, t_c, rows_per_seq):
    n_in = nb if rows_per_seq else 1
    rows_refs, v_refs = refs[:n_in], refs[n_in:2 * n_in]
    s0_ref, o_ref, sout_ref, s_scr, vt_scr, sk_scr, pr_scr = refs[2 * n_in:]
    ci = pl.program_id(1)

    @pl.when(ci == 0)
    def _():
        s_scr[...] = s0_ref[...]

    def seq_rows(ref_list, kq, start, size):
        if rows_per_seq:
            return ref_list[kq][pl.ds(start, size), :]
        return ref_list[0][pl.ds(kq * t_c + start, size), :]

    lane = lax.broadcasted_iota(jnp.int32, (1, LANES), 1)
    head_a = lane < DH_D
    ones_bd = ((lax.broadcasted_iota(jnp.int32, (LANES, LANES), 0) // DH_D)
               == (lax.broadcasted_iota(jnp.int32, (LANES, LANES), 1) // DH_D)).astype(BF16)

    spb = 1 if rows_per_seq else nb
    n_sel = spb * t_c
    sel = ((lax.broadcasted_iota(jnp.int32, (n_sel, spb * LANES), 0) // t_c
            == lax.broadcasted_iota(jnp.int32, (n_sel, spb * LANES), 1) // LANES)
           & (lax.broadcasted_iota(jnp.int32, (n_sel, spb * LANES), 0) % t_c
              == lax.broadcasted_iota(jnp.int32, (n_sel, spb * LANES), 1) % DH_D)).astype(BF16)
    for blk_i in range(n_in):
        for p in range(N_PAIRS_D):
            vp = v_refs[blk_i][:, p * LANES:(p + 1) * LANES].astype(BF16)
            vt = lax.dot_general(vp, sel, (((0,), (0,)), ((), ())), preferred_element_type=F32)
            for q in range(spb):
                blk = vt[:, q * LANES:(q + 1) * LANES]
                vt_scr[blk_i * spb + q, p] = jnp.where(head_a, blk[:DH_D], blk[DH_D:]).astype(BF16)
    sk_scr[...] = jnp.zeros(sk_scr.shape, F32)
    pr_scr[...] = jnp.zeros(pr_scr.shape, F32)

    def body(i, carry):
        ta = 2 * i
        here_a = (lane % DH_D) == ta
        here_b = (lane % DH_D) == ta + 1
        ha16, hb16 = here_a.astype(BF16), here_b.astype(BF16)
        for kq in range(nb):
            ra = seq_rows(rows_refs, kq, ta, 1)
            rb = seq_rows(rows_refs, kq, ta + 1, 1)
            for p in range(N_PAIRS_D):
                def col(r, n):
                    return r[:, n * DM_D + p * LANES:n * DM_D + (p + 1) * LANES]
                w2, b1, k1, beta_b, kp_b, x1, x2 = (col(rb, n) for n in range(2, N_SCAN_STEP_ROWS))
                sp = s_scr[kq, p]
                s16 = sp.astype(BF16)
                vt = vt_scr[kq, p]
                lhs = jnp.concatenate(
                    [s16 * col(ra, 0).astype(BF16), s16 * col(ra, 1).astype(BF16),
                     s16 * col(rb, 0).astype(BF16), s16 * col(rb, 1).astype(BF16),
                     vt * ha16, vt * hb16], axis=0)
                red = _dot(lhs, ones_bd)
                skk_a, pr_a, r3, pr_b, v_a, v_b = (red[n * DH_D:(n + 1) * DH_D] for n in range(6))
                skk_b = r3 - skk_a * x1 + v_a * x2
                s_scr[kq, p] = sp * w2 - skk_a * b1 + v_a * k1 - skk_b * beta_b + v_b * kp_b
                sk_scr[kq, p] = jnp.where(here_a, skk_a, jnp.where(here_b, skk_b, sk_scr[kq, p]))
                pr_scr[kq, p] = jnp.where(here_a, pr_a, jnp.where(here_b, pr_b, pr_scr[kq, p]))
        return carry

    lax.fori_loop(0, t_c // 2, body, 0)

    def token_major(tile):
        x = tile.T
        return jnp.concatenate([x[0:t_c], x[DH_D:DH_D + t_c]], axis=1)

    for kq in range(nb):
        rows = seq_rows(rows_refs, kq, 0, t_c)
        v = seq_rows(v_refs, kq, 0, t_c)
        for p in range(N_PAIRS_D):
            x3, x4, c1, c2 = (rows[:, n * DM_D + p * LANES:n * DM_D + (p + 1) * LANES]
                              for n in range(N_SCAN_STEP_ROWS, N_SCAN_ROWS))
            skt = token_major(sk_scr[kq, p])
            vp = v[:, p * LANES:(p + 1) * LANES]
            o = (token_major(pr_scr[kq, p]) - skt * c1 + vp * c2
                 - jnp.roll(skt, 1, axis=0) * x3 + jnp.roll(vp, 1, axis=0) * x4)
            o_ref[kq, :, p * LANES:(p + 1) * LANES] = o

    @pl.when(ci == pl.num_programs(1) - 1)
    def _():
        sout_ref[...] = s_scr[...]


def rwkv_scan(rows, v, s0, *, tok0, t, nb):
    n_seq = s0.shape[0]
    rows_per_seq = t >= DH_D
    t_c = DH_D if rows_per_seq else t
    n_chunks = t // t_c
    if rows_per_seq:
        def tok_spec(ncol):
            return [pl.BlockSpec((t_c, ncol),
                                 lambda g, c, kq=kq: (tok0 // t_c + (g * nb + kq) * n_chunks + c, 0))
                    for kq in range(nb)]
        n_in = nb
    else:
        def tok_spec(ncol):
            return [pl.BlockSpec((nb * t_c, ncol), lambda g, c: (tok0 // (nb * t_c) + g, 0))]
        n_in = 1
    st_shape = (nb,) + s0.shape[1:]
    st_spec = pl.BlockSpec(st_shape, lambda g, c: (g, 0, 0, 0))
    return pl.pallas_call(
        functools.partial(_rwkv_scan_kernel, nb=nb, t_c=t_c, rows_per_seq=rows_per_seq),
        grid=(n_seq // nb, n_chunks),
        in_specs=tok_spec(rows.shape[1]) + tok_spec(DM_D) + [st_spec],
        out_specs=[pl.BlockSpec((nb, t_c, DM_D), lambda g, c: (g, c, 0)), st_spec],
        out_shape=[jax.ShapeDtypeStruct((n_seq, t, DM_D), F32), jax.ShapeDtypeStruct(s0.shape, F32)],
        scratch_shapes=[pltpu.VMEM(st_shape, F32), pltpu.VMEM(st_shape, BF16),
                        pltpu.VMEM(st_shape, F32), pltpu.VMEM(st_shape, F32)],
        compiler_params=_params("parallel", "arbitrary"),
        name="rwkv_scan",
    )(*([rows] * n_in), *([v] * n_in), s0)


def _odd_out_kernel(x_ref, oc_ref, od_ref, gb_ref, lnw_ref, lnb_ref, seg_ref, w_ref, o_ref):
    seg = seg_ref[...]
    o = od_ref[...]
    d = o - _head_sum(o, seg) * (1.0 / DH_D)
    var = _head_sum(d * d, seg) * (1.0 / DH_D)
    od = d * lax.rsqrt(var + RWKV_LN_EPS) * lnw_ref[...] + lnb_ref[...]
    od = (od + gb_ref[:, DM_D:]) * gb_ref[:, 0:DM_D]
    a = jnp.concatenate([oc_ref[...], od], axis=-1).astype(BF16)
    o_ref[...] = x_ref[...] + _dot(a, w_ref[...])


def odd_out(x, o_c, o_d, gb, ln_w, ln_b, seg, w_out):
    nt = x.shape[0]
    tok = lambda n: pl.BlockSpec((TOKEN_TILE, n), lambda i: (i, 0))
    vec = _resident((1, DM_D))
    return pl.pallas_call(
        _odd_out_kernel,
        grid=(nt // TOKEN_TILE,),
        in_specs=[tok(D_MODEL), tok(HALF), tok(DM_D), tok(2 * DM_D), vec, vec,
                  _resident(seg.shape), _resident(w_out.shape)],
        out_specs=tok(D_MODEL),
        out_shape=jax.ShapeDtypeStruct(x.shape, F32),
        compiler_params=_params("parallel"),
        name="odd_out",
    )(x, o_c, o_d, gb, ln_w, ln_b, seg, w_out)


def _state_to_pairs(s):
    n_seq = s.shape[0]
    s = s.reshape(n_seq, N_PAIRS_D, 2, DH_D, DH_D)
    return jnp.transpose(s, (0, 1, 3, 2, 4)).reshape(n_seq, N_PAIRS_D, DH_D, LANES)


def _state_from_pairs(s):
    n_seq = s.shape[0]
    s = s.reshape(n_seq, N_PAIRS_D, DH_D, 2, DH_D)
    return jnp.transpose(s, (0, 1, 3, 2, 4)).reshape(n_seq, H_D, DH_D, DH_D)


def run_rwkv(z, n_p, t_p, n_s, t_s, state, shift, mu, w0, w2, a0, a2, g2, k_k, k_a, r_k, seg,
             *, nb_sample):
    np_tok = n_p * t_p
    zd_s = z[np_tok:, :N_COLS_RWKV].reshape(n_s, t_s, N_COLS_RWKV)
    prev_s = jnp.concatenate([shift[:, None, :], zd_s[:, :-1]], axis=1).reshape(n_s * t_s, N_COLS_RWKV)
    w2p = jnp.pad(w2, ((0, R_A), (0, 0))).astype(BF16)
    a2p = jnp.pad(a2, ((R_W, 0), (0, 0))).astype(BF16)
    row = lambda a: a.reshape(1, -1)
    rows, v, gb = rwkv_prep(z, prev_s, row(mu), row(w0), w2p, row(a0), a2p, g2.astype(BF16),
                            row(k_k), row(k_a), row(r_k), seg, n_prompt_tok=np_tok, t_prompt=t_p)
    zero = jnp.zeros((n_p, N_PAIRS_D, DH_D, LANES), F32)
    o_p, st_p = rwkv_scan(rows, v, zero, tok0=0, t=t_p, nb=n_p)
    o_s, st_s = rwkv_scan(rows, v, _state_to_pairs(state), tok0=np_tok, t=t_s, nb=nb_sample)
    o = jnp.concatenate([o_p.reshape(np_tok, DM_D), o_s.reshape(n_s * t_s, DM_D)], axis=0)
    return o, gb, _state_from_pairs(st_p), _state_from_pairs(st_s)


EVEN_CHUNK = 64
SB_BLOCK = 256
SB_GROUP = 4
RWKV_SAMPLE_GROUP = 8


def kernel(x_prompt, x_sample, p_prompt, p_sample, cache_k, cache_v, page_table, state_gla, state_hgrn, state_rwkv, state_rwkv_shift, ln_ffn1, ffn1_w_gate, ffn1_w_up, ffn1_w_down, ln_mix, ln_ffn2, ffn2_w_gate, ffn2_w_up, ffn2_w_down, ln_ple, ple_w_gate, ple_w_proj, w_in_even, w_out_even, gla_w_gk, gla_b_gk, gla_norm, hgrn_lb_logits, hgrn_norm, w_in_odd, w_out_odd, sb_bias, rwkv_mu, rwkv_w0, rwkv_w2, rwkv_a0, rwkv_a2, rwkv_g2, rwkv_k_k, rwkv_k_a, rwkv_r_k, rwkv_ln_w, rwkv_ln_b, final_norm):
    n_p, t_p, _ = x_prompt.shape
    n_s, t_s, _ = x_sample.shape
    np_tok, ns_tok = n_p * t_p, n_s * t_s
    assert ns_tok == TOKEN_TILE and t_p % TOKEN_TILE == 0 and t_s < SUBLANES
    bf = lambda w: w.astype(BF16)
    row = lambda a: a.reshape(1, -1)

    x = jnp.concatenate([x_prompt.reshape(np_tok, D_MODEL), x_sample.reshape(ns_tok, D_MODEL)], axis=0)
    p_all = jnp.concatenate([p_prompt.reshape(DEPTH, np_tok, PLE_DIM),
                             p_sample.reshape(DEPTH, ns_tok, PLE_DIM)], axis=1)
    sm = jax.nn.softmax(hgrn_lb_logits.astype(F32), axis=0)
    lower_bounds = jnp.concatenate([jnp.zeros_like(sm[:1]), jnp.cumsum(sm[1:], axis=0)], axis=0)
    head_id = jnp.arange(DM_D) // DH_D
    seg = (head_id[:, None] == head_id[None, :]).astype(BF16)
    n_phys = cache_k.shape[1]
    ck = jnp.transpose(cache_k, (0, 1, 3, 4, 2)).reshape(cache_k.shape[0], n_phys, HALF, PAGE_SIZE)
    cv = jnp.transpose(cache_v, (0, 1, 3, 4, 2)).reshape(cache_v.shape[0], n_phys, HALF, PAGE_SIZE)
    pt_flat = page_table.reshape(-1)
    q0 = N_COLS_RWKV
    k0, v0 = q0 + HALF, q0 + 2 * HALF

    k_rows_p, v_rows_p, k_rows_s, v_rows_s = [], [], [], []
    gla_p, gla_s, hgrn_p, hgrn_s = [], [], [], []
    rwkv_p, rwkv_s, shift_p, shift_s = [], [], [], []
    for i in range(DEPTH):
        j = i // 2
        x = ffn(x, row(ln_ffn1[i]), bf(ffn1_w_gate[i]), bf(ffn1_w_up[i]), bf(ffn1_w_down[i]))
        if i % 2 == 0:
            w_in, wgk, bgk = even_weights(w_in_even[j], gla_w_gk[j], gla_b_gk[j])
            z = rms_matmul(x, row(ln_mix[i]), w_in)
            o, (ga_p, hb_p), (ga_s, hb_s) = run_even_mixer(
                z, n_p, t_p, n_s, t_s, state_gla[j], state_hgrn[j], wgk, bgk,
                lower_bounds[j], gla_norm[j], hgrn_norm[j], chunk=EVEN_CHUNK)
            x = matmul_residual(x, o, bf(w_out_even[j]))
            gla_p.append(ga_p)
            gla_s.append(ga_s)
            hgrn_p.append(hb_p)
            hgrn_s.append(hb_s)
        else:
            w = w_in_odd[j]
            z, qkv = rms_matmul(x, row(ln_mix[i]),
                                bf(jnp.concatenate([w[:, 3 * HALF:], w[:, :3 * HALF]], axis=1)), bf16_from=q0)
            oc_p = sb_attention_prompt(qkv, sb_bias[j], n_seq=n_p, t=t_p, q_col=0, blk=SB_BLOCK,
                                       group=SB_GROUP)
            z_s = z[np_tok:]
            q_s, kc_s, vc_s = (z_s[:, c:c + HALF].reshape(n_s, t_s, HALF) for c in (q0, k0, v0))
            bias_rows = jnp.broadcast_to(jnp.tile(sb_bias[j], t_s)[:, None], (t_s * H_C, PAGE_SIZE))
            oc_s = sb_attention_sample(q_s, kc_s, vc_s, ck, cv, j, pt_flat, bias_rows)
            o_c = jnp.concatenate([oc_p, oc_s.reshape(ns_tok, HALF)], axis=0)
            o_d, gb, sd_p, sd_s = run_rwkv(
                z, n_p, t_p, n_s, t_s, state_rwkv[j], state_rwkv_shift[j], rwkv_mu[j], rwkv_w0[j],
                rwkv_w2[j], rwkv_a0[j], rwkv_a2[j], rwkv_g2[j], rwkv_k_k[j], rwkv_k_a[j],
                rwkv_r_k[j], seg, nb_sample=RWKV_SAMPLE_GROUP)
            x = odd_out(x, o_c, o_d, gb, row(rwkv_ln_w[j]), row(rwkv_ln_b[j]), seg, bf(w_out_odd[j]))
            z_p = z[:np_tok].reshape(n_p, t_p, -1)
            k_rows_p.append(z_p[:, :, k0:k0 + HALF].reshape(n_p, t_p, H_C, DH_C))
            v_rows_p.append(z_p[:, :, v0:v0 + HALF].reshape(n_p, t_p, H_C, DH_C))
            k_rows_s.append(kc_s.reshape(n_s, t_s, H_C, DH_C))
            v_rows_s.append(vc_s.reshape(n_s, t_s, H_C, DH_C))
            rwkv_p.append(sd_p)
            rwkv_s.append(sd_s)
            shift_p.append(z_p[:, -1, :N_COLS_RWKV])
            shift_s.append(z_s.reshape(n_s, t_s, -1)[:, -1, :N_COLS_RWKV])
        x = ffn(x, row(ln_ffn2[i]), bf(ffn2_w_gate[i]), bf(ffn2_w_up[i]), bf(ffn2_w_down[i]))
        x = ple(x, p_all[i], row(ln_ple[i]), bf(ple_w_gate[i]), bf(ple_w_proj[i]))

    y = final_rmsnorm(x, row(final_norm))
    return (y[:np_tok].reshape(n_p, t_p, D_MODEL), y[np_tok:].reshape(n_s, t_s, D_MODEL),
            jnp.stack(k_rows_p), jnp.stack(v_rows_p), jnp.stack(k_rows_s), jnp.stack(v_rows_s),
            jnp.stack(gla_p), jnp.stack(gla_s), jnp.stack(hgrn_p), jnp.stack(hgrn_s),
            jnp.stack(rwkv_p), jnp.stack(rwkv_s), jnp.stack(shift_p), jnp.stack(shift_s))
```

```python
import functools

import jax
import jax.numpy as jnp
from jax import lax
from jax.experimental import pallas as pl
from jax.experimental.pallas import tpu as pltpu

F32 = jnp.float32
BF16 = jnp.bfloat16

D_MODEL = 1024
DEPTH = 4
PAGE_SIZE = 128
HALF = D_MODEL // 2
D_FF = 2816
PLE_DIM = 256
RMS_EPS = 1e-6
H_A = 4
DK_A = HALF // H_A // 2
DV_A = HALF // H_A
GLA_GK_RANK = 16
GLA_GK_NORM = 16.0
H_B = 4
EXP_B = 128
DV_B = HALF // H_B
H_C = 8
DH_C = HALF // H_C
H_D = 8
DH_D = HALF // H_D
DM_D = H_D * DH_D
R_W = 64
R_A = 64
R_G = 128
RWKV_LN_EPS = 64e-5
N_COLS_RWKV = 3 * DM_D + R_W + R_A + R_G
N_COLS_ODD = 3 * HALF + N_COLS_RWKV

LANES = 128
SUBLANES = 8
MXU_DIM = 256
VMEM_LIMIT_BYTES = 56 * 1024 * 1024

TOKEN_TILE = 512
N_EVEN_HEADS = H_A + H_B
N_COLS_EVEN_PAD = (4 * N_EVEN_HEADS + 1) * LANES
GATED_SUB_BLOCK = 16


def _params(*semantics):
    return pltpu.CompilerParams(dimension_semantics=semantics,
                                vmem_limit_bytes=VMEM_LIMIT_BYTES)


def _resident(shape):
    return pl.BlockSpec(shape, lambda *_: (0,) * len(shape), pipeline_mode=pl.Buffered(1))


def _rms(x, g):
    return x * lax.rsqrt(jnp.mean(x * x, axis=-1, keepdims=True) + RMS_EPS) * g


def _dot(a, b):
    return jnp.dot(a, b, preferred_element_type=F32)


def _ffn_kernel(x_ref, ln_ref, wg_ref, wu_ref, wd_ref, o_ref):
    x = x_ref[...]
    h = _rms(x, ln_ref[...]).astype(BF16)
    acc = jnp.zeros(x.shape, F32)
    for f0 in range(0, D_FF, MXU_DIM):
        g = _dot(h, wg_ref[:, f0:f0 + MXU_DIM])
        u = _dot(h, wu_ref[:, f0:f0 + MXU_DIM])
        a = (g * jax.nn.sigmoid(g) * u).astype(BF16)
        acc = acc + _dot(a, wd_ref[f0:f0 + MXU_DIM, :])
    o_ref[...] = x + 0.5 * acc


def ffn(x, ln, wg, wu, wd):
    nt = x.shape[0]
    tok = pl.BlockSpec((TOKEN_TILE, D_MODEL), lambda i: (i, 0))
    return pl.pallas_call(
        _ffn_kernel,
        grid=(nt // TOKEN_TILE,),
        in_specs=[tok, _resident((1, D_MODEL)), _resident((D_MODEL, D_FF)),
                  _resident((D_MODEL, D_FF)), _resident((D_FF, D_MODEL))],
        out_specs=tok,
        out_shape=jax.ShapeDtypeStruct(x.shape, F32),
        compiler_params=_params("parallel"),
        name="ffn",
    )(x, ln, wg, wu, wd)


def _rms_matmul_kernel(x_ref, ln_ref, w_ref, o_ref, *maybe_ob_ref, n_chunk, bf16_from):
    h = _rms(x_ref[...], ln_ref[...]).astype(BF16)
    n = w_ref.shape[1]
    for n0 in range(0, n, n_chunk):
        n1 = min(n0 + n_chunk, n)
        o_ref[:, n0:n1] = _dot(h, w_ref[:, n0:n1])
    if maybe_ob_ref:
        maybe_ob_ref[0][...] = o_ref[:, bf16_from:].astype(BF16)


def rms_matmul(x, ln, w, bf16_from=None):
    nt, n = x.shape[0], w.shape[1]
    out_specs = [pl.BlockSpec((TOKEN_TILE, n), lambda i: (i, 0))]
    out_shape = [jax.ShapeDtypeStruct((nt, n), F32)]
    if bf16_from is not None:
        out_specs.append(pl.BlockSpec((TOKEN_TILE, n - bf16_from), lambda i: (i, 0)))
        out_shape.append(jax.ShapeDtypeStruct((nt, n - bf16_from), BF16))
    out = pl.pallas_call(
        functools.partial(_rms_matmul_kernel, n_chunk=2 * MXU_DIM, bf16_from=bf16_from),
        grid=(nt // TOKEN_TILE,),
        in_specs=[pl.BlockSpec((TOKEN_TILE, D_MODEL), lambda i: (i, 0)),
                  _resident((1, D_MODEL)), _resident((D_MODEL, n))],
        out_specs=out_specs,
        out_shape=out_shape,
        compiler_params=_params("parallel"),
        name="rms_matmul",
    )(x, ln, w)
    return out[0] if bf16_from is None else out


def _matmul_residual_kernel(x_ref, a_ref, w_ref, o_ref):
    o_ref[...] = x_ref[...] + _dot(a_ref[...].astype(BF16), w_ref[...])


def matmul_residual(x, a, w):
    nt, k = a.shape
    return pl.pallas_call(
        _matmul_residual_kernel,
        grid=(nt // TOKEN_TILE,),
        in_specs=[pl.BlockSpec((TOKEN_TILE, D_MODEL), lambda i: (i, 0)),
                  pl.BlockSpec((TOKEN_TILE, k), lambda i: (i, 0)),
                  _resident((k, D_MODEL))],
        out_specs=pl.BlockSpec((TOKEN_TILE, D_MODEL), lambda i: (i, 0)),
        out_shape=jax.ShapeDtypeStruct(x.shape, F32),
        compiler_params=_params("parallel"),
        name="matmul_residual",
    )(x, a, w)


def _ple_kernel(x_ref, p_ref, ln_ref, wg_ref, wp_ref, o_ref):
    x = x_ref[...]
    h = _rms(x, ln_ref[...]).astype(BF16)
    gate = jax.nn.sigmoid(_dot(h, wg_ref[...]))
    o_ref[...] = x + gate * _dot(p_ref[...].astype(BF16), wp_ref[...])


def ple(x, p, ln, wg, wp):
    nt = x.shape[0]
    tok = pl.BlockSpec((TOKEN_TILE, D_MODEL), lambda i: (i, 0))
    return pl.pallas_call(
        _ple_kernel,
        grid=(nt // TOKEN_TILE,),
        in_specs=[tok, pl.BlockSpec((TOKEN_TILE, PLE_DIM), lambda i: (i, 0)),
                  _resident((1, D_MODEL)), _resident((D_MODEL, D_MODEL)),
                  _resident((PLE_DIM, D_MODEL))],
        out_specs=tok,
        out_shape=jax.ShapeDtypeStruct(x.shape, F32),
        compiler_params=_params("parallel"),
        name="ple",
    )(x, p, ln, wg, wp)


def _final_norm_kernel(x_ref, ln_ref, o_ref):
    o_ref[...] = _rms(x_ref[...], ln_ref[...])


def final_rmsnorm(x, ln):
    nt = x.shape[0]
    tok = pl.BlockSpec((TOKEN_TILE, D_MODEL), lambda i: (i, 0))
    return pl.pallas_call(
        _final_norm_kernel,
        grid=(nt // TOKEN_TILE,),
        in_specs=[tok, _resident((1, D_MODEL))],
        out_specs=tok,
        out_shape=jax.ShapeDtypeStruct(x.shape, F32),
        compiler_params=_params("parallel"),
        name="final_norm",
    )(x, ln)


def _log_sigmoid(x):
    return jnp.minimum(x, 0.0) - jnp.log1p(jnp.exp(-jnp.abs(x)))


def _split_bf16(x):
    hi = x.astype(BF16)
    lo = (x - hi.astype(F32)).astype(BF16)
    return hi, lo


def _gated_chunks(heads, st_ref, tri):
    c = heads[0][0].shape[0]
    n = len(heads)
    sub = min(c, GATED_SUB_BLOCK)
    row = lax.broadcasted_iota(jnp.int32, (sub, 1), 0)
    tn_dims = (((0,), (0,)), ((), ()))

    bs = []
    for q, k, v, g in heads:
        g_hi, g_lo = _split_bf16(g * LOG2E)
        bs.append(_dot(tri, g_hi) + _dot(tri, g_lo))
    sts = [st_ref[h] for h in range(n)]
    o_inter = [lax.dot_general((heads[h][0] * jnp.exp2(bs[h])).astype(BF16), sts[h].astype(BF16),
                               _NT_DIMS, preferred_element_type=F32) for h in range(n)]

    scores = [[None] * (c // sub) for _ in range(n)]
    for h, (q, k, v, g) in enumerate(heads):
        b = bs[h]
        for i, r0 in enumerate(range(sub, c, sub), start=1):
            b_ref = b[r0:r0 + 1, :]
            q_rel = (q[r0:r0 + sub, :] * jnp.exp2(b[r0:r0 + sub, :] - b_ref)).astype(BF16)
            k_rel = (k[:r0, :] * jnp.exp2(b_ref - b[:r0, :])).astype(BF16)
            scores[h][i] = lax.dot_general(q_rel, k_rel, _NT_DIMS, preferred_element_type=F32)
    off_diag = [[None] * (c // sub) for _ in range(n)]
    for h, (q, k, v, g) in enumerate(heads):
        for i, r0 in enumerate(range(sub, c, sub), start=1):
            off_diag[h][i] = _dot(scores[h][i].astype(BF16), v[:r0, :].astype(BF16))

    for h, (q, k, v, g) in enumerate(heads):
        b_last = bs[h][c - 1:c, :]
        ke = (k * jnp.exp2(b_last - bs[h])).astype(BF16)
        st_ref[h] = sts[h] * jnp.exp2(b_last) + lax.dot_general(
            v.astype(BF16), ke, tn_dims, preferred_element_type=F32)

    outs = []
    for h, (q, k, v, g) in enumerate(heads):
        parts = []
        for i, r0 in enumerate(range(0, c, sub)):
            ki, vi, bi = (a[r0:r0 + sub, :] for a in (k, v, bs[h]))
            oi = o_inter[h][r0:r0 + sub, :]
            if i > 0:
                oi = oi + off_diag[h][i]
            for p0 in range(0, sub, SUBLANES):
                qp = q[r0 + p0:r0 + p0 + SUBLANES, :]
                bp = bi[p0:p0 + SUBLANES, :]
                op = oi[p0:p0 + SUBLANES, :]
                for s in range(min(sub, p0 + SUBLANES)):
                    d = bp - bi[s:s + 1, :]
                    if s > p0:
                        d = jnp.where(row[p0:p0 + SUBLANES] >= s, d, -1e30)
                    a_s = jnp.sum(qp * ki[s:s + 1, :] * jnp.exp2(d), axis=-1, keepdims=True)
                    op = op + a_s * vi[s:s + 1, :]
                parts.append(op)
        outs.append(jnp.concatenate(parts, axis=0) if len(parts) > 1 else parts[0])
    return outs


def _even_mixer_kernel(z_ref, s0_ref, wgk_ref, bgk_ref, lb_ref, na_ref, nb_ref,
                       o_ref, sout_ref, st_ref, *, t_valid):
    c = z_ref.shape[0]
    ci = pl.program_id(1)

    @pl.when(ci == 0)
    def _():
        st_ref[...] = s0_ref[0]

    def blk(i):
        return z_ref[:, i * LANES:(i + 1) * LANES]

    row = lax.broadcasted_iota(jnp.int32, (c, 1), 0)
    tri = (lax.broadcasted_iota(jnp.int32, (c, c), 0)
           >= lax.broadcasted_iota(jnp.int32, (c, c), 1)).astype(BF16)
    valid = row < t_valid

    def masked(g, k):
        if t_valid < c:
            return jnp.where(valid, g, 0.0), jnp.where(valid, k, 0.0)
        return g, k

    heads, gains, gates = [], [], []
    gk = _log_sigmoid(_dot(blk(4 * N_EVEN_HEADS).astype(BF16), wgk_ref[...]) + bgk_ref[...])
    gk = gk * (1.0 / GLA_GK_NORM)
    for h in range(H_A):
        g, k = masked(gk[:, h * LANES:(h + 1) * LANES], blk(H_A + h))
        heads.append((blk(h) * DK_A ** -0.5, k, blk(2 * H_A + h), g))
        gains.append(na_ref[...])
        gates.append(blk(3 * H_A + h))

    base = 4 * H_A
    for h in range(H_B):
        lb = lb_ref[:, h * LANES:(h + 1) * LANES]
        fb = blk(base + H_B + h)
        a = jnp.log1p(-lb) + _log_sigmoid(fb)
        log_lb = jnp.log(lb)
        g = jnp.maximum(a, log_lb) + jnp.log1p(jnp.exp(-jnp.abs(a - log_lb)))
        g, k = masked(g, (1.0 - lb) * jax.nn.sigmoid(-fb))
        heads.append((blk(base + h), k, blk(base + 2 * H_B + h), g))
        gains.append(nb_ref[...])
        gates.append(blk(base + 3 * H_B + h))

    for h, o in enumerate(_gated_chunks(heads, st_ref, tri)):
        o = o * lax.rsqrt(jnp.mean(o * o, axis=-1, keepdims=True) + RMS_EPS) * gains[h]
        o_ref[:, h * LANES:(h + 1) * LANES] = o * (gates[h] * jax.nn.sigmoid(gates[h]))

    @pl.when(ci == pl.num_programs(1) - 1)
    def _():
        sout_ref[0] = st_ref[...]


def even_mixer(z, s0, wgk, bgk, lb, norm_a, norm_b, *, n_seq, t, chunk, t_valid=None):
    ncol = z.shape[1]
    t_valid = chunk if t_valid is None else t_valid
    n_chunks = t // chunk
    st_shape = (N_EVEN_HEADS, LANES, LANES)
    return pl.pallas_call(
        functools.partial(_even_mixer_kernel, t_valid=t_valid),
        grid=(n_seq, n_chunks),
        in_specs=[pl.BlockSpec((chunk, ncol), lambda b, c: (b * n_chunks + c, 0)),
                  pl.BlockSpec((1,) + st_shape, lambda b, c: (b, 0, 0, 0)),
                  _resident(wgk.shape), _resident(bgk.shape), _resident(lb.shape),
                  _resident(norm_a.shape), _resident(norm_b.shape)],
        out_specs=[pl.BlockSpec((chunk, D_MODEL), lambda b, c: (b * n_chunks + c, 0)),
                   pl.BlockSpec((1,) + st_shape, lambda b, c: (b, 0, 0, 0))],
        out_shape=[jax.ShapeDtypeStruct((n_seq * t, D_MODEL), F32),
                   jax.ShapeDtypeStruct((n_seq,) + st_shape, F32)],
        scratch_shapes=[pltpu.VMEM(st_shape, F32)],
        compiler_params=_params("parallel", "arbitrary"),
        name="even_mixer",
    )(z, s0, wgk, bgk, lb, norm_a, norm_b)


def _pad_heads(w, n_heads):
    d = w.shape[-1] // n_heads
    w = w.reshape(w.shape[:-1] + (n_heads, d))
    w = jnp.pad(w, [(0, 0)] * (w.ndim - 1) + [(0, LANES - d)])
    return w.reshape(w.shape[:-2] + (n_heads * LANES,))


def _split_cols(w, sizes):
    out, o = [], 0
    for s in sizes:
        out.append(w[..., o:o + s])
        o += s
    return out


def even_weights(w_in, w_gk, b_gk):
    ka = H_A * DK_A
    qa, kk, va, gk_in, ga, qb, fb, ib, gb = _split_cols(
        w_in, (ka, ka, HALF, GLA_GK_RANK, HALF, HALF, HALF, HALF, HALF))
    gk_in = jnp.pad(gk_in, ((0, 0), (0, LANES - GLA_GK_RANK)))
    w = jnp.concatenate([_pad_heads(qa, H_A), _pad_heads(kk, H_A), va, ga, qb, fb, ib, gb, gk_in],
                        axis=-1).astype(BF16)
    wgk = jnp.pad(_pad_heads(w_gk, H_A), ((0, LANES - GLA_GK_RANK), (0, 0))).astype(BF16)
    bgk = _pad_heads(b_gk[None, :], H_A)
    return w, wgk, bgk


def run_even_mixer(z, n_p, t_p, n_s, t_s, state_gla, state_hgrn, wgk, bgk, lb, norm_a, norm_b,
                   *, chunk):
    ncol = z.shape[1]
    args = (wgk, bgk, lb[None, :], norm_a[None, :], norm_b[None, :])
    zero = jnp.zeros((n_p, N_EVEN_HEADS, LANES, LANES), F32)
    o_p, st_p = even_mixer(z, zero, *args, n_seq=n_p, t=t_p, chunk=chunk)
    zs = z[n_p * t_p:].reshape(n_s, t_s, ncol)
    zs = jnp.pad(zs, ((0, 0), (0, SUBLANES - t_s), (0, 0))).reshape(n_s * SUBLANES, ncol)
    s0 = jnp.concatenate([
        jnp.pad(jnp.swapaxes(state_gla, -1, -2), ((0, 0), (0, 0), (0, 0), (0, LANES - DK_A))),
        jnp.swapaxes(state_hgrn, -1, -2)], axis=1)
    o_s, st_s = even_mixer(zs, s0, *args, n_seq=n_s, t=SUBLANES, chunk=SUBLANES, t_valid=t_s)
    o_s = o_s.reshape(n_s, SUBLANES, D_MODEL)[:, :t_s].reshape(n_s * t_s, D_MODEL)

    def unpack(st):
        gla = jnp.swapaxes(st[:, :H_A], -1, -2)[:, :, :DK_A, :]
        return gla, jnp.swapaxes(st[:, H_A:], -1, -2)

    return jnp.concatenate([o_p, o_s], axis=0), unpack(st_p), unpack(st_s)


_NT_DIMS = (((1,), (1,)), ((), ()))
LOG2E = 1.4426950408889634


def _neg_softplus(z):
    return -(jnp.maximum(z, 0.0) + jnp.log1p(jnp.exp(-jnp.abs(z))))


def _suffix_tri(n):
    return (lax.broadcasted_iota(jnp.int32, (n, n), 0)
            >= lax.broadcasted_iota(jnp.int32, (n, n), 1)).astype(BF16)


def _sb_prompt_kernel(bias_ref, q_ref, k_ref, v_ref, o_ref, u_ref, tot_ref, carry_ref, acc_ref,
                      *, blk, group):
    pair = pl.program_id(1)
    qi = pl.program_id(2)
    lane = lax.broadcasted_iota(jnp.int32, (1, LANES), 1)
    head_a = lane < DH_C
    q2 = q_ref[...].astype(F32) * (DH_C ** -0.5 * LOG2E)
    qs = jnp.concatenate([jnp.where(head_a, q2, 0.0), jnp.where(head_a, 0.0, q2)], axis=0).astype(BF16)
    row = lax.broadcasted_iota(jnp.int32, (2 * blk, 1), 0)
    bias2 = jnp.where(row < blk, bias_ref[2 * pair], bias_ref[2 * pair + 1]) * LOG2E
    neg_tri = -_suffix_tri(blk)
    strict = lax.broadcasted_iota(jnp.int32, (2 * blk, blk), 1) < (row % blk)

    def rows_of(ref, j):
        return ref[pl.ds(pl.multiple_of(j * blk, blk), blk), :]

    def scores(slot, j, diagonal=False):
        b2 = jnp.where(j >= 0, bias2, -1e30)
        z2 = lax.dot_general(qs, rows_of(k_ref, jnp.maximum(j, 0)), _NT_DIMS,
                             preferred_element_type=F32) + b2
        if diagonal:
            z2 = jnp.where(strict, z2, -1e30)
        sp2 = jnp.maximum(z2, 0.0) + jnp.log2(1.0 + jnp.exp2(-jnp.abs(z2)))
        incl = _dot(sp2.astype(BF16), neg_tri)
        u_ref[slot] = z2 + incl
        tot_ref[slot] = incl[:, 0:1]

    def accumulate(slot, j):
        w = jnp.exp2(u_ref[slot] + carry_ref[...])
        acc_ref[...] += _dot(w.astype(BF16), rows_of(v_ref, jnp.maximum(j, 0)))
        carry_ref[...] += tot_ref[slot]

    def body(k, _):
        j = qi - group * k
        for s in range(group):
            accumulate(s, j + group - s)
        for s in range(group):
            scores(s, j - s)
        return 0

    carry_ref[...] = jnp.zeros(carry_ref.shape, F32)
    acc_ref[...] = jnp.zeros(acc_ref.shape, F32)
    for s in range(group):
        scores(s, qi - s, diagonal=(s == 0))
    lax.fori_loop(1, qi // group + 1, body, 0)
    for s in range(group):
        accumulate(s, qi % group - s)
    o_ref[...] = jnp.where(head_a, acc_ref[:blk, :], acc_ref[blk:, :])


def sb_attention_prompt(z, bias, *, n_seq, t, q_col, blk, group):
    nq = t // blk
    n_pairs = H_C // 2
    return pl.pallas_call(
        functools.partial(_sb_prompt_kernel, blk=blk, group=group),
        grid=(n_seq, n_pairs, nq),
        in_specs=[pl.BlockSpec(memory_space=pltpu.SMEM),
                  pl.BlockSpec((blk, LANES), lambda b, p, i: (b * nq + i, q_col + p)),
                  pl.BlockSpec((t, LANES), lambda b, p, i: (b, q_col + n_pairs + p)),
                  pl.BlockSpec((t, LANES), lambda b, p, i: (b, q_col + 2 * n_pairs + p))],
        out_specs=pl.BlockSpec((blk, LANES), lambda b, p, i: (b * nq + i, p)),
        out_shape=jax.ShapeDtypeStruct((n_seq * t, HALF), F32),
        scratch_shapes=[pltpu.VMEM((group, 2 * blk, blk), F32), pltpu.VMEM((group, 2 * blk, 1), F32),
                        pltpu.VMEM((2 * blk, 1), F32), pltpu.VMEM((2 * blk, LANES), F32)],
        compiler_params=_params("parallel", "parallel", "arbitrary"),
        name="sb_attention_prompt",
    )(bias, z, z, z)


def _sb_sample_kernel(pt_ref, q_ref, kn_ref, vn_ref, bias_ref, *refs, n_pages, t_s):
    del pt_ref
    k_pages, v_pages, o_ref = refs[:n_pages], refs[n_pages:2 * n_pages], refs[2 * n_pages]
    rows = t_s * H_C
    q = q_ref[0] * DH_C ** -0.5
    qx = jnp.concatenate([jnp.broadcast_to(q[i:i + 1, :], (H_C, HALF)) for i in range(t_s)], axis=0)
    r_id = lax.broadcasted_iota(jnp.int32, (rows, HALF), 0)
    l_id = lax.broadcasted_iota(jnp.int32, (rows, HALF), 1)
    head_lanes = (l_id // DH_C) == (r_id % H_C)
    qx = jnp.where(head_lanes, qx, 0.0)
    bias = bias_ref[...]
    q_idx = lax.broadcasted_iota(jnp.int32, (rows, 1), 0) // H_C

    carry = jnp.zeros((rows, 1), F32)
    acc = jnp.zeros((rows, HALF), F32)
    kn, vn = kn_ref[0], vn_ref[0]
    for j in reversed(range(t_s)):
        visible = q_idx > j
        z = jnp.sum(qx * kn[j:j + 1, :], axis=-1, keepdims=True) + bias[:, 0:1]
        lk = jnp.where(visible, _neg_softplus(z), 0.0)
        w = jnp.where(visible, jnp.exp(z + lk + carry), 0.0)
        acc = acc + w * vn[j:j + 1, :]
        carry = carry + lk

    qx = qx.astype(BF16)
    tri = _suffix_tri(PAGE_SIZE)
    zs = [_dot(qx, k_pages[p][...].astype(BF16)) + bias for p in range(n_pages)]
    incls = [_dot(_neg_softplus(z).astype(BF16), tri) for z in zs]
    ws = [None] * n_pages
    for p in reversed(range(n_pages)):
        ws[p] = jnp.exp(zs[p] + incls[p] + carry).astype(BF16)
        carry = carry + incls[p][:, 0:1]
    for p in range(n_pages):
        acc = acc + lax.dot_general(ws[p], v_pages[p][...].astype(BF16), _NT_DIMS,
                                    preferred_element_type=F32)

    acc = jnp.where(head_lanes, acc, 0.0)
    for i in range(t_s):
        o_ref[0, i:i + 1, :] = jnp.sum(acc[i * H_C:(i + 1) * H_C, :], axis=0, keepdims=True)


def sb_attention_sample(q, k_new, v_new, cache_k, cache_v, layer, page_table, bias_rows):
    n_seq, t_s, _ = q.shape
    n_pages = page_table.shape[0] // n_seq
    tok = pl.BlockSpec((1, t_s, HALF), lambda b, pt: (b, 0, 0))

    def page_spec(p):
        return pl.BlockSpec((None, None, HALF, PAGE_SIZE),
                            lambda b, pt: (layer, pt[b * n_pages + p], 0, 0))

    pages = [page_spec(p) for p in range(n_pages)]
    return pl.pallas_call(
        functools.partial(_sb_sample_kernel, n_pages=n_pages, t_s=t_s),
        grid_spec=pltpu.PrefetchScalarGridSpec(
            num_scalar_prefetch=1,
            grid=(n_seq,),
            in_specs=[tok, tok, tok,
                      pl.BlockSpec(bias_rows.shape, lambda b, pt: (0, 0))] + pages + pages,
            out_specs=tok),
        out_shape=jax.ShapeDtypeStruct((n_seq, t_s, HALF), F32),
        compiler_params=_params("parallel"),
        name="sb_attention_sample",
    )(page_table, q, k_new, v_new, bias_rows, *([cache_k] * n_pages), *([cache_v] * n_pages))


N_SCAN_ROWS = 13
N_SCAN_STEP_ROWS = 9
RWKV_STEPS_PER_TRIP = 4
N_PAIRS_D = H_D // 2


def _head_sum(x, seg):
    hi, lo = _split_bf16(x)
    return _dot(hi, seg) + _dot(lo, seg)


def _rwkv_prep_kernel(zd_ref, zb_ref, prev_s_ref, mu_ref, w0_ref, w2_ref, a0_ref, a2_ref, g2_ref,
                      kk_ref, ka_ref, rk_ref, seg_ref, rows_ref, v_ref, gb_ref,
                      *, tiles_per_seq, n_prompt_tiles):
    i = pl.program_id(0)
    z = zd_ref[...]
    tm = z.shape[0]
    boundary = jnp.where(i % tiles_per_seq == 0, 0.0, 1.0) * zb_ref[SUBLANES - 1:SUBLANES, :]
    row = lax.broadcasted_iota(jnp.int32, (tm, 1), 0)
    prev = jnp.where(row == 0, boundary, pltpu.roll(z, 1, axis=0))
    prev = jnp.where(i >= n_prompt_tiles, prev_s_ref[...], prev)
    zs = z + (prev - z) * mu_ref[...]
    r, k, v = zs[:, 0:DM_D], zs[:, DM_D:2 * DM_D], zs[:, 2 * DM_D:3 * DM_D]
    u = zs[:, 3 * DM_D:3 * DM_D + R_W + R_A]
    g_in = zs[:, 3 * DM_D + R_W + R_A:]
    seg = seg_ref[...]
    w = w0_ref[...] + _dot(jnp.tanh(u).astype(BF16), w2_ref[...])
    decay = jnp.exp(-jnp.exp(_log_sigmoid(w) - 0.5))
    a = jax.nn.sigmoid(a0_ref[...] + _dot(u.astype(BF16), a2_ref[...]))
    g = _dot(jax.nn.sigmoid(g_in).astype(BF16), g2_ref[...])
    kk = k * kk_ref[...]
    kk = kk * lax.rsqrt(jnp.maximum(_head_sum(kk * kk, seg), 1e-24))
    kp = k * (1.0 + (a - 1.0) * ka_ref[...])
    beta = kk * a
    wr = decay * r
    odd = (row % 2) == 1
    w_a, beta_a, kp_a = (pltpu.roll(t, 1, axis=0) for t in (decay, beta, kp))
    outs = (jnp.where(odd, w_a * kk, kk), jnp.where(odd, w_a * wr, wr),
            w_a * decay, beta_a * decay, kp_a * decay, beta, kp,
            _head_sum(beta_a * kk, seg), _head_sum(kp_a * kk, seg),
            jnp.where(odd, _head_sum(beta_a * wr, seg), 0.0),
            jnp.where(odd, _head_sum(kp_a * wr, seg), 0.0),
            _head_sum(beta * r, seg), _head_sum(kp * r, seg))
    for n, val in enumerate(outs):
        rows_ref[:, n * DM_D:(n + 1) * DM_D] = val
    v_ref[...] = v
    gb_ref[:, 0:DM_D] = g
    gb_ref[:, DM_D:] = _head_sum(r * kp * rk_ref[...], seg) * v


def rwkv_prep(z, prev_s, mu, w0, w2, a0, a2, g2, k_k, k_a, r_k, seg, *, n_prompt_tok, t_prompt):
    nt = z.shape[0]
    tm = TOKEN_TILE
    vec = _resident((1, DM_D))
    tok = lambda n: pl.BlockSpec((tm, n), lambda i: (i, 0))
    return pl.pallas_call(
        functools.partial(_rwkv_prep_kernel, tiles_per_seq=t_prompt // tm,
                          n_prompt_tiles=n_prompt_tok // tm),
        grid=(nt // tm,),
        in_specs=[tok(N_COLS_RWKV),
                  pl.BlockSpec((SUBLANES, N_COLS_RWKV),
                               lambda i: (jnp.maximum(i * (tm // SUBLANES) - 1, 0), 0)),
                  _resident(prev_s.shape), _resident((1, N_COLS_RWKV)),
                  vec, _resident(w2.shape), vec, _resident(a2.shape), _resident(g2.shape),
                  vec, vec, vec, _resident(seg.shape)],
        out_specs=[tok(N_SCAN_ROWS * DM_D), tok(DM_D), tok(2 * DM_D)],
        out_shape=[jax.ShapeDtypeStruct((nt, N_SCAN_ROWS * DM_D), F32),
                   jax.ShapeDtypeStruct((nt, DM_D), F32),
                   jax.ShapeDtypeStruct((nt, 2 * DM_D), F32)],
        compiler_params=_params("parallel"),
        name="rwkv_prep",
    )(z, z, prev_s, mu, w0, w2, a0, a2, g2, k_k, k_a, r_k, seg)


def _rwkv_scan_kernel(*refs, nb, t_c, rows_per_seq):
    n_in = nb if rows_per_seq else 1
    rows_refs, v_refs = refs[:n_in], refs[n_in:2 * n_in]
    s0_ref, o_ref, sout_ref, s_scr, vt_scr, sk_scr, pr_scr = refs[2 * n_in:]
    ci = pl.program_id(1)

    @pl.when(ci == 0)
    def _():
        s_scr[...] = s0_ref[...]

    def seq_rows(ref_list, kq, start, size):
        if rows_per_seq:
            return ref_list[kq][pl.ds(start, size), :]
        return ref_list[0][pl.ds(kq * t_c + start, size), :]

    lane = lax.broadcasted_iota(jnp.int32, (1, LANES), 1)
    head_a = lane < DH_D
    ones_bd = ((lax.broadcasted_iota(jnp.int32, (LANES, LANES), 0) // DH_D)
               == (lax.broadcasted_iota(jnp.int32, (LANES, LANES), 1) // DH_D)).astype(BF16)

    spb = 1 if rows_per_seq else nb
    n_sel = spb * t_c
    sel = ((lax.broadcasted_iota(jnp.int32, (n_sel, spb * LANES), 0) // t_c
            == lax.broadcasted_iota(jnp.int32, (n_sel, spb * LANES), 1) // LANES)
           & (lax.broadcasted_iota(jnp.int32, (n_sel, spb * LANES), 0) % t_c
              == lax.broadcasted_iota(jnp.int32, (n_sel, spb * LANES), 1) % DH_D)).astype(BF16)
    for blk_i in range(n_in):
        for p in range(N_PAIRS_D):
            vp = v_refs[blk_i][:, p * LANES:(p + 1) * LANES].astype(BF16)
            vt = lax.dot_general(vp, sel, (((0,), (0,)), ((), ())), preferred_element_type=F32)
            for q in range(spb):
                blk = vt[:, q * LANES:(q + 1) * LANES]
                vt_scr[blk_i * spb + q, p] = jnp.where(head_a, blk[:DH_D], blk[DH_D:]).astype(BF16)
    sk_scr[...] = jnp.zeros(sk_scr.shape, F32)
    pr_scr[...] = jnp.zeros(pr_scr.shape, F32)

    def body(i, carry):
        ta = 2 * i
        here_a = (lane % DH_D) == ta
        here_b = (lane % DH_D) == ta + 1
        ha16, hb16 = here_a.astype(BF16), here_b.astype(BF16)
        for kq in range(nb):
            ra = seq_rows(rows_refs, kq, ta, 1)
            rb = seq_rows(rows_refs, kq, ta + 1, 1)
            for p in range(N_PAIRS_D):
                def col(r, n):
                    return r[:, n * DM_D + p * LANES:n * DM_D + (p + 1) * LANES]
                w2, b1, k1, beta_b, kp_b, x1, x2 = (col(rb, n) for n in range(2, N_SCAN_STEP_ROWS))
                sp = s_scr[kq, p]
                s16 = sp.astype(BF16)
                vt = vt_scr[kq, p]
                lhs = jnp.concatenate(
                    [s16 * col(ra, 0).astype(BF16), s16 * col(ra, 1).astype(BF16),
                     s16 * col(rb, 0).astype(BF16), s16 * col(rb, 1).astype(BF16),
                     vt * ha16, vt * hb16], axis=0)
                red = _dot(lhs, ones_bd)
                skk_a, pr_a, r3, pr_b, v_a, v_b = (red[n * DH_D:(n + 1) * DH_D] for n in range(6))
                skk_b = r3 - skk_a * x1 + v_a * x2
                s_scr[kq, p] = sp * w2 - skk_a * b1 + v_a * k1 - skk_b * beta_b + v_b * kp_b
                sk_scr[kq, p] = jnp.where(here_a, skk_a, jnp.where(here_b, skk_b, sk_scr[kq, p]))
                pr_scr[kq, p] = jnp.where(here_a, pr_a, jnp.where(here_b, pr_b, pr_scr[kq, p]))
        return carry

    steps_per_trip = min(RWKV_STEPS_PER_TRIP, t_c // 2)

    def body_unrolled(j, carry):
        for u in range(steps_per_trip):
            body(steps_per_trip * j + u, carry)
        return carry

    lax.fori_loop(0, t_c // (2 * steps_per_trip), body_unrolled, 0)

    def token_major(tile):
        x = tile.T
        return jnp.concatenate([x[0:t_c], x[DH_D:DH_D + t_c]], axis=1)

    for kq in range(nb):
        rows = seq_rows(rows_refs, kq, 0, t_c)
        v = seq_rows(v_refs, kq, 0, t_c)
        for p in range(N_PAIRS_D):
            x3, x4, c1, c2 = (rows[:, n * DM_D + p * LANES:n * DM_D + (p + 1) * LANES]
                              for n in range(N_SCAN_STEP_ROWS, N_SCAN_ROWS))
            skt = token_major(sk_scr[kq, p])
            vp = v[:, p * LANES:(p + 1) * LANES]
            o = (token_major(pr_scr[kq, p]) - skt * c1 + vp * c2
                 - jnp.roll(skt, 1, axis=0) * x3 + jnp.roll(vp, 1, axis=0) * x4)
            o_ref[kq, :, p * LANES:(p + 1) * LANES] = o

    @pl.when(ci == pl.num_programs(1) - 1)
    def _():
        sout_ref[...] = s_scr[...]


def rwkv_scan(rows, v, s0, *, tok0, t, nb):
    n_seq = s0.shape[0]
    rows_per_seq = t >= DH_D
    t_c = DH_D if rows_per_seq else t
    n_chunks = t // t_c
    if rows_per_seq:
        def tok_spec(ncol):
            return [pl.BlockSpec((t_c, ncol),
                                 lambda g, c, kq=kq: (tok0 // t_c + (g * nb + kq) * n_chunks + c, 0))
                    for kq in range(nb)]
        n_in = nb
    else:
        def tok_spec(ncol):
            return [pl.BlockSpec((nb * t_c, ncol), lambda g, c: (tok0 // (nb * t_c) + g, 0))]
        n_in = 1
    st_shape = (nb,) + s0.shape[1:]
    st_spec = pl.BlockSpec(st_shape, lambda g, c: (g, 0, 0, 0))
    return pl.pallas_call(
        functools.partial(_rwkv_scan_kernel, nb=nb, t_c=t_c, rows_per_seq=rows_per_seq),
        grid=(n_seq // nb, n_chunks),
        in_specs=tok_spec(rows.shape[1]) + tok_spec(DM_D) + [st_spec],
        out_specs=[pl.BlockSpec((nb, t_c, DM_D), lambda g, c: (g, c, 0)), st_spec],
        out_shape=[jax.ShapeDtypeStruct((n_seq, t, DM_D), F32), jax.ShapeDtypeStruct(s0.shape, F32)],
        scratch_shapes=[pltpu.VMEM(st_shape, F32), pltpu.VMEM(st_shape, BF16),
                        pltpu.VMEM(st_shape, F32), pltpu.VMEM(st_shape, F32)],
        compiler_params=_params("parallel", "arbitrary"),
        name="rwkv_scan",
    )(*([rows] * n_in), *([v] * n_in), s0)


def _odd_out_kernel(x_ref, oc_ref, od_ref, gb_ref, lnw_ref, lnb_ref, seg_ref, w_ref, o_ref):
    seg = seg_ref[...]
    o = od_ref[...]
    d = o - _head_sum(o, seg) * (1.0 / DH_D)
    var = _head_sum(d * d, seg) * (1.0 / DH_D)
    od = d * lax.rsqrt(var + RWKV_LN_EPS) * lnw_ref[...] + lnb_ref[...]
    od = (od + gb_ref[:, DM_D:]) * gb_ref[:, 0:DM_D]
    a = jnp.concatenate([oc_ref[...], od], axis=-1).astype(BF16)
    o_ref[...] = x_ref[...] + _dot(a, w_ref[...])


def odd_out(x, o_c, o_d, gb, ln_w, ln_b, seg, w_out):
    nt = x.shape[0]
    tok = lambda n: pl.BlockSpec((TOKEN_TILE, n), lambda i: (i, 0))
    vec = _resident((1, DM_D))
    return pl.pallas_call(
        _odd_out_kernel,
        grid=(nt // TOKEN_TILE,),
        in_specs=[tok(D_MODEL), tok(HALF), tok(DM_D), tok(2 * DM_D), vec, vec,
                  _resident(seg.shape), _resident(w_out.shape)],
        out_specs=tok(D_MODEL),
        out_shape=jax.ShapeDtypeStruct(x.shape, F32),
        compiler_params=_params("parallel"),
        name="odd_out",
    )(x, o_c, o_d, gb, ln_w, ln_b, seg, w_out)


def _state_to_pairs(s):
    n_seq = s.shape[0]
    s = s.reshape(n_seq, N_PAIRS_D, 2, DH_D, DH_D)
    return jnp.transpose(s, (0, 1, 3, 2, 4)).reshape(n_seq, N_PAIRS_D, DH_D, LANES)


def _state_from_pairs(s):
    n_seq = s.shape[0]
    s = s.reshape(n_seq, N_PAIRS_D, DH_D, 2, DH_D)
    return jnp.transpose(s, (0, 1, 3, 2, 4)).reshape(n_seq, H_D, DH_D, DH_D)


def run_rwkv(z, n_p, t_p, n_s, t_s, state, shift, mu, w0, w2, a0, a2, g2, k_k, k_a, r_k, seg,
             *, nb_sample):
    np_tok = n_p * t_p
    zd_s = z[np_tok:, :N_COLS_RWKV].reshape(n_s, t_s, N_COLS_RWKV)
    prev_s = jnp.concatenate([shift[:, None, :], zd_s[:, :-1]], axis=1).reshape(n_s * t_s, N_COLS_RWKV)
    w2p = jnp.pad(w2, ((0, R_A), (0, 0))).astype(BF16)
    a2p = jnp.pad(a2, ((R_W, 0), (0, 0))).astype(BF16)
    row = lambda a: a.reshape(1, -1)
    rows, v, gb = rwkv_prep(z, prev_s, row(mu), row(w0), w2p, row(a0), a2p, g2.astype(BF16),
                            row(k_k), row(k_a), row(r_k), seg, n_prompt_tok=np_tok, t_prompt=t_p)
    zero = jnp.zeros((n_p, N_PAIRS_D, DH_D, LANES), F32)
    o_p, st_p = rwkv_scan(rows, v, zero, tok0=0, t=t_p, nb=n_p)
    o_s, st_s = rwkv_scan(rows, v, _state_to_pairs(state), tok0=np_tok, t=t_s, nb=nb_sample)
    o = jnp.concatenate([o_p.reshape(np_tok, DM_D), o_s.reshape(n_s * t_s, DM_D)], axis=0)
    return o, gb, _state_from_pairs(st_p), _state_from_pairs(st_s)


EVEN_CHUNK = 64
SB_BLOCK = 256
SB_GROUP = 4
RWKV_SAMPLE_GROUP = 8


def kernel(x_prompt, x_sample, p_prompt, p_sample, cache_k, cache_v, page_table, state_gla, state_hgrn, state_rwkv, state_rwkv_shift, ln_ffn1, ffn1_w_gate, ffn1_w_up, ffn1_w_down, ln_mix, ln_ffn2, ffn2_w_gate, ffn2_w_up, ffn2_w_down, ln_ple, ple_w_gate, ple_w_proj, w_in_even, w_out_even, gla_w_gk, gla_b_gk, gla_norm, hgrn_lb_logits, hgrn_norm, w_in_odd, w_out_odd, sb_bias, rwkv_mu, rwkv_w0, rwkv_w2, rwkv_a0, rwkv_a2, rwkv_g2, rwkv_k_k, rwkv_k_a, rwkv_r_k, rwkv_ln_w, rwkv_ln_b, final_norm):
    n_p, t_p, _ = x_prompt.shape
    n_s, t_s, _ = x_sample.shape
    np_tok, ns_tok = n_p * t_p, n_s * t_s
    assert ns_tok == TOKEN_TILE and t_p % TOKEN_TILE == 0 and t_s < SUBLANES
    bf = lambda w: w.astype(BF16)
    row = lambda a: a.reshape(1, -1)

    x = jnp.concatenate([x_prompt.reshape(np_tok, D_MODEL), x_sample.reshape(ns_tok, D_MODEL)], axis=0)
    p_all = jnp.concatenate([p_prompt.reshape(DEPTH, np_tok, PLE_DIM),
                             p_sample.reshape(DEPTH, ns_tok, PLE_DIM)], axis=1)
    sm = jax.nn.softmax(hgrn_lb_logits.astype(F32), axis=0)
    lower_bounds = jnp.concatenate([jnp.zeros_like(sm[:1]), jnp.cumsum(sm[1:], axis=0)], axis=0)
    head_id = jnp.arange(DM_D) // DH_D
    seg = (head_id[:, None] == head_id[None, :]).astype(BF16)
    n_phys = cache_k.shape[1]
    ck = jnp.transpose(cache_k, (0, 1, 3, 4, 2)).reshape(cache_k.shape[0], n_phys, HALF, PAGE_SIZE)
    cv = jnp.transpose(cache_v, (0, 1, 3, 4, 2)).reshape(cache_v.shape[0], n_phys, HALF, PAGE_SIZE)
    pt_flat = page_table.reshape(-1)
    q0 = N_COLS_RWKV
    k0, v0 = q0 + HALF, q0 + 2 * HALF

    k_rows_p, v_rows_p, k_rows_s, v_rows_s = [], [], [], []
    gla_p, gla_s, hgrn_p, hgrn_s = [], [], [], []
    rwkv_p, rwkv_s, shift_p, shift_s = [], [], [], []
    for i in range(DEPTH):
        j = i // 2
        x = ffn(x, row(ln_ffn1[i]), bf(ffn1_w_gate[i]), bf(ffn1_w_up[i]), bf(ffn1_w_down[i]))
        if i % 2 == 0:
            w_in, wgk, bgk = even_weights(w_in_even[j], gla_w_gk[j], gla_b_gk[j])
            z = rms_matmul(x, row(ln_mix[i]), w_in)
            o, (ga_p, hb_p), (ga_s, hb_s) = run_even_mixer(
                z, n_p, t_p, n_s, t_s, state_gla[j], state_hgrn[j], wgk, bgk,
                lower_bounds[j], gla_norm[j], hgrn_norm[j], chunk=EVEN_CHUNK)
            x = matmul_residual(x, o, bf(w_out_even[j]))
            gla_p.append(ga_p)
            gla_s.append(ga_s)
            hgrn_p.append(hb_p)
            hgrn_s.append(hb_s)
        else:
            w = w_in_odd[j]
            z, qkv = rms_matmul(x, row(ln_mix[i]),
                                bf(jnp.concatenate([w[:, 3 * HALF:], w[:, :3 * HALF]], axis=1)), bf16_from=q0)
            oc_p = sb_attention_prompt(qkv, sb_bias[j], n_seq=n_p, t=t_p, q_col=0, blk=SB_BLOCK,
                                       group=SB_GROUP)
            z_s = z[np_tok:]
            q_s, kc_s, vc_s = (z_s[:, c:c + HALF].reshape(n_s, t_s, HALF) for c in (q0, k0, v0))
            bias_rows = jnp.broadcast_to(jnp.tile(sb_bias[j], t_s)[:, None], (t_s * H_C, PAGE_SIZE))
            oc_s = sb_attention_sample(q_s, kc_s, vc_s, ck, cv, j, pt_flat, bias_rows)
            o_c = jnp.concatenate([oc_p, oc_s.reshape(ns_tok, HALF)], axis=0)
            o_d, gb, sd_p, sd_s = run_rwkv(
                z, n_p, t_p, n_s, t_s, state_rwkv[j], state_rwkv_shift[j], rwkv_mu[j], rwkv_w0[j],
                rwkv_w2[j], rwkv_a0[j], rwkv_a2[j], rwkv_g2[j], rwkv_k_k[j], rwkv_k_a[j],
                rwkv_r_k[j], seg, nb_sample=RWKV_SAMPLE_GROUP)
            x = odd_out(x, o_c, o_d, gb, row(rwkv_ln_w[j]), row(rwkv_ln_b[j]), seg, bf(w_out_odd[j]))
            k_rows_p.append(lax.slice(z, (0, k0), (np_tok, k0 + HALF)).reshape(n_p, t_p, H_C, DH_C))
            v_rows_p.append(lax.slice(z, (0, v0), (np_tok, v0 + HALF)).reshape(n_p, t_p, H_C, DH_C))
            k_rows_s.append(kc_s.reshape(n_s, t_s, H_C, DH_C))
            v_rows_s.append(vc_s.reshape(n_s, t_s, H_C, DH_C))
            rwkv_p.append(sd_p)
            rwkv_s.append(sd_s)
            shift_p.append(lax.slice(z, (t_p - 1, 0), (np_tok, N_COLS_RWKV), (t_p, 1)))
            shift_s.append(z_s.reshape(n_s, t_s, -1)[:, -1, :N_COLS_RWKV])
        x = ffn(x, row(ln_ffn2[i]), bf(ffn2_w_gate[i]), bf(ffn2_w_up[i]), bf(ffn2_w_down[i]))
        x = ple(x, p_all[i], row(ln_ple[i]), bf(ple_w_gate[i]), bf(ple_w_proj[i]))

    y = final_rmsnorm(x, row(final_norm))
    return (y[:np_tok].reshape(n_p, t_p, D_MODEL), y[np_tok:].reshape(n_s, t_s, D_MODEL),
            jnp.stack(k_rows_p), jnp.stack(v_rows_p), jnp.stack(k_rows_s), jnp.stack(v_rows_s),
            jnp.stack(gla_p), jnp.stack(gla_s), jnp.stack(hgrn_p), jnp.stack(hgrn_s),
            jnp.stack(rwkv_p), jnp.stack(rwkv_s), jnp.stack(shift_p), jnp.stack(shift_s))
```

```python
import functools

import jax
import jax.numpy as jnp
from jax import lax
from jax.experimental import pallas as pl
from jax.experimental.pallas import tpu as pltpu

F32 = jnp.float32
BF16 = jnp.bfloat16

D_MODEL = 1024
DEPTH = 4
PAGE_SIZE = 128
HALF = D_MODEL // 2
D_FF = 2816
PLE_DIM = 256
RMS_EPS = 1e-6
H_A = 4
DK_A = HALF // H_A // 2
DV_A = HALF // H_A
GLA_GK_RANK = 16
GLA_GK_NORM = 16.0
H_B = 4
EXP_B = 128
DV_B = HALF // H_B
H_C = 8
DH_C = HALF // H_C
H_D = 8
DH_D = HALF // H_D
DM_D = H_D * DH_D
R_W = 64
R_A = 64
R_G = 128
RWKV_LN_EPS = 64e-5
N_COLS_RWKV = 3 * DM_D + R_W + R_A + R_G
N_COLS_ODD = 3 * HALF + N_COLS_RWKV

LANES = 128
SUBLANES = 8
MXU_DIM = 256
VMEM_LIMIT_BYTES = 56 * 1024 * 1024

TOKEN_TILE = 512
FFN_TOKEN_TILE = 768
N_EVEN_HEADS = H_A + H_B
N_COLS_EVEN_PAD = (4 * N_EVEN_HEADS + 1) * LANES
GATED_SUB_BLOCK = 16


def _params(*semantics):
    return pltpu.CompilerParams(dimension_semantics=semantics,
                                vmem_limit_bytes=VMEM_LIMIT_BYTES)


def _resident(shape):
    return pl.BlockSpec(shape, lambda *_: (0,) * len(shape), pipeline_mode=pl.Buffered(1))


def _rms(x, g):
    return x * lax.rsqrt(jnp.mean(x * x, axis=-1, keepdims=True) + RMS_EPS) * g


def _dot(a, b):
    return jnp.dot(a, b, preferred_element_type=F32)


def _ffn_kernel(x_ref, ln_ref, wg_ref, wu_ref, wd_ref, o_ref):
    x = x_ref[...]
    h = _rms(x, ln_ref[...]).astype(BF16)
    acc = jnp.zeros(x.shape, F32)
    for f0 in range(0, D_FF, MXU_DIM):
        g = _dot(h, wg_ref[:, f0:f0 + MXU_DIM])
        u = _dot(h, wu_ref[:, f0:f0 + MXU_DIM])
        a = (g * jax.nn.sigmoid(g) * u).astype(BF16)
        acc = acc + _dot(a, wd_ref[f0:f0 + MXU_DIM, :])
    o_ref[...] = x + 0.5 * acc


def ffn(x, ln, wg, wu, wd):
    nt = x.shape[0]
    tile = FFN_TOKEN_TILE if nt % FFN_TOKEN_TILE == 0 else TOKEN_TILE
    tok = pl.BlockSpec((tile, D_MODEL), lambda i: (i, 0))
    return pl.pallas_call(
        _ffn_kernel,
        grid=(nt // tile,),
        in_specs=[tok, _resident((1, D_MODEL)), _resident((D_MODEL, D_FF)),
                  _resident((D_MODEL, D_FF)), _resident((D_FF, D_MODEL))],
        out_specs=tok,
        out_shape=jax.ShapeDtypeStruct(x.shape, F32),
        compiler_params=_params("parallel"),
        name="ffn",
    )(x, ln, wg, wu, wd)


def _rms_matmul_kernel(x_ref, ln_ref, w_ref, o_ref, *maybe_ob_ref, n_chunk, bf16_from):
    h = _rms(x_ref[...], ln_ref[...]).astype(BF16)
    n = w_ref.shape[1]
    for n0 in range(0, n, n_chunk):
        n1 = min(n0 + n_chunk, n)
        o_ref[:, n0:n1] = _dot(h, w_ref[:, n0:n1])
    if maybe_ob_ref:
        maybe_ob_ref[0][...] = o_ref[:, bf16_from:].astype(BF16)


def rms_matmul(x, ln, w, bf16_from=None):
    nt, n = x.shape[0], w.shape[1]
    out_specs = [pl.BlockSpec((TOKEN_TILE, n), lambda i: (i, 0))]
    out_shape = [jax.ShapeDtypeStruct((nt, n), F32)]
    if bf16_from is not None:
        out_specs.append(pl.BlockSpec((TOKEN_TILE, n - bf16_from), lambda i: (i, 0)))
        out_shape.append(jax.ShapeDtypeStruct((nt, n - bf16_from), BF16))
    out = pl.pallas_call(
        functools.partial(_rms_matmul_kernel, n_chunk=2 * MXU_DIM, bf16_from=bf16_from),
        grid=(nt // TOKEN_TILE,),
        in_specs=[pl.BlockSpec((TOKEN_TILE, D_MODEL), lambda i: (i, 0)),
                  _resident((1, D_MODEL)), _resident((D_MODEL, n))],
        out_specs=out_specs,
        out_shape=out_shape,
        compiler_params=_params("parallel"),
        name="rms_matmul",
    )(x, ln, w)
    return out[0] if bf16_from is None else out


def _matmul_residual_kernel(x_ref, a_ref, w_ref, o_ref):
    o_ref[...] = x_ref[...] + _dot(a_ref[...].astype(BF16), w_ref[...])


def matmul_residual(x, a, w):
    nt, k = a.shape
    return pl.pallas_call(
        _matmul_residual_kernel,
        grid=(nt // TOKEN_TILE,),
        in_specs=[pl.BlockSpec((TOKEN_TILE, D_MODEL), lambda i: (i, 0)),
                  pl.BlockSpec((TOKEN_TILE, k), lambda i: (i, 0)),
                  _resident((k, D_MODEL))],
        out_specs=pl.BlockSpec((TOKEN_TILE, D_MODEL), lambda i: (i, 0)),
        out_shape=jax.ShapeDtypeStruct(x.shape, F32),
        compiler_params=_params("parallel"),
        name="matmul_residual",
    )(x, a, w)


def _ple_kernel(x_ref, p_ref, ln_ref, wg_ref, wp_ref, o_ref):
    x = x_ref[...]
    h = _rms(x, ln_ref[...]).astype(BF16)
    gate = jax.nn.sigmoid(_dot(h, wg_ref[...]))
    o_ref[...] = x + gate * _dot(p_ref[...].astype(BF16), wp_ref[...])


def ple(x, p, ln, wg, wp):
    nt = x.shape[0]
    tok = pl.BlockSpec((TOKEN_TILE, D_MODEL), lambda i: (i, 0))
    return pl.pallas_call(
        _ple_kernel,
        grid=(nt // TOKEN_TILE,),
        in_specs=[tok, pl.BlockSpec((TOKEN_TILE, PLE_DIM), lambda i: (i, 0)),
                  _resident((1, D_MODEL)), _resident((D_MODEL, D_MODEL)),
                  _resident((PLE_DIM, D_MODEL))],
        out_specs=tok,
        out_shape=jax.ShapeDtypeStruct(x.shape, F32),
        compiler_params=_params("parallel"),
        name="ple",
    )(x, p, ln, wg, wp)


def _final_norm_kernel(x_ref, ln_ref, o_ref):
    o_ref[...] = _rms(x_ref[...], ln_ref[...])


def final_rmsnorm(x, ln):
    nt = x.shape[0]
    tok = pl.BlockSpec((TOKEN_TILE, D_MODEL), lambda i: (i, 0))
    return pl.pallas_call(
        _final_norm_kernel,
        grid=(nt // TOKEN_TILE,),
        in_specs=[tok, _resident((1, D_MODEL))],
        out_specs=tok,
        out_shape=jax.ShapeDtypeStruct(x.shape, F32),
        compiler_params=_params("parallel"),
        name="final_norm",
    )(x, ln)


def _log_sigmoid(x):
    return jnp.minimum(x, 0.0) - jnp.log1p(jnp.exp(-jnp.abs(x)))


def _split_bf16(x):
    hi = x.astype(BF16)
    lo = (x - hi.astype(F32)).astype(BF16)
    return hi, lo


def _gated_chunks(heads, st_ref, tri):
    c = heads[0][0].shape[0]
    n = len(heads)
    sub = min(c, GATED_SUB_BLOCK)
    row = lax.broadcasted_iota(jnp.int32, (sub, 1), 0)
    tn_dims = (((0,), (0,)), ((), ()))

    bs = []
    for q, k, v, g in heads:
        g_hi, g_lo = _split_bf16(g * LOG2E)
        bs.append(_dot(tri, g_hi) + _dot(tri, g_lo))
    sts = [st_ref[h] for h in range(n)]
    o_inter = [lax.dot_general((heads[h][0] * jnp.exp2(bs[h])).astype(BF16), sts[h].astype(BF16),
                               _NT_DIMS, preferred_element_type=F32) for h in range(n)]

    scores = [[None] * (c // sub) for _ in range(n)]
    for h, (q, k, v, g) in enumerate(heads):
        b = bs[h]
        for i, r0 in enumerate(range(sub, c, sub), start=1):
            b_ref = b[r0:r0 + 1, :]
            q_rel = (q[r0:r0 + sub, :] * jnp.exp2(b[r0:r0 + sub, :] - b_ref)).astype(BF16)
            k_rel = (k[:r0, :] * jnp.exp2(b_ref - b[:r0, :])).astype(BF16)
            scores[h][i] = lax.dot_general(q_rel, k_rel, _NT_DIMS, preferred_element_type=F32)
    off_diag = [[None] * (c // sub) for _ in range(n)]
    for h, (q, k, v, g) in enumerate(heads):
        for i, r0 in enumerate(range(sub, c, sub), start=1):
            off_diag[h][i] = _dot(scores[h][i].astype(BF16), v[:r0, :].astype(BF16))

    for h, (q, k, v, g) in enumerate(heads):
        b_last = bs[h][c - 1:c, :]
        ke = (k * jnp.exp2(b_last - bs[h])).astype(BF16)
        st_ref[h] = sts[h] * jnp.exp2(b_last) + lax.dot_general(
            v.astype(BF16), ke, tn_dims, preferred_element_type=F32)

    outs = []
    for h, (q, k, v, g) in enumerate(heads):
        parts = []
        for i, r0 in enumerate(range(0, c, sub)):
            ki, vi, bi = (a[r0:r0 + sub, :] for a in (k, v, bs[h]))
            oi = o_inter[h][r0:r0 + sub, :]
            if i > 0:
                oi = oi + off_diag[h][i]
            for p0 in range(0, sub, SUBLANES):
                qp = q[r0 + p0:r0 + p0 + SUBLANES, :]
                bp = bi[p0:p0 + SUBLANES, :]
                op = oi[p0:p0 + SUBLANES, :]
                for s in range(min(sub, p0 + SUBLANES)):
                    d = bp - bi[s:s + 1, :]
                    if s > p0:
                        d = jnp.where(row[p0:p0 + SUBLANES] >= s, d, -1e30)
                    a_s = jnp.sum(qp * ki[s:s + 1, :] * jnp.exp2(d), axis=-1, keepdims=True)
                    op = op + a_s * vi[s:s + 1, :]
                parts.append(op)
        outs.append(jnp.concatenate(parts, axis=0) if len(parts) > 1 else parts[0])
    return outs


def _even_heads(z_ref, wgk_ref, bgk_ref, lb_ref, na_ref, nb_ref):
    def blk(i):
        return z_ref[:, i * LANES:(i + 1) * LANES]

    heads, gains, gates = [], [], []
    gk = _log_sigmoid(_dot(blk(4 * N_EVEN_HEADS).astype(BF16), wgk_ref[...]) + bgk_ref[...])
    gk = gk * (1.0 / GLA_GK_NORM)
    for h in range(H_A):
        heads.append((blk(h) * DK_A ** -0.5, blk(H_A + h), blk(2 * H_A + h),
                      gk[:, h * LANES:(h + 1) * LANES]))
        gains.append(na_ref[...])
        gates.append(blk(3 * H_A + h))
    base = 4 * H_A
    for h in range(H_B):
        lb = lb_ref[:, h * LANES:(h + 1) * LANES]
        fb = blk(base + H_B + h)
        a = jnp.log1p(-lb) + _log_sigmoid(fb)
        log_lb = jnp.log(lb)
        g = jnp.maximum(a, log_lb) + jnp.log1p(jnp.exp(-jnp.abs(a - log_lb)))
        heads.append((blk(base + h), (1.0 - lb) * jax.nn.sigmoid(-fb), blk(base + 2 * H_B + h), g))
        gains.append(nb_ref[...])
        gates.append(blk(base + 3 * H_B + h))
    return heads, gains, gates


def _even_finish(o, gain, gate):
    o = o * lax.rsqrt(jnp.mean(o * o, axis=-1, keepdims=True) + RMS_EPS) * gain
    return o * (gate * jax.nn.sigmoid(gate))


def _lower_tri(c):
    return (lax.broadcasted_iota(jnp.int32, (c, c), 0)
            >= lax.broadcasted_iota(jnp.int32, (c, c), 1)).astype(BF16)


def _even_mixer_kernel(z_ref, s0_ref, wgk_ref, bgk_ref, lb_ref, na_ref, nb_ref,
                       o_ref, sout_ref, st_ref):
    ci = pl.program_id(1)

    @pl.when(ci == 0)
    def _():
        st_ref[...] = s0_ref[0]

    heads, gains, gates = _even_heads(z_ref, wgk_ref, bgk_ref, lb_ref, na_ref, nb_ref)
    for h, o in enumerate(_gated_chunks(heads, st_ref, _lower_tri(z_ref.shape[0]))):
        o_ref[:, h * LANES:(h + 1) * LANES] = _even_finish(o, gains[h], gates[h])

    @pl.when(ci == pl.num_programs(1) - 1)
    def _():
        sout_ref[0] = st_ref[...]


def _even_mixer_short_kernel(z_ref, sg_ref, sh_ref, wgk_ref, bgk_ref, lb_ref, na_ref, nb_ref,
                             o_prev_ref, sg_prev_ref, sh_prev_ref, o_ref, sg_out_ref, sh_out_ref,
                             st_ref, *, t_s):
    del o_prev_ref, sg_prev_ref, sh_prev_ref
    c = z_ref.shape[0]
    spb = c // t_s
    row = lax.broadcasted_iota(jnp.int32, (c, 1), 0)
    zeros = jnp.zeros((LANES, LANES - DK_A), F32)
    for q in range(spb):
        for h in range(H_A):
            st_ref[q * N_EVEN_HEADS + h] = jnp.concatenate([sg_ref[q, h].T, zeros], axis=1)
        for h in range(H_B):
            st_ref[q * N_EVEN_HEADS + H_A + h] = sh_ref[q, h].T

    heads, gains, gates = _even_heads(z_ref, wgk_ref, bgk_ref, lb_ref, na_ref, nb_ref)
    all_heads = []
    for q in range(spb):
        mine = (row // t_s) == q
        all_heads += [(hq, jnp.where(mine, hk, 0.0), hv, jnp.where(mine, hg, 0.0))
                      for hq, hk, hv, hg in heads]
    outs = _gated_chunks(all_heads, st_ref, _lower_tri(c))
    for h in range(N_EVEN_HEADS):
        o = outs[h]
        for q in range(1, spb):
            o = jnp.where((row // t_s) == q, outs[q * N_EVEN_HEADS + h], o)
        o_ref[:, h * LANES:(h + 1) * LANES] = _even_finish(o, gains[h], gates[h])

    for q in range(spb):
        for h in range(H_A):
            sg_out_ref[q, h] = st_ref[q * N_EVEN_HEADS + h].T[:DK_A, :]
        for h in range(H_B):
            sh_out_ref[q, h] = st_ref[q * N_EVEN_HEADS + H_A + h].T


def even_mixer(z, s0, wgk, bgk, lb, norm_a, norm_b, *, n_seq, t, chunk):
    nt, ncol = z.shape
    n_chunks = t // chunk
    st_shape = (N_EVEN_HEADS, LANES, LANES)
    return pl.pallas_call(
        _even_mixer_kernel,
        grid=(n_seq, n_chunks),
        in_specs=[pl.BlockSpec((chunk, ncol), lambda b, c: (b * n_chunks + c, 0)),
                  pl.BlockSpec((1,) + st_shape, lambda b, c: (b, 0, 0, 0)),
                  _resident(wgk.shape), _resident(bgk.shape), _resident(lb.shape),
                  _resident(norm_a.shape), _resident(norm_b.shape)],
        out_specs=[pl.BlockSpec((chunk, D_MODEL), lambda b, c: (b * n_chunks + c, 0)),
                   pl.BlockSpec((1,) + st_shape, lambda b, c: (b, 0, 0, 0))],
        out_shape=[jax.ShapeDtypeStruct((nt, D_MODEL), F32),
                   jax.ShapeDtypeStruct((n_seq,) + st_shape, F32)],
        scratch_shapes=[pltpu.VMEM(st_shape, F32)],
        compiler_params=_params("parallel", "arbitrary"),
        name="even_mixer",
    )(z, s0, wgk, bgk, lb, norm_a, norm_b)


def even_mixer_short(z, state_gla, state_hgrn, layer, wgk, bgk, lb, norm_a, norm_b,
                     o_buf, gla_buf, hgrn_buf, *, tok0, n_seq, t_s):
    ncol = z.shape[1]
    spb = SUBLANES // t_s

    def st_spec(a):
        return pl.BlockSpec((None, spb) + a.shape[2:], lambda g: (layer, g, 0, 0, 0))

    any_spec = pl.BlockSpec(memory_space=pl.ANY)
    row_blk = lambda n: pl.BlockSpec((SUBLANES, n), lambda g: (tok0 // SUBLANES + g, 0))
    return pl.pallas_call(
        functools.partial(_even_mixer_short_kernel, t_s=t_s),
        grid=(n_seq // spb,),
        in_specs=[row_blk(ncol), st_spec(state_gla), st_spec(state_hgrn),
                  _resident(wgk.shape), _resident(bgk.shape), _resident(lb.shape),
                  _resident(norm_a.shape), _resident(norm_b.shape), any_spec, any_spec, any_spec],
        out_specs=[row_blk(D_MODEL), st_spec(gla_buf), st_spec(hgrn_buf)],
        out_shape=[jax.ShapeDtypeStruct(o_buf.shape, F32), jax.ShapeDtypeStruct(gla_buf.shape, F32),
                   jax.ShapeDtypeStruct(hgrn_buf.shape, F32)],
        input_output_aliases={8: 0, 9: 1, 10: 2},
        scratch_shapes=[pltpu.VMEM((spb * N_EVEN_HEADS, LANES, LANES), F32)],
        compiler_params=_params("parallel"),
        name="even_mixer_short",
    )(z, state_gla, state_hgrn, wgk, bgk, lb, norm_a, norm_b, o_buf, gla_buf, hgrn_buf)


def _pad_heads(w, n_heads):
    d = w.shape[-1] // n_heads
    w = w.reshape(w.shape[:-1] + (n_heads, d))
    w = jnp.pad(w, [(0, 0)] * (w.ndim - 1) + [(0, LANES - d)])
    return w.reshape(w.shape[:-2] + (n_heads * LANES,))


def _split_cols(w, sizes):
    out, o = [], 0
    for s in sizes:
        out.append(w[..., o:o + s])
        o += s
    return out


def even_weights(w_in, w_gk, b_gk):
    ka = H_A * DK_A
    qa, kk, va, gk_in, ga, qb, fb, ib, gb = _split_cols(
        w_in, (ka, ka, HALF, GLA_GK_RANK, HALF, HALF, HALF, HALF, HALF))
    gk_in = jnp.pad(gk_in, ((0, 0), (0, LANES - GLA_GK_RANK)))
    w = jnp.concatenate([_pad_heads(qa, H_A), _pad_heads(kk, H_A), va, ga, qb, fb, ib, gb, gk_in],
                        axis=-1).astype(BF16)
    wgk = jnp.pad(_pad_heads(w_gk, H_A), ((0, LANES - GLA_GK_RANK), (0, 0))).astype(BF16)
    bgk = _pad_heads(b_gk[None, :], H_A)
    return w, wgk, bgk


def run_even_mixer(z, n_p, t_p, n_s, t_s, state_gla, state_hgrn, layer, gla_buf, hgrn_buf,
                   wgk, bgk, lb, norm_a, norm_b, *, chunk):
    args = (wgk, bgk, lb[None, :], norm_a[None, :], norm_b[None, :])
    zero = jnp.zeros((n_p, N_EVEN_HEADS, LANES, LANES), F32)
    o_buf, st_p = even_mixer(z, zero, *args, n_seq=n_p, t=t_p, chunk=chunk)
    o, gla_buf, hgrn_buf = even_mixer_short(z, state_gla, state_hgrn, layer, *args,
                                            o_buf, gla_buf, hgrn_buf,
                                            tok0=n_p * t_p, n_seq=n_s, t_s=t_s)
    gla_p = jnp.swapaxes(st_p[:, :H_A], -1, -2)[:, :, :DK_A, :]
    hgrn_p = jnp.swapaxes(st_p[:, H_A:], -1, -2)
    return o, (gla_p, hgrn_p), gla_buf, hgrn_buf


_NT_DIMS = (((1,), (1,)), ((), ()))
LOG2E = 1.4426950408889634


def _neg_softplus(z):
    return -(jnp.maximum(z, 0.0) + jnp.log1p(jnp.exp(-jnp.abs(z))))


def _suffix_tri(n):
    return (lax.broadcasted_iota(jnp.int32, (n, n), 0)
            >= lax.broadcasted_iota(jnp.int32, (n, n), 1)).astype(BF16)


def _sb_prompt_kernel(bias_ref, q_ref, k_ref, v_ref, o_ref, u_ref, tot_ref, carry_ref, acc_ref,
                      *, blk, group):
    pair = pl.program_id(1)
    qi = pl.program_id(2)
    lane = lax.broadcasted_iota(jnp.int32, (1, LANES), 1)
    head_a = lane < DH_C
    q2 = q_ref[...].astype(F32) * (DH_C ** -0.5 * LOG2E)
    qs = jnp.concatenate([jnp.where(head_a, q2, 0.0), jnp.where(head_a, 0.0, q2)], axis=0).astype(BF16)
    row = lax.broadcasted_iota(jnp.int32, (2 * blk, 1), 0)
    bias2 = jnp.where(row < blk, bias_ref[2 * pair], bias_ref[2 * pair + 1]) * LOG2E
    neg_tri = -_suffix_tri(blk)
    strict = lax.broadcasted_iota(jnp.int32, (2 * blk, blk), 1) < (row % blk)

    def rows_of(ref, j):
        return ref[pl.ds(pl.multiple_of(j * blk, blk), blk), :]

    def scores(slot, j, diagonal=False):
        b2 = jnp.where(j >= 0, bias2, -1e30)
        z2 = lax.dot_general(qs, rows_of(k_ref, jnp.maximum(j, 0)), _NT_DIMS,
                             preferred_element_type=F32) + b2
        if diagonal:
            z2 = jnp.where(strict, z2, -1e30)
        sp2 = jnp.maximum(z2, 0.0) + jnp.log2(1.0 + jnp.exp2(-jnp.abs(z2)))
        incl = _dot(sp2.astype(BF16), neg_tri)
        u_ref[slot] = z2 + incl
        tot_ref[slot] = incl[:, 0:1]

    def accumulate(slot, j):
        w = jnp.exp2(u_ref[slot] + carry_ref[...])
        acc_ref[...] += _dot(w.astype(BF16), rows_of(v_ref, jnp.maximum(j, 0)))
        carry_ref[...] += tot_ref[slot]

    def body(k, _):
        j = qi - group * k
        for s in range(group):
            accumulate(s, j + group - s)
        for s in range(group):
            scores(s, j - s)
        return 0

    carry_ref[...] = jnp.zeros(carry_ref.shape, F32)
    acc_ref[...] = jnp.zeros(acc_ref.shape, F32)
    for s in range(group):
        scores(s, qi - s, diagonal=(s == 0))
    lax.fori_loop(1, qi // group + 1, body, 0)
    for s in range(group):
        accumulate(s, qi % group - s)
    o_ref[...] = jnp.where(head_a, acc_ref[:blk, :], acc_ref[blk:, :])


def sb_attention_prompt(z, bias, *, n_seq, t, q_col, blk, group):
    nq = t // blk
    n_pairs = H_C // 2
    return pl.pallas_call(
        functools.partial(_sb_prompt_kernel, blk=blk, group=group),
        grid=(n_seq, n_pairs, nq),
        in_specs=[pl.BlockSpec(memory_space=pltpu.SMEM),
                  pl.BlockSpec((blk, LANES), lambda b, p, i: (b * nq + i, q_col + p)),
                  pl.BlockSpec((t, LANES), lambda b, p, i: (b, q_col + n_pairs + p)),
                  pl.BlockSpec((t, LANES), lambda b, p, i: (b, q_col + 2 * n_pairs + p))],
        out_specs=pl.BlockSpec((blk, LANES), lambda b, p, i: (b * nq + i, p)),
        out_shape=jax.ShapeDtypeStruct((n_seq * t, HALF), F32),
        scratch_shapes=[pltpu.VMEM((group, 2 * blk, blk), F32), pltpu.VMEM((group, 2 * blk, 1), F32),
                        pltpu.VMEM((2 * blk, 1), F32), pltpu.VMEM((2 * blk, LANES), F32)],
        compiler_params=_params("parallel", "parallel", "arbitrary"),
        name="sb_attention_prompt",
    )(bias, z, z, z)


def _sb_sample_kernel(pt_ref, q_ref, kn_ref, vn_ref, bias_ref, *refs, n_pages, t_s):
    del pt_ref
    k_pages, v_pages, o_ref = refs[:n_pages], refs[n_pages:2 * n_pages], refs[2 * n_pages]
    rows = t_s * H_C
    q = q_ref[0] * DH_C ** -0.5
    qx = jnp.concatenate([jnp.broadcast_to(q[i:i + 1, :], (H_C, HALF)) for i in range(t_s)], axis=0)
    r_id = lax.broadcasted_iota(jnp.int32, (rows, HALF), 0)
    l_id = lax.broadcasted_iota(jnp.int32, (rows, HALF), 1)
    head_lanes = (l_id // DH_C) == (r_id % H_C)
    qx = jnp.where(head_lanes, qx, 0.0)
    bias = bias_ref[...]
    q_idx = lax.broadcasted_iota(jnp.int32, (rows, 1), 0) // H_C

    carry = jnp.zeros((rows, 1), F32)
    acc = jnp.zeros((rows, HALF), F32)
    kn, vn = kn_ref[0], vn_ref[0]
    for j in reversed(range(t_s)):
        visible = q_idx > j
        z = jnp.sum(qx * kn[j:j + 1, :], axis=-1, keepdims=True) + bias[:, 0:1]
        lk = jnp.where(visible, _neg_softplus(z), 0.0)
        w = jnp.where(visible, jnp.exp(z + lk + carry), 0.0)
        acc = acc + w * vn[j:j + 1, :]
        carry = carry + lk

    qx = qx.astype(BF16)
    tri = _suffix_tri(PAGE_SIZE)
    zs = [_dot(qx, k_pages[p][...].astype(BF16)) + bias for p in range(n_pages)]
    incls = [_dot(_neg_softplus(z).astype(BF16), tri) for z in zs]
    ws = [None] * n_pages
    for p in reversed(range(n_pages)):
        ws[p] = jnp.exp(zs[p] + incls[p] + carry).astype(BF16)
        carry = carry + incls[p][:, 0:1]
    for p in range(n_pages):
        acc = acc + lax.dot_general(ws[p], v_pages[p][...].astype(BF16), _NT_DIMS,
                                    preferred_element_type=F32)

    acc = jnp.where(head_lanes, acc, 0.0)
    for i in range(t_s):
        o_ref[0, i:i + 1, :] = jnp.sum(acc[i * H_C:(i + 1) * H_C, :], axis=0, keepdims=True)


def sb_attention_sample(q, k_new, v_new, cache_k, cache_v, layer, page_table, bias_rows):
    n_seq, t_s, _ = q.shape
    n_pages = page_table.shape[0] // n_seq
    tok = pl.BlockSpec((1, t_s, HALF), lambda b, pt: (b, 0, 0))

    def page_spec(p):
        return pl.BlockSpec((None, None, HALF, PAGE_SIZE),
                            lambda b, pt: (layer, pt[b * n_pages + p], 0, 0))

    pages = [page_spec(p) for p in range(n_pages)]
    return pl.pallas_call(
        functools.partial(_sb_sample_kernel, n_pages=n_pages, t_s=t_s),
        grid_spec=pltpu.PrefetchScalarGridSpec(
            num_scalar_prefetch=1,
            grid=(n_seq,),
            in_specs=[tok, tok, tok,
                      pl.BlockSpec(bias_rows.shape, lambda b, pt: (0, 0))] + pages + pages,
            out_specs=tok),
        out_shape=jax.ShapeDtypeStruct((n_seq, t_s, HALF), F32),
        compiler_params=_params("parallel"),
        name="sb_attention_sample",
    )(page_table, q, k_new, v_new, bias_rows, *([cache_k] * n_pages), *([cache_v] * n_pages))


N_SCAN_ROWS = 13
N_SCAN_STEP_ROWS = 9
RWKV_STEPS_PER_TRIP = 4
N_PAIRS_D = H_D // 2


def _head_sum(x, seg):
    hi, lo = _split_bf16(x)
    return _dot(hi, seg) + _dot(lo, seg)


def _rwkv_prep_kernel(zd_ref, zb_ref, prev_s_ref, mu_ref, w0_ref, w2_ref, a0_ref, a2_ref, g2_ref,
                      kk_ref, ka_ref, rk_ref, seg_ref, rows_ref, v_ref, gb_ref,
                      *, tiles_per_seq, n_prompt_tiles):
    i = pl.program_id(0)
    z = zd_ref[...]
    tm = z.shape[0]
    boundary = jnp.where(i % tiles_per_seq == 0, 0.0, 1.0) * zb_ref[SUBLANES - 1:SUBLANES, :]
    row = lax.broadcasted_iota(jnp.int32, (tm, 1), 0)
    prev = jnp.where(row == 0, boundary, pltpu.roll(z, 1, axis=0))
    prev = jnp.where(i >= n_prompt_tiles, prev_s_ref[...], prev)
    zs = z + (prev - z) * mu_ref[...]
    r, k, v = zs[:, 0:DM_D], zs[:, DM_D:2 * DM_D], zs[:, 2 * DM_D:3 * DM_D]
    u = zs[:, 3 * DM_D:3 * DM_D + R_W + R_A]
    g_in = zs[:, 3 * DM_D + R_W + R_A:]
    seg = seg_ref[...]
    w = w0_ref[...] + _dot(jnp.tanh(u).astype(BF16), w2_ref[...])
    decay = jnp.exp(-jnp.exp(_log_sigmoid(w) - 0.5))
    a = jax.nn.sigmoid(a0_ref[...] + _dot(u.astype(BF16), a2_ref[...]))
    g = _dot(jax.nn.sigmoid(g_in).astype(BF16), g2_ref[...])
    kk = k * kk_ref[...]
    kk = kk * lax.rsqrt(jnp.maximum(_head_sum(kk * kk, seg), 1e-24))
    kp = k * (1.0 + (a - 1.0) * ka_ref[...])
    beta = kk * a
    wr = decay * r
    odd = (row % 2) == 1
    w_a, beta_a, kp_a = (pltpu.roll(t, 1, axis=0) for t in (decay, beta, kp))
    outs = (jnp.where(odd, w_a * kk, kk), jnp.where(odd, w_a * wr, wr),
            w_a * decay, beta_a * decay, kp_a * decay, beta, kp,
            _head_sum(beta_a * kk, seg), _head_sum(kp_a * kk, seg),
            jnp.where(odd, _head_sum(beta_a * wr, seg), 0.0),
            jnp.where(odd, _head_sum(kp_a * wr, seg), 0.0),
            _head_sum(beta * r, seg), _head_sum(kp * r, seg))
    for n, val in enumerate(outs):
        rows_ref[:, n * DM_D:(n + 1) * DM_D] = val
    v_ref[...] = v
    gb_ref[:, 0:DM_D] = g
    gb_ref[:, DM_D:] = _head_sum(r * kp * rk_ref[...], seg) * v


def rwkv_prep(z, prev_s, mu, w0, w2, a0, a2, g2, k_k, k_a, r_k, seg, *, n_prompt_tok, t_prompt):
    nt = z.shape[0]
    tm = TOKEN_TILE
    vec = _resident((1, DM_D))
    tok = lambda n: pl.BlockSpec((tm, n), lambda i: (i, 0))
    return pl.pallas_call(
        functools.partial(_rwkv_prep_kernel, tiles_per_seq=t_prompt // tm,
                          n_prompt_tiles=n_prompt_tok // tm),
        grid=(nt // tm,),
        in_specs=[tok(N_COLS_RWKV),
                  pl.BlockSpec((SUBLANES, N_COLS_RWKV),
                               lambda i: (jnp.maximum(i * (tm // SUBLANES) - 1, 0), 0)),
                  _resident(prev_s.shape), _resident((1, N_COLS_RWKV)),
                  vec, _resident(w2.shape), vec, _resident(a2.shape), _resident(g2.shape),
                  vec, vec, vec, _resident(seg.shape)],
        out_specs=[tok(N_SCAN_ROWS * DM_D), tok(DM_D), tok(2 * DM_D)],
        out_shape=[jax.ShapeDtypeStruct((nt, N_SCAN_ROWS * DM_D), F32),
                   jax.ShapeDtypeStruct((nt, DM_D), F32),
                   jax.ShapeDtypeStruct((nt, 2 * DM_D), F32)],
        compiler_params=_params("parallel"),
        name="rwkv_prep",
    )(z, z, prev_s, mu, w0, w2, a0, a2, g2, k_k, k_a, r_k, seg)


def _rwkv_scan_kernel(*refs, nb, t_c, rows_per_seq):
    n_in = nb if rows_per_seq else 1
    rows_refs, v_refs = refs[:n_in], refs[n_in:2 * n_in]
    s0_ref, o_ref, sout_ref, s_scr, vt_scr, sk_scr, pr_scr = refs[2 * n_in:]
    ci = pl.program_id(1)

    @pl.when(ci == 0)
    def _():
        s_scr[...] = s0_ref[...]

    def seq_rows(ref_list, kq, start, size):
        if rows_per_seq:
            return ref_list[kq][pl.ds(start, size), :]
        return ref_list[0][pl.ds(kq * t_c + start, size), :]

    lane = lax.broadcasted_iota(jnp.int32, (1, LANES), 1)
    head_a = lane < DH_D
    ones_bd = ((lax.broadcasted_iota(jnp.int32, (LANES, LANES), 0) // DH_D)
               == (lax.broadcasted_iota(jnp.int32, (LANES, LANES), 1) // DH_D)).astype(BF16)

    spb = 1 if rows_per_seq else nb
    n_sel = spb * t_c
    sel = ((lax.broadcasted_iota(jnp.int32, (n_sel, spb * LANES), 0) // t_c
            == lax.broadcasted_iota(jnp.int32, (n_sel, spb * LANES), 1) // LANES)
           & (lax.broadcasted_iota(jnp.int32, (n_sel, spb * LANES), 0) % t_c
              == lax.broadcasted_iota(jnp.int32, (n_sel, spb * LANES), 1) % DH_D)).astype(BF16)
    for blk_i in range(n_in):
        for p in range(N_PAIRS_D):
            vp = v_refs[blk_i][:, p * LANES:(p + 1) * LANES].astype(BF16)
            vt = lax.dot_general(vp, sel, (((0,), (0,)), ((), ())), preferred_element_type=F32)
            for q in range(spb):
                blk = vt[:, q * LANES:(q + 1) * LANES]
                vt_scr[blk_i * spb + q, p] = jnp.where(head_a, blk[:DH_D], blk[DH_D:]).astype(BF16)
    sk_scr[...] = jnp.zeros(sk_scr.shape, F32)
    pr_scr[...] = jnp.zeros(pr_scr.shape, F32)

    def body(i, carry):
        ta = 2 * i
        here_a = (lane % DH_D) == ta
        here_b = (lane % DH_D) == ta + 1
        ha16, hb16 = here_a.astype(BF16), here_b.astype(BF16)
        for kq in range(nb):
            ra = seq_rows(rows_refs, kq, ta, 1)
            rb = seq_rows(rows_refs, kq, ta + 1, 1)
            for p in range(N_PAIRS_D):
                def col(r, n):
                    return r[:, n * DM_D + p * LANES:n * DM_D + (p + 1) * LANES]
                w2, b1, k1, beta_b, kp_b, x1, x2 = (col(rb, n) for n in range(2, N_SCAN_STEP_ROWS))
                sp = s_scr[kq, p]
                s16 = sp.astype(BF16)
                vt = vt_scr[kq, p]
                lhs = jnp.concatenate(
                    [s16 * col(ra, 0).astype(BF16), s16 * col(ra, 1).astype(BF16),
                     s16 * col(rb, 0).astype(BF16), s16 * col(rb, 1).astype(BF16),
                     vt * ha16, vt * hb16], axis=0)
                red = _dot(lhs, ones_bd)
                skk_a, pr_a, r3, pr_b, v_a, v_b = (red[n * DH_D:(n + 1) * DH_D] for n in range(6))
                skk_b = r3 - skk_a * x1 + v_a * x2
                s_scr[kq, p] = sp * w2 - skk_a * b1 + v_a * k1 - skk_b * beta_b + v_b * kp_b
                sk_scr[kq, p] = jnp.where(here_a, skk_a, jnp.where(here_b, skk_b, sk_scr[kq, p]))
                pr_scr[kq, p] = jnp.where(here_a, pr_a, jnp.where(here_b, pr_b, pr_scr[kq, p]))
        return carry

    steps_per_trip = min(RWKV_STEPS_PER_TRIP, t_c // 2)

    def body_unrolled(j, carry):
        for u in range(steps_per_trip):
            body(steps_per_trip * j + u, carry)
        return carry

    lax.fori_loop(0, t_c // (2 * steps_per_trip), body_unrolled, 0)

    def token_major(tile):
        x = tile.T
        return jnp.concatenate([x[0:t_c], x[DH_D:DH_D + t_c]], axis=1)

    for kq in range(nb):
        rows = seq_rows(rows_refs, kq, 0, t_c)
        v = seq_rows(v_refs, kq, 0, t_c)
        for p in range(N_PAIRS_D):
            x3, x4, c1, c2 = (rows[:, n * DM_D + p * LANES:n * DM_D + (p + 1) * LANES]
                              for n in range(N_SCAN_STEP_ROWS, N_SCAN_ROWS))
            skt = token_major(sk_scr[kq, p])
            vp = v[:, p * LANES:(p + 1) * LANES]
            o = (token_major(pr_scr[kq, p]) - skt * c1 + vp * c2
                 - jnp.roll(skt, 1, axis=0) * x3 + jnp.roll(vp, 1, axis=0) * x4)
            o_ref[kq, :, p * LANES:(p + 1) * LANES] = o

    @pl.when(ci == pl.num_programs(1) - 1)
    def _():
        sout_ref[...] = s_scr[...]


def rwkv_scan(rows, v, s0, *, tok0, t, nb):
    n_seq = s0.shape[0]
    rows_per_seq = t >= DH_D
    t_c = DH_D if rows_per_seq else t
    n_chunks = t // t_c
    if rows_per_seq:
        def tok_spec(ncol):
            return [pl.BlockSpec((t_c, ncol),
                                 lambda g, c, kq=kq: (tok0 // t_c + (g * nb + kq) * n_chunks + c, 0))
                    for kq in range(nb)]
        n_in = nb
    else:
        def tok_spec(ncol):
            return [pl.BlockSpec((nb * t_c, ncol), lambda g, c: (tok0 // (nb * t_c) + g, 0))]
        n_in = 1
    st_shape = (nb,) + s0.shape[1:]
    st_spec = pl.BlockSpec(st_shape, lambda g, c: (g, 0, 0, 0))
    return pl.pallas_call(
        functools.partial(_rwkv_scan_kernel, nb=nb, t_c=t_c, rows_per_seq=rows_per_seq),
        grid=(n_seq // nb, n_chunks),
        in_specs=tok_spec(rows.shape[1]) + tok_spec(DM_D) + [st_spec],
        out_specs=[pl.BlockSpec((nb, t_c, DM_D), lambda g, c: (g, c, 0)), st_spec],
        out_shape=[jax.ShapeDtypeStruct((n_seq, t, DM_D), F32), jax.ShapeDtypeStruct(s0.shape, F32)],
        scratch_shapes=[pltpu.VMEM(st_shape, F32), pltpu.VMEM(st_shape, BF16),
                        pltpu.VMEM(st_shape, F32), pltpu.VMEM(st_shape, F32)],
        compiler_params=_params("parallel", "arbitrary"),
        name="rwkv_scan",
    )(*([rows] * n_in), *([v] * n_in), s0)


def _odd_out_kernel(x_ref, oc_ref, od_ref, gb_ref, lnw_ref, lnb_ref, seg_ref, w_ref, o_ref):
    seg = seg_ref[...]
    o = od_ref[...]
    d = o - _head_sum(o, seg) * (1.0 / DH_D)
    var = _head_sum(d * d, seg) * (1.0 / DH_D)
    od = d * lax.rsqrt(var + RWKV_LN_EPS) * lnw_ref[...] + lnb_ref[...]
    od = (od + gb_ref[:, DM_D:]) * gb_ref[:, 0:DM_D]
    a = jnp.concatenate([oc_ref[...], od], axis=-1).astype(BF16)
    o_ref[...] = x_ref[...] + _dot(a, w_ref[...])


def odd_out(x, o_c, o_d, gb, ln_w, ln_b, seg, w_out):
    nt = x.shape[0]
    tok = lambda n: pl.BlockSpec((TOKEN_TILE, n), lambda i: (i, 0))
    vec = _resident((1, DM_D))
    return pl.pallas_call(
        _odd_out_kernel,
        grid=(nt // TOKEN_TILE,),
        in_specs=[tok(D_MODEL), tok(HALF), tok(DM_D), tok(2 * DM_D), vec, vec,
                  _resident(seg.shape), _resident(w_out.shape)],
        out_specs=tok(D_MODEL),
        out_shape=jax.ShapeDtypeStruct(x.shape, F32),
        compiler_params=_params("parallel"),
        name="odd_out",
    )(x, o_c, o_d, gb, ln_w, ln_b, seg, w_out)


def _state_to_pairs(s):
    n_seq = s.shape[0]
    s = s.reshape(n_seq, N_PAIRS_D, 2, DH_D, DH_D)
    return jnp.transpose(s, (0, 1, 3, 2, 4)).reshape(n_seq, N_PAIRS_D, DH_D, LANES)


def _state_from_pairs(s):
    n_seq = s.shape[0]
    s = s.reshape(n_seq, N_PAIRS_D, DH_D, 2, DH_D)
    return jnp.transpose(s, (0, 1, 3, 2, 4)).reshape(n_seq, H_D, DH_D, DH_D)


def run_rwkv(z, n_p, t_p, n_s, t_s, state, shift, mu, w0, w2, a0, a2, g2, k_k, k_a, r_k, seg,
             *, nb_sample):
    np_tok = n_p * t_p
    zd_s = z[np_tok:, :N_COLS_RWKV].reshape(n_s, t_s, N_COLS_RWKV)
    prev_s = jnp.concatenate([shift[:, None, :], zd_s[:, :-1]], axis=1).reshape(n_s * t_s, N_COLS_RWKV)
    w2p = jnp.pad(w2, ((0, R_A), (0, 0))).astype(BF16)
    a2p = jnp.pad(a2, ((R_W, 0), (0, 0))).astype(BF16)
    row = lambda a: a.reshape(1, -1)
    rows, v, gb = rwkv_prep(z, prev_s, row(mu), row(w0), w2p, row(a0), a2p, g2.astype(BF16),
                            row(k_k), row(k_a), row(r_k), seg, n_prompt_tok=np_tok, t_prompt=t_p)
    zero = jnp.zeros((n_p, N_PAIRS_D, DH_D, LANES), F32)
    o_p, st_p = rwkv_scan(rows, v, zero, tok0=0, t=t_p, nb=n_p)
    o_s, st_s = rwkv_scan(rows, v, _state_to_pairs(state), tok0=np_tok, t=t_s, nb=nb_sample)
    o = jnp.concatenate([o_p.reshape(np_tok, DM_D), o_s.reshape(n_s * t_s, DM_D)], axis=0)
    return o, gb, _state_from_pairs(st_p), _state_from_pairs(st_s)


EVEN_CHUNK = 64
SB_BLOCK = 256
SB_GROUP = 4
RWKV_SAMPLE_GROUP = 8


def kernel(x_prompt, x_sample, p_prompt, p_sample, cache_k, cache_v, page_table, state_gla, state_hgrn, state_rwkv, state_rwkv_shift, ln_ffn1, ffn1_w_gate, ffn1_w_up, ffn1_w_down, ln_mix, ln_ffn2, ffn2_w_gate, ffn2_w_up, ffn2_w_down, ln_ple, ple_w_gate, ple_w_proj, w_in_even, w_out_even, gla_w_gk, gla_b_gk, gla_norm, hgrn_lb_logits, hgrn_norm, w_in_odd, w_out_odd, sb_bias, rwkv_mu, rwkv_w0, rwkv_w2, rwkv_a0, rwkv_a2, rwkv_g2, rwkv_k_k, rwkv_k_a, rwkv_r_k, rwkv_ln_w, rwkv_ln_b, final_norm):
    n_p, t_p, _ = x_prompt.shape
    n_s, t_s, _ = x_sample.shape
    np_tok, ns_tok = n_p * t_p, n_s * t_s
    assert ns_tok == TOKEN_TILE and t_p % TOKEN_TILE == 0 and t_s < SUBLANES
    bf = lambda w: w.astype(BF16)
    row = lambda a: a.reshape(1, -1)

    x = jnp.concatenate([x_prompt.reshape(np_tok, D_MODEL), x_sample.reshape(ns_tok, D_MODEL)], axis=0)
    p_all = jnp.concatenate([p_prompt.reshape(DEPTH, np_tok, PLE_DIM),
                             p_sample.reshape(DEPTH, ns_tok, PLE_DIM)], axis=1)
    sm = jax.nn.softmax(hgrn_lb_logits.astype(F32), axis=0)
    lower_bounds = jnp.concatenate([jnp.zeros_like(sm[:1]), jnp.cumsum(sm[1:], axis=0)], axis=0)
    head_id = jnp.arange(DM_D) // DH_D
    seg = (head_id[:, None] == head_id[None, :]).astype(BF16)
    n_phys = cache_k.shape[1]
    ck = jnp.transpose(cache_k, (0, 1, 3, 4, 2)).reshape(cache_k.shape[0], n_phys, HALF, PAGE_SIZE)
    cv = jnp.transpose(cache_v, (0, 1, 3, 4, 2)).reshape(cache_v.shape[0], n_phys, HALF, PAGE_SIZE)
    pt_flat = page_table.reshape(-1)
    q0 = N_COLS_RWKV
    k0, v0 = q0 + HALF, q0 + 2 * HALF

    k_rows_p, v_rows_p, k_rows_s, v_rows_s = [], [], [], []
    gla_p, hgrn_p = [], []
    gla_s, hgrn_s = jnp.zeros(state_gla.shape, F32), jnp.zeros(state_hgrn.shape, F32)
    rwkv_p, rwkv_s, shift_p, shift_s = [], [], [], []
    for i in range(DEPTH):
        j = i // 2
        x = ffn(x, row(ln_ffn1[i]), bf(ffn1_w_gate[i]), bf(ffn1_w_up[i]), bf(ffn1_w_down[i]))
        if i % 2 == 0:
            w_in, wgk, bgk = even_weights(w_in_even[j], gla_w_gk[j], gla_b_gk[j])
            z = rms_matmul(x, row(ln_mix[i]), w_in)
            o, (ga_p, hb_p), gla_s, hgrn_s = run_even_mixer(
                z, n_p, t_p, n_s, t_s, state_gla, state_hgrn, j, gla_s, hgrn_s, wgk, bgk,
                lower_bounds[j], gla_norm[j], hgrn_norm[j], chunk=EVEN_CHUNK)
            x = matmul_residual(x, o, bf(w_out_even[j]))
            gla_p.append(ga_p)
            hgrn_p.append(hb_p)
        else:
            w = w_in_odd[j]
            z, qkv = rms_matmul(x, row(ln_mix[i]),
                                bf(jnp.concatenate([w[:, 3 * HALF:], w[:, :3 * HALF]], axis=1)), bf16_from=q0)
            oc_p = sb_attention_prompt(qkv, sb_bias[j], n_seq=n_p, t=t_p, q_col=0, blk=SB_BLOCK,
                                       group=SB_GROUP)
            z_s = z[np_tok:]
            q_s, kc_s, vc_s = (z_s[:, c:c + HALF].reshape(n_s, t_s, HALF) for c in (q0, k0, v0))
            bias_rows = jnp.broadcast_to(jnp.tile(sb_bias[j], t_s)[:, None], (t_s * H_C, PAGE_SIZE))
            oc_s = sb_attention_sample(q_s, kc_s, vc_s, ck, cv, j, pt_flat, bias_rows)
            o_c = jnp.concatenate([oc_p, oc_s.reshape(ns_tok, HALF)], axis=0)
            o_d, gb, sd_p, sd_s = run_rwkv(
                z, n_p, t_p, n_s, t_s, state_rwkv[j], state_rwkv_shift[j], rwkv_mu[j], rwkv_w0[j],
                rwkv_w2[j], rwkv_a0[j], rwkv_a2[j], rwkv_g2[j], rwkv_k_k[j], rwkv_k_a[j],
                rwkv_r_k[j], seg, nb_sample=RWKV_SAMPLE_GROUP)
            x = odd_out(x, o_c, o_d, gb, row(rwkv_ln_w[j]), row(rwkv_ln_b[j]), seg, bf(w_out_odd[j]))
            k_rows_p.append(lax.slice(z, (0, k0), (np_tok, k0 + HALF)).reshape(n_p, t_p, H_C, DH_C))
            v_rows_p.append(lax.slice(z, (0, v0), (np_tok, v0 + HALF)).reshape(n_p, t_p, H_C, DH_C))
            k_rows_s.append(kc_s.reshape(n_s, t_s, H_C, DH_C))
            v_rows_s.append(vc_s.reshape(n_s, t_s, H_C, DH_C))
            rwkv_p.append(sd_p)
            rwkv_s.append(sd_s)
            shift_p.append(jnp.concatenate(
                [lax.slice(z, ((b + 1) * t_p - 1, 0), ((b + 1) * t_p, N_COLS_RWKV)) for b in range(n_p)],
                axis=0))
            shift_s.append(z_s.reshape(n_s, t_s, -1)[:, -1, :N_COLS_RWKV])
        x = ffn(x, row(ln_ffn2[i]), bf(ffn2_w_gate[i]), bf(ffn2_w_up[i]), bf(ffn2_w_down[i]))
        x = ple(x, p_all[i], row(ln_ple[i]), bf(ple_w_gate[i]), bf(ple_w_proj[i]))

    y = final_rmsnorm(x, row(final_norm))
    return (y[:np_tok].reshape(n_p, t_p, D_MODEL), y[np_tok:].reshape(n_s, t_s, D_MODEL),
            jnp.stack(k_rows_p), jnp.stack(v_rows_p), jnp.stack(k_rows_s), jnp.stack(v_rows_s),
            jnp.stack(gla_p), gla_s, jnp.stack(hgrn_p), hgrn_s,
            jnp.stack(rwkv_p), jnp.stack(rwkv_s), jnp.stack(shift_p), jnp.stack(shift_s))
```

```python
import functools

import jax
import jax.numpy as jnp
from jax import lax
from jax.experimental import pallas as pl
from jax.experimental.pallas import tpu as pltpu

F32 = jnp.float32
BF16 = jnp.bfloat16

D_MODEL = 1024
DEPTH = 4
PAGE_SIZE = 128
HALF = D_MODEL // 2
D_FF = 2816
PLE_DIM = 256
RMS_EPS = 1e-6
H_A = 4
DK_A = HALF // H_A // 2
DV_A = HALF // H_A
GLA_GK_RANK = 16
GLA_GK_NORM = 16.0
H_B = 4
EXP_B = 128
DV_B = HALF // H_B
H_C = 8
DH_C = HALF // H_C
H_D = 8
DH_D = HALF // H_D
DM_D = H_D * DH_D
R_W = 64
R_A = 64
R_G = 128
RWKV_LN_EPS = 64e-5
N_COLS_RWKV = 3 * DM_D + R_W + R_A + R_G
N_COLS_ODD = 3 * HALF + N_COLS_RWKV

LANES = 128
SUBLANES = 8
MXU_DIM = 256
VMEM_LIMIT_BYTES = 56 * 1024 * 1024

TOKEN_TILE = 512
FFN_TOKEN_TILE = 768
N_EVEN_HEADS = H_A + H_B
N_COLS_EVEN_PAD = (4 * N_EVEN_HEADS + 1) * LANES
GATED_SUB_BLOCK = 16


def _params(*semantics):
    return pltpu.CompilerParams(dimension_semantics=semantics,
                                vmem_limit_bytes=VMEM_LIMIT_BYTES)


def _resident(shape):
    return pl.BlockSpec(shape, lambda *_: (0,) * len(shape), pipeline_mode=pl.Buffered(1))


def _rms(x, g):
    return x * lax.rsqrt(jnp.mean(x * x, axis=-1, keepdims=True) + RMS_EPS) * g


def _dot(a, b):
    return jnp.dot(a, b, preferred_element_type=F32)


def _ffn_kernel(x_ref, ln_ref, wg_ref, wu_ref, wd_ref, o_ref):
    x = x_ref[...]
    h = _rms(x, ln_ref[...]).astype(BF16)
    acc = jnp.zeros(x.shape, F32)
    for f0 in range(0, D_FF, MXU_DIM):
        g = _dot(h, wg_ref[:, f0:f0 + MXU_DIM])
        u = _dot(h, wu_ref[:, f0:f0 + MXU_DIM])
        a = (g * jax.nn.sigmoid(g) * u).astype(BF16)
        acc = acc + _dot(a, wd_ref[f0:f0 + MXU_DIM, :])
    o_ref[...] = x + 0.5 * acc


def ffn(x, ln, wg, wu, wd):
    nt = x.shape[0]
    tile = FFN_TOKEN_TILE if nt % FFN_TOKEN_TILE == 0 else TOKEN_TILE
    tok = pl.BlockSpec((tile, D_MODEL), lambda i: (i, 0))
    return pl.pallas_call(
        _ffn_kernel,
        grid=(nt // tile,),
        in_specs=[tok, _resident((1, D_MODEL)), _resident((D_MODEL, D_FF)),
                  _resident((D_MODEL, D_FF)), _resident((D_FF, D_MODEL))],
        out_specs=tok,
        out_shape=jax.ShapeDtypeStruct(x.shape, F32),
        compiler_params=_params("parallel"),
        name="ffn",
    )(x, ln, wg, wu, wd)


def _rms_matmul_kernel(x_ref, ln_ref, w_ref, o_ref, *maybe_ob_ref, n_chunk, bf16_from):
    h = _rms(x_ref[...], ln_ref[...]).astype(BF16)
    n = w_ref.shape[1]
    for n0 in range(0, n, n_chunk):
        n1 = min(n0 + n_chunk, n)
        o_ref[:, n0:n1] = _dot(h, w_ref[:, n0:n1])
    if maybe_ob_ref:
        maybe_ob_ref[0][...] = o_ref[:, bf16_from:].astype(BF16)


def rms_matmul(x, ln, w, bf16_from=None):
    nt, n = x.shape[0], w.shape[1]
    out_specs = [pl.BlockSpec((TOKEN_TILE, n), lambda i: (i, 0))]
    out_shape = [jax.ShapeDtypeStruct((nt, n), F32)]
    if bf16_from is not None:
        out_specs.append(pl.BlockSpec((TOKEN_TILE, n - bf16_from), lambda i: (i, 0)))
        out_shape.append(jax.ShapeDtypeStruct((nt, n - bf16_from), BF16))
    out = pl.pallas_call(
        functools.partial(_rms_matmul_kernel, n_chunk=2 * MXU_DIM, bf16_from=bf16_from),
        grid=(nt // TOKEN_TILE,),
        in_specs=[pl.BlockSpec((TOKEN_TILE, D_MODEL), lambda i: (i, 0)),
                  _resident((1, D_MODEL)), _resident((D_MODEL, n))],
        out_specs=out_specs,
        out_shape=out_shape,
        compiler_params=_params("parallel"),
        name="rms_matmul",
    )(x, ln, w)
    return out[0] if bf16_from is None else out


def _odd_in_proj_kernel(x_ref, ln_ref, w_ref, kt_prev_ref, vt_prev_ref, o_ref, ob_ref, kt_ref, vt_ref,
                        *, n_chunk, q0, n_feature_major_tiles):
    del kt_prev_ref, vt_prev_ref
    h = _rms(x_ref[...], ln_ref[...]).astype(BF16)
    n = w_ref.shape[1]
    for n0 in range(0, n, n_chunk):
        n1 = min(n0 + n_chunk, n)
        o_ref[:, n0:n1] = _dot(h, w_ref[:, n0:n1])
    ob_ref[...] = o_ref[:, q0:].astype(BF16)

    @pl.when(pl.program_id(0) < n_feature_major_tiles)
    def _():
        kt_ref[...] = o_ref[:, q0 + HALF:q0 + 2 * HALF].T
        vt_ref[...] = o_ref[:, q0 + 2 * HALF:q0 + 3 * HALF].T


def odd_in_proj(x, ln, w, kt_buf, vt_buf, layer):
    nt, n = x.shape[0], w.shape[1]
    q0 = n - 3 * HALF
    _, n_seq, _, t = kt_buf.shape
    tiles_per_seq = t // TOKEN_TILE
    n_fm = n_seq * tiles_per_seq
    tok = lambda ncol: pl.BlockSpec((TOKEN_TILE, ncol), lambda i: (i, 0))

    def fm_index(i):
        i = jnp.minimum(i, n_fm - 1)
        return layer, i // tiles_per_seq, 0, i % tiles_per_seq

    fm = pl.BlockSpec((None, None, HALF, TOKEN_TILE), fm_index)
    any_spec = pl.BlockSpec(memory_space=pl.ANY)
    return pl.pallas_call(
        functools.partial(_odd_in_proj_kernel, n_chunk=2 * MXU_DIM, q0=q0,
                          n_feature_major_tiles=n_fm),
        grid=(nt // TOKEN_TILE,),
        in_specs=[tok(D_MODEL), _resident((1, D_MODEL)), _resident((D_MODEL, n)), any_spec, any_spec],
        out_specs=[tok(n), tok(n - q0), fm, fm],
        out_shape=[jax.ShapeDtypeStruct((nt, n), F32), jax.ShapeDtypeStruct((nt, n - q0), BF16),
                   jax.ShapeDtypeStruct(kt_buf.shape, F32), jax.ShapeDtypeStruct(vt_buf.shape, F32)],
        input_output_aliases={3: 2, 4: 3},
        compiler_params=_params("arbitrary"),
        name="odd_in_proj",
    )(x, ln, w, kt_buf, vt_buf)


def _matmul_residual_kernel(x_ref, a_ref, w_ref, o_ref):
    o_ref[...] = x_ref[...] + _dot(a_ref[...].astype(BF16), w_ref[...])


def matmul_residual(x, a, w):
    nt, k = a.shape
    return pl.pallas_call(
        _matmul_residual_kernel,
        grid=(nt // TOKEN_TILE,),
        in_specs=[pl.BlockSpec((TOKEN_TILE, D_MODEL), lambda i: (i, 0)),
                  pl.BlockSpec((TOKEN_TILE, k), lambda i: (i, 0)),
                  _resident((k, D_MODEL))],
        out_specs=pl.BlockSpec((TOKEN_TILE, D_MODEL), lambda i: (i, 0)),
        out_shape=jax.ShapeDtypeStruct(x.shape, F32),
        compiler_params=_params("parallel"),
        name="matmul_residual",
    )(x, a, w)


def _ple_kernel(x_ref, p_ref, ln_ref, wg_ref, wp_ref, o_ref):
    x = x_ref[...]
    h = _rms(x, ln_ref[...]).astype(BF16)
    gate = jax.nn.sigmoid(_dot(h, wg_ref[...]))
    o_ref[...] = x + gate * _dot(p_ref[...].astype(BF16), wp_ref[...])


def ple(x, p, ln, wg, wp):
    nt = x.shape[0]
    tok = pl.BlockSpec((TOKEN_TILE, D_MODEL), lambda i: (i, 0))
    return pl.pallas_call(
        _ple_kernel,
        grid=(nt // TOKEN_TILE,),
        in_specs=[tok, pl.BlockSpec((TOKEN_TILE, PLE_DIM), lambda i: (i, 0)),
                  _resident((1, D_MODEL)), _resident((D_MODEL, D_MODEL)),
                  _resident((PLE_DIM, D_MODEL))],
        out_specs=tok,
        out_shape=jax.ShapeDtypeStruct(x.shape, F32),
        compiler_params=_params("parallel"),
        name="ple",
    )(x, p, ln, wg, wp)


def _final_norm_kernel(x_ref, ln_ref, o_ref):
    o_ref[...] = _rms(x_ref[...], ln_ref[...])


def final_rmsnorm(x, ln):
    nt = x.shape[0]
    tok = pl.BlockSpec((TOKEN_TILE, D_MODEL), lambda i: (i, 0))
    return pl.pallas_call(
        _final_norm_kernel,
        grid=(nt // TOKEN_TILE,),
        in_specs=[tok, _resident((1, D_MODEL))],
        out_specs=tok,
        out_shape=jax.ShapeDtypeStruct(x.shape, F32),
        compiler_params=_params("parallel"),
        name="final_norm",
    )(x, ln)


def _log_sigmoid(x):
    return jnp.minimum(x, 0.0) - jnp.log1p(jnp.exp(-jnp.abs(x)))


def _split_bf16(x):
    hi = x.astype(BF16)
    lo = (x - hi.astype(F32)).astype(BF16)
    return hi, lo


def _gated_chunks(heads, st_ref, tri):
    c = heads[0][0].shape[0]
    n = len(heads)
    sub = min(c, GATED_SUB_BLOCK)
    row = lax.broadcasted_iota(jnp.int32, (sub, 1), 0)
    tn_dims = (((0,), (0,)), ((), ()))

    bs = []
    for q, k, v, g in heads:
        g_hi, g_lo = _split_bf16(g * LOG2E)
        bs.append(_dot(tri, g_hi) + _dot(tri, g_lo))
    sts = [st_ref[h] for h in range(n)]
    o_inter = [lax.dot_general((heads[h][0] * jnp.exp2(bs[h])).astype(BF16), sts[h].astype(BF16),
                               _NT_DIMS, preferred_element_type=F32) for h in range(n)]

    scores = [[None] * (c // sub) for _ in range(n)]
    for h, (q, k, v, g) in enumerate(heads):
        b = bs[h]
        for i, r0 in enumerate(range(sub, c, sub), start=1):
            b_ref = b[r0:r0 + 1, :]
            q_rel = (q[r0:r0 + sub, :] * jnp.exp2(b[r0:r0 + sub, :] - b_ref)).astype(BF16)
            k_rel = (k[:r0, :] * jnp.exp2(b_ref - b[:r0, :])).astype(BF16)
            scores[h][i] = lax.dot_general(q_rel, k_rel, _NT_DIMS, preferred_element_type=F32)
    off_diag = [[None] * (c // sub) for _ in range(n)]
    for h, (q, k, v, g) in enumerate(heads):
        for i, r0 in enumerate(range(sub, c, sub), start=1):
            off_diag[h][i] = _dot(scores[h][i].astype(BF16), v[:r0, :].astype(BF16))

    for h, (q, k, v, g) in enumerate(heads):
        b_last = bs[h][c - 1:c, :]
        ke = (k * jnp.exp2(b_last - bs[h])).astype(BF16)
        st_ref[h] = sts[h] * jnp.exp2(b_last) + lax.dot_general(
            v.astype(BF16), ke, tn_dims, preferred_element_type=F32)

    outs = []
    for h, (q, k, v, g) in enumerate(heads):
        parts = []
        for i, r0 in enumerate(range(0, c, sub)):
            ki, vi, bi = (a[r0:r0 + sub, :] for a in (k, v, bs[h]))
            oi = o_inter[h][r0:r0 + sub, :]
            if i > 0:
                oi = oi + off_diag[h][i]
            for p0 in range(0, sub, SUBLANES):
                qp = q[r0 + p0:r0 + p0 + SUBLANES, :]
                bp = bi[p0:p0 + SUBLANES, :]
                op = oi[p0:p0 + SUBLANES, :]
                for s in range(min(sub, p0 + SUBLANES)):
                    d = bp - bi[s:s + 1, :]
                    if s > p0:
                        d = jnp.where(row[p0:p0 + SUBLANES] >= s, d, -1e30)
                    a_s = jnp.sum(qp * ki[s:s + 1, :] * jnp.exp2(d), axis=-1, keepdims=True)
                    op = op + a_s * vi[s:s + 1, :]
                parts.append(op)
        outs.append(jnp.concatenate(parts, axis=0) if len(parts) > 1 else parts[0])
    return outs


def _even_heads(z_ref, wgk_ref, bgk_ref, lb_ref, na_ref, nb_ref):
    def blk(i):
        return z_ref[:, i * LANES:(i + 1) * LANES]

    heads, gains, gates = [], [], []
    gk = _log_sigmoid(_dot(blk(4 * N_EVEN_HEADS).astype(BF16), wgk_ref[...]) + bgk_ref[...])
    gk = gk * (1.0 / GLA_GK_NORM)
    for h in range(H_A):
        heads.append((blk(h) * DK_A ** -0.5, blk(H_A + h), blk(2 * H_A + h),
                      gk[:, h * LANES:(h + 1) * LANES]))
        gains.append(na_ref[...])
        gates.append(blk(3 * H_A + h))
    base = 4 * H_A
    for h in range(H_B):
        lb = lb_ref[:, h * LANES:(h + 1) * LANES]
        fb = blk(base + H_B + h)
        a = jnp.log1p(-lb) + _log_sigmoid(fb)
        log_lb = jnp.log(lb)
        g = jnp.maximum(a, log_lb) + jnp.log1p(jnp.exp(-jnp.abs(a - log_lb)))
        heads.append((blk(base + h), (1.0 - lb) * jax.nn.sigmoid(-fb), blk(base + 2 * H_B + h), g))
        gains.append(nb_ref[...])
        gates.append(blk(base + 3 * H_B + h))
    return heads, gains, gates


def _even_finish(o, gain, gate):
    o = o * lax.rsqrt(jnp.mean(o * o, axis=-1, keepdims=True) + RMS_EPS) * gain
    return o * (gate * jax.nn.sigmoid(gate))


def _lower_tri(c):
    return (lax.broadcasted_iota(jnp.int32, (c, c), 0)
            >= lax.broadcasted_iota(jnp.int32, (c, c), 1)).astype(BF16)


def _even_mixer_kernel(z_ref, s0_ref, wgk_ref, bgk_ref, lb_ref, na_ref, nb_ref,
                       o_ref, sout_ref, st_ref):
    ci = pl.program_id(1)

    @pl.when(ci == 0)
    def _():
        st_ref[...] = s0_ref[0]

    heads, gains, gates = _even_heads(z_ref, wgk_ref, bgk_ref, lb_ref, na_ref, nb_ref)
    for h, o in enumerate(_gated_chunks(heads, st_ref, _lower_tri(z_ref.shape[0]))):
        o_ref[:, h * LANES:(h + 1) * LANES] = _even_finish(o, gains[h], gates[h])

    @pl.when(ci == pl.num_programs(1) - 1)
    def _():
        sout_ref[0] = st_ref[...]


def _even_mixer_short_kernel(z_ref, sg_ref, sh_ref, wgk_ref, bgk_ref, lb_ref, na_ref, nb_ref,
                             o_prev_ref, sg_prev_ref, sh_prev_ref, o_ref, sg_out_ref, sh_out_ref,
                             st_ref, *, t_s):
    del o_prev_ref, sg_prev_ref, sh_prev_ref
    c = z_ref.shape[0]
    spb = c // t_s
    row = lax.broadcasted_iota(jnp.int32, (c, 1), 0)
    zeros = jnp.zeros((LANES, LANES - DK_A), F32)
    for q in range(spb):
        for h in range(H_A):
            st_ref[q * N_EVEN_HEADS + h] = jnp.concatenate([sg_ref[q, h].T, zeros], axis=1)
        for h in range(H_B):
            st_ref[q * N_EVEN_HEADS + H_A + h] = sh_ref[q, h].T

    heads, gains, gates = _even_heads(z_ref, wgk_ref, bgk_ref, lb_ref, na_ref, nb_ref)
    all_heads = []
    for q in range(spb):
        mine = (row // t_s) == q
        all_heads += [(hq, jnp.where(mine, hk, 0.0), hv, jnp.where(mine, hg, 0.0))
                      for hq, hk, hv, hg in heads]
    outs = _gated_chunks(all_heads, st_ref, _lower_tri(c))
    for h in range(N_EVEN_HEADS):
        o = outs[h]
        for q in range(1, spb):
            o = jnp.where((row // t_s) == q, outs[q * N_EVEN_HEADS + h], o)
        o_ref[:, h * LANES:(h + 1) * LANES] = _even_finish(o, gains[h], gates[h])

    for q in range(spb):
        for h in range(H_A):
            sg_out_ref[q, h] = st_ref[q * N_EVEN_HEADS + h].T[:DK_A, :]
        for h in range(H_B):
            sh_out_ref[q, h] = st_ref[q * N_EVEN_HEADS + H_A + h].T


def even_mixer(z, s0, wgk, bgk, lb, norm_a, norm_b, *, n_seq, t, chunk):
    nt, ncol = z.shape
    n_chunks = t // chunk
    st_shape = (N_EVEN_HEADS, LANES, LANES)
    return pl.pallas_call(
        _even_mixer_kernel,
        grid=(n_seq, n_chunks),
        in_specs=[pl.BlockSpec((chunk, ncol), lambda b, c: (b * n_chunks + c, 0)),
                  pl.BlockSpec((1,) + st_shape, lambda b, c: (b, 0, 0, 0)),
                  _resident(wgk.shape), _resident(bgk.shape), _resident(lb.shape),
                  _resident(norm_a.shape), _resident(norm_b.shape)],
        out_specs=[pl.BlockSpec((chunk, D_MODEL), lambda b, c: (b * n_chunks + c, 0)),
                   pl.BlockSpec((1,) + st_shape, lambda b, c: (b, 0, 0, 0))],
        out_shape=[jax.ShapeDtypeStruct((nt, D_MODEL), F32),
                   jax.ShapeDtypeStruct((n_seq,) + st_shape, F32)],
        scratch_shapes=[pltpu.VMEM(st_shape, F32)],
        compiler_params=_params("parallel", "arbitrary"),
        name="even_mixer",
    )(z, s0, wgk, bgk, lb, norm_a, norm_b)


def even_mixer_short(z, state_gla, state_hgrn, layer, wgk, bgk, lb, norm_a, norm_b,
                     o_buf, gla_buf, hgrn_buf, *, tok0, n_seq, t_s):
    ncol = z.shape[1]
    spb = SUBLANES // t_s

    def st_spec(a):
        return pl.BlockSpec((None, spb) + a.shape[2:], lambda g: (layer, g, 0, 0, 0))

    any_spec = pl.BlockSpec(memory_space=pl.ANY)
    row_blk = lambda n: pl.BlockSpec((SUBLANES, n), lambda g: (tok0 // SUBLANES + g, 0))
    return pl.pallas_call(
        functools.partial(_even_mixer_short_kernel, t_s=t_s),
        grid=(n_seq // spb,),
        in_specs=[row_blk(ncol), st_spec(state_gla), st_spec(state_hgrn),
                  _resident(wgk.shape), _resident(bgk.shape), _resident(lb.shape),
                  _resident(norm_a.shape), _resident(norm_b.shape), any_spec, any_spec, any_spec],
        out_specs=[row_blk(D_MODEL), st_spec(gla_buf), st_spec(hgrn_buf)],
        out_shape=[jax.ShapeDtypeStruct(o_buf.shape, F32), jax.ShapeDtypeStruct(gla_buf.shape, F32),
                   jax.ShapeDtypeStruct(hgrn_buf.shape, F32)],
        input_output_aliases={8: 0, 9: 1, 10: 2},
        scratch_shapes=[pltpu.VMEM((spb * N_EVEN_HEADS, LANES, LANES), F32)],
        compiler_params=_params("parallel"),
        name="even_mixer_short",
    )(z, state_gla, state_hgrn, wgk, bgk, lb, norm_a, norm_b, o_buf, gla_buf, hgrn_buf)


def _pad_heads(w, n_heads):
    d = w.shape[-1] // n_heads
    w = w.reshape(w.shape[:-1] + (n_heads, d))
    w = jnp.pad(w, [(0, 0)] * (w.ndim - 1) + [(0, LANES - d)])
    return w.reshape(w.shape[:-2] + (n_heads * LANES,))


def _split_cols(w, sizes):
    out, o = [], 0
    for s in sizes:
        out.append(w[..., o:o + s])
        o += s
    return out


def even_weights(w_in, w_gk, b_gk):
    ka = H_A * DK_A
    qa, kk, va, gk_in, ga, qb, fb, ib, gb = _split_cols(
        w_in, (ka, ka, HALF, GLA_GK_RANK, HALF, HALF, HALF, HALF, HALF))
    gk_in = jnp.pad(gk_in, ((0, 0), (0, LANES - GLA_GK_RANK)))
    w = jnp.concatenate([_pad_heads(qa, H_A), _pad_heads(kk, H_A), va, ga, qb, fb, ib, gb, gk_in],
                        axis=-1).astype(BF16)
    wgk = jnp.pad(_pad_heads(w_gk, H_A), ((0, LANES - GLA_GK_RANK), (0, 0))).astype(BF16)
    bgk = _pad_heads(b_gk[None, :], H_A)
    return w, wgk, bgk


def run_even_mixer(z, n_p, t_p, n_s, t_s, state_gla, state_hgrn, layer, gla_buf, hgrn_buf,
                   wgk, bgk, lb, norm_a, norm_b, *, chunk):
    args = (wgk, bgk, lb[None, :], norm_a[None, :], norm_b[None, :])
    zero = jnp.zeros((n_p, N_EVEN_HEADS, LANES, LANES), F32)
    o_buf, st_p = even_mixer(z, zero, *args, n_seq=n_p, t=t_p, chunk=chunk)
    o, gla_buf, hgrn_buf = even_mixer_short(z, state_gla, state_hgrn, layer, *args,
                                            o_buf, gla_buf, hgrn_buf,
                                            tok0=n_p * t_p, n_seq=n_s, t_s=t_s)
    gla_p = jnp.swapaxes(st_p[:, :H_A], -1, -2)[:, :, :DK_A, :]
    hgrn_p = jnp.swapaxes(st_p[:, H_A:], -1, -2)
    return o, (gla_p, hgrn_p), gla_buf, hgrn_buf


_NT_DIMS = (((1,), (1,)), ((), ()))
LOG2E = 1.4426950408889634


def _neg_softplus(z):
    return -(jnp.maximum(z, 0.0) + jnp.log1p(jnp.exp(-jnp.abs(z))))


def _suffix_tri(n):
    return (lax.broadcasted_iota(jnp.int32, (n, n), 0)
            >= lax.broadcasted_iota(jnp.int32, (n, n), 1)).astype(BF16)


def _sb_prompt_kernel(bias_ref, q_ref, k_ref, v_ref, o_ref, u_ref, tot_ref, carry_ref, acc_ref,
                      *, blk, group):
    pair = pl.program_id(1)
    qi = pl.program_id(2)
    lane = lax.broadcasted_iota(jnp.int32, (1, LANES), 1)
    head_a = lane < DH_C
    q2 = q_ref[...].astype(F32) * (DH_C ** -0.5 * LOG2E)
    qs = jnp.concatenate([jnp.where(head_a, q2, 0.0), jnp.where(head_a, 0.0, q2)], axis=0).astype(BF16)
    row = lax.broadcasted_iota(jnp.int32, (2 * blk, 1), 0)
    bias2 = jnp.where(row < blk, bias_ref[2 * pair], bias_ref[2 * pair + 1]) * LOG2E
    neg_tri = -_suffix_tri(blk)
    strict = lax.broadcasted_iota(jnp.int32, (2 * blk, blk), 1) < (row % blk)

    def rows_of(ref, j):
        return ref[pl.ds(pl.multiple_of(j * blk, blk), blk), :]

    def scores(slot, j, diagonal=False):
        b2 = jnp.where(j >= 0, bias2, -1e30)
        z2 = lax.dot_general(qs, rows_of(k_ref, jnp.maximum(j, 0)), _NT_DIMS,
                             preferred_element_type=F32) + b2
        if diagonal:
            z2 = jnp.where(strict, z2, -1e30)
        sp2 = jnp.maximum(z2, 0.0) + jnp.log2(1.0 + jnp.exp2(-jnp.abs(z2)))
        incl = _dot(sp2.astype(BF16), neg_tri)
        u_ref[slot] = z2 + incl
        tot_ref[slot] = incl[:, 0:1]

    def accumulate(slot, j):
        w = jnp.exp2(u_ref[slot] + carry_ref[...])
        acc_ref[...] += _dot(w.astype(BF16), rows_of(v_ref, jnp.maximum(j, 0)))
        carry_ref[...] += tot_ref[slot]

    def body(k, _):
        j = qi - group * k
        for s in range(group):
            accumulate(s, j + group - s)
        for s in range(group):
            scores(s, j - s)
        return 0

    carry_ref[...] = jnp.zeros(carry_ref.shape, F32)
    acc_ref[...] = jnp.zeros(acc_ref.shape, F32)
    for s in range(group):
        scores(s, qi - s, diagonal=(s == 0))
    lax.fori_loop(1, qi // group + 1, body, 0)
    for s in range(group):
        accumulate(s, qi % group - s)
    o_ref[...] = jnp.where(head_a, acc_ref[:blk, :], acc_ref[blk:, :])


def sb_attention_prompt(z, bias, *, n_seq, t, q_col, blk, group):
    nq = t // blk
    n_pairs = H_C // 2
    return pl.pallas_call(
        functools.partial(_sb_prompt_kernel, blk=blk, group=group),
        grid=(n_seq, n_pairs, nq),
        in_specs=[pl.BlockSpec(memory_space=pltpu.SMEM),
                  pl.BlockSpec((blk, LANES), lambda b, p, i: (b * nq + i, q_col + p)),
                  pl.BlockSpec((t, LANES), lambda b, p, i: (b, q_col + n_pairs + p)),
                  pl.BlockSpec((t, LANES), lambda b, p, i: (b, q_col + 2 * n_pairs + p))],
        out_specs=pl.BlockSpec((blk, LANES), lambda b, p, i: (b * nq + i, p)),
        out_shape=jax.ShapeDtypeStruct((n_seq * t, HALF), F32),
        scratch_shapes=[pltpu.VMEM((group, 2 * blk, blk), F32), pltpu.VMEM((group, 2 * blk, 1), F32),
                        pltpu.VMEM((2 * blk, 1), F32), pltpu.VMEM((2 * blk, LANES), F32)],
        compiler_params=_params("parallel", "parallel", "arbitrary"),
        name="sb_attention_prompt",
    )(bias, z, z, z)


def _sb_sample_kernel(pt_ref, q_ref, kn_ref, vn_ref, bias_ref, *refs, n_pages, t_s):
    del pt_ref
    k_pages, v_pages, o_ref = refs[:n_pages], refs[n_pages:2 * n_pages], refs[2 * n_pages]
    rows = t_s * H_C
    q = q_ref[0] * DH_C ** -0.5
    qx = jnp.concatenate([jnp.broadcast_to(q[i:i + 1, :], (H_C, HALF)) for i in range(t_s)], axis=0)
    r_id = lax.broadcasted_iota(jnp.int32, (rows, HALF), 0)
    l_id = lax.broadcasted_iota(jnp.int32, (rows, HALF), 1)
    head_lanes = (l_id // DH_C) == (r_id % H_C)
    qx = jnp.where(head_lanes, qx, 0.0)
    bias = bias_ref[...]
    q_idx = lax.broadcasted_iota(jnp.int32, (rows, 1), 0) // H_C

    carry = jnp.zeros((rows, 1), F32)
    acc = jnp.zeros((rows, HALF), F32)
    kn, vn = kn_ref[0], vn_ref[0]
    for j in reversed(range(t_s)):
        visible = q_idx > j
        z = jnp.sum(qx * kn[j:j + 1, :], axis=-1, keepdims=True) + bias[:, 0:1]
        lk = jnp.where(visible, _neg_softplus(z), 0.0)
        w = jnp.where(visible, jnp.exp(z + lk + carry), 0.0)
        acc = acc + w * vn[j:j + 1, :]
        carry = carry + lk

    qx = qx.astype(BF16)
    tri = _suffix_tri(PAGE_SIZE)
    zs = [_dot(qx, k_pages[p][...].astype(BF16)) + bias for p in range(n_pages)]
    incls = [_dot(_neg_softplus(z).astype(BF16), tri) for z in zs]
    ws = [None] * n_pages
    for p in reversed(range(n_pages)):
        ws[p] = jnp.exp(zs[p] + incls[p] + carry).astype(BF16)
        carry = carry + incls[p][:, 0:1]
    for p in range(n_pages):
        acc = acc + lax.dot_general(ws[p], v_pages[p][...].astype(BF16), _NT_DIMS,
                                    preferred_element_type=F32)

    acc = jnp.where(head_lanes, acc, 0.0)
    for i in range(t_s):
        o_ref[0, i:i + 1, :] = jnp.sum(acc[i * H_C:(i + 1) * H_C, :], axis=0, keepdims=True)


def sb_attention_sample(q, k_new, v_new, cache_k, cache_v, layer, page_table, bias_rows):
    n_seq, t_s, _ = q.shape
    n_pages = page_table.shape[0] // n_seq
    tok = pl.BlockSpec((1, t_s, HALF), lambda b, pt: (b, 0, 0))

    def page_spec(p):
        return pl.BlockSpec((None, None, HALF, PAGE_SIZE),
                            lambda b, pt: (layer, pt[b * n_pages + p], 0, 0))

    pages = [page_spec(p) for p in range(n_pages)]
    return pl.pallas_call(
        functools.partial(_sb_sample_kernel, n_pages=n_pages, t_s=t_s),
        grid_spec=pltpu.PrefetchScalarGridSpec(
            num_scalar_prefetch=1,
            grid=(n_seq,),
            in_specs=[tok, tok, tok,
                      pl.BlockSpec(bias_rows.shape, lambda b, pt: (0, 0))] + pages + pages,
            out_specs=tok),
        out_shape=jax.ShapeDtypeStruct((n_seq, t_s, HALF), F32),
        compiler_params=_params("parallel"),
        name="sb_attention_sample",
    )(page_table, q, k_new, v_new, bias_rows, *([cache_k] * n_pages), *([cache_v] * n_pages))


N_SCAN_ROWS = 13
N_SCAN_STEP_ROWS = 9
RWKV_STEPS_PER_TRIP = 8
N_PAIRS_D = H_D // 2


def _head_sum(x, seg):
    hi, lo = _split_bf16(x)
    return _dot(hi, seg) + _dot(lo, seg)


def _rwkv_prep_kernel(zd_ref, zb_ref, prev_s_ref, mu_ref, w0_ref, w2_ref, a0_ref, a2_ref, g2_ref,
                      kk_ref, ka_ref, rk_ref, seg_ref, rows_ref, v_ref, gb_ref,
                      *, tiles_per_seq, n_prompt_tiles):
    i = pl.program_id(0)
    z = zd_ref[...]
    tm = z.shape[0]
    boundary = jnp.where(i % tiles_per_seq == 0, 0.0, 1.0) * zb_ref[SUBLANES - 1:SUBLANES, :]
    row = lax.broadcasted_iota(jnp.int32, (tm, 1), 0)
    prev = jnp.where(row == 0, boundary, pltpu.roll(z, 1, axis=0))
    prev = jnp.where(i >= n_prompt_tiles, prev_s_ref[...], prev)
    zs = z + (prev - z) * mu_ref[...]
    r, k, v = zs[:, 0:DM_D], zs[:, DM_D:2 * DM_D], zs[:, 2 * DM_D:3 * DM_D]
    u = zs[:, 3 * DM_D:3 * DM_D + R_W + R_A]
    g_in = zs[:, 3 * DM_D + R_W + R_A:]
    seg = seg_ref[...]
    w = w0_ref[...] + _dot(jnp.tanh(u).astype(BF16), w2_ref[...])
    decay = jnp.exp(-jnp.exp(_log_sigmoid(w) - 0.5))
    a = jax.nn.sigmoid(a0_ref[...] + _dot(u.astype(BF16), a2_ref[...]))
    g = _dot(jax.nn.sigmoid(g_in).astype(BF16), g2_ref[...])
    kk = k * kk_ref[...]
    kk = kk * lax.rsqrt(jnp.maximum(_head_sum(kk * kk, seg), 1e-24))
    kp = k * (1.0 + (a - 1.0) * ka_ref[...])
    beta = kk * a
    wr = decay * r
    odd = (row % 2) == 1
    w_a, beta_a, kp_a = (pltpu.roll(t, 1, axis=0) for t in (decay, beta, kp))
    outs = (jnp.where(odd, w_a * kk, kk), jnp.where(odd, w_a * wr, wr),
            w_a * decay, beta_a * decay, kp_a * decay, beta, kp,
            _head_sum(beta_a * kk, seg), _head_sum(kp_a * kk, seg),
            jnp.where(odd, _head_sum(beta_a * wr, seg), 0.0),
            jnp.where(odd, _head_sum(kp_a * wr, seg), 0.0),
            _head_sum(beta * r, seg), _head_sum(kp * r, seg))
    for n, val in enumerate(outs):
        rows_ref[:, n * DM_D:(n + 1) * DM_D] = val
    v_ref[...] = v
    gb_ref[:, 0:DM_D] = g
    gb_ref[:, DM_D:] = _head_sum(r * kp * rk_ref[...], seg) * v


def rwkv_prep(z, prev_s, mu, w0, w2, a0, a2, g2, k_k, k_a, r_k, seg, *, n_prompt_tok, t_prompt):
    nt = z.shape[0]
    tm = TOKEN_TILE
    vec = _resident((1, DM_D))
    tok = lambda n: pl.BlockSpec((tm, n), lambda i: (i, 0))
    return pl.pallas_call(
        functools.partial(_rwkv_prep_kernel, tiles_per_seq=t_prompt // tm,
                          n_prompt_tiles=n_prompt_tok // tm),
        grid=(nt // tm,),
        in_specs=[tok(N_COLS_RWKV),
                  pl.BlockSpec((SUBLANES, N_COLS_RWKV),
                               lambda i: (jnp.maximum(i * (tm // SUBLANES) - 1, 0), 0)),
                  _resident(prev_s.shape), _resident((1, N_COLS_RWKV)),
                  vec, _resident(w2.shape), vec, _resident(a2.shape), _resident(g2.shape),
                  vec, vec, vec, _resident(seg.shape)],
        out_specs=[tok(N_SCAN_ROWS * DM_D), tok(DM_D), tok(2 * DM_D)],
        out_shape=[jax.ShapeDtypeStruct((nt, N_SCAN_ROWS * DM_D), F32),
                   jax.ShapeDtypeStruct((nt, DM_D), F32),
                   jax.ShapeDtypeStruct((nt, 2 * DM_D), F32)],
        compiler_params=_params("parallel"),
        name="rwkv_prep",
    )(z, z, prev_s, mu, w0, w2, a0, a2, g2, k_k, k_a, r_k, seg)


def _rwkv_scan_kernel(*refs, nb, t_c, rows_per_seq):
    n_in = nb if rows_per_seq else 1
    rows_refs, v_refs = refs[:n_in], refs[n_in:2 * n_in]
    s0_ref, o_ref, sout_ref, s_scr, vt_scr, sk_scr, pr_scr = refs[2 * n_in:]
    ci = pl.program_id(1)

    @pl.when(ci == 0)
    def _():
        s_scr[...] = s0_ref[...]

    def seq_rows(ref_list, kq, start, size):
        if rows_per_seq:
            return ref_list[kq][pl.ds(start, size), :]
        return ref_list[0][pl.ds(kq * t_c + start, size), :]

    lane = lax.broadcasted_iota(jnp.int32, (1, LANES), 1)
    head_a = lane < DH_D
    ones_bd = ((lax.broadcasted_iota(jnp.int32, (LANES, LANES), 0) // DH_D)
               == (lax.broadcasted_iota(jnp.int32, (LANES, LANES), 1) // DH_D)).astype(BF16)

    spb = 1 if rows_per_seq else nb
    n_sel = spb * t_c
    sel = ((lax.broadcasted_iota(jnp.int32, (n_sel, spb * LANES), 0) // t_c
            == lax.broadcasted_iota(jnp.int32, (n_sel, spb * LANES), 1) // LANES)
           & (lax.broadcasted_iota(jnp.int32, (n_sel, spb * LANES), 0) % t_c
              == lax.broadcasted_iota(jnp.int32, (n_sel, spb * LANES), 1) % DH_D)).astype(BF16)
    for blk_i in range(n_in):
        for p in range(N_PAIRS_D):
            vp = v_refs[blk_i][:, p * LANES:(p + 1) * LANES].astype(BF16)
            vt = lax.dot_general(vp, sel, (((0,), (0,)), ((), ())), preferred_element_type=F32)
            for q in range(spb):
                blk = vt[:, q * LANES:(q + 1) * LANES]
                vt_scr[blk_i * spb + q, p] = jnp.where(head_a, blk[:DH_D], blk[DH_D:]).astype(BF16)
    sk_scr[...] = jnp.zeros(sk_scr.shape, F32)
    pr_scr[...] = jnp.zeros(pr_scr.shape, F32)

    def body(i, carry):
        ta = 2 * i
        here_a = (lane % DH_D) == ta
        here_b = (lane % DH_D) == ta + 1
        ha16, hb16 = here_a.astype(BF16), here_b.astype(BF16)
        for kq in range(nb):
            ra = seq_rows(rows_refs, kq, ta, 1)
            rb = seq_rows(rows_refs, kq, ta + 1, 1)
            for p in range(N_PAIRS_D):
                def col(r, n):
                    return r[:, n * DM_D + p * LANES:n * DM_D + (p + 1) * LANES]
                w2, b1, k1, beta_b, kp_b, x1, x2 = (col(rb, n) for n in range(2, N_SCAN_STEP_ROWS))
                sp = s_scr[kq, p]
                s16 = sp.astype(BF16)
                vt = vt_scr[kq, p]
                lhs = jnp.concatenate(
                    [s16 * col(ra, 0).astype(BF16), s16 * col(ra, 1).astype(BF16),
                     s16 * col(rb, 0).astype(BF16), s16 * col(rb, 1).astype(BF16),
                     vt * ha16, vt * hb16], axis=0)
                red = _dot(lhs, ones_bd)
                skk_a, pr_a, r3, pr_b, v_a, v_b = (red[n * DH_D:(n + 1) * DH_D] for n in range(6))
                skk_b = r3 - skk_a * x1 + v_a * x2
                s_scr[kq, p] = sp * w2 - skk_a * b1 + v_a * k1 - skk_b * beta_b + v_b * kp_b
                sk_scr[kq, p] = jnp.where(here_a, skk_a, jnp.where(here_b, skk_b, sk_scr[kq, p]))
                pr_scr[kq, p] = jnp.where(here_a, pr_a, jnp.where(here_b, pr_b, pr_scr[kq, p]))
        return carry

    steps_per_trip = min(RWKV_STEPS_PER_TRIP, t_c // 2)

    def body_unrolled(j, carry):
        for u in range(steps_per_trip):
            body(steps_per_trip * j + u, carry)
        return carry

    lax.fori_loop(0, t_c // (2 * steps_per_trip), body_unrolled, 0)

    def token_major(tile):
        x = tile.T
        return jnp.concatenate([x[0:t_c], x[DH_D:DH_D + t_c]], axis=1)

    for kq in range(nb):
        rows = seq_rows(rows_refs, kq, 0, t_c)
        v = seq_rows(v_refs, kq, 0, t_c)
        for p in range(N_PAIRS_D):
            x3, x4, c1, c2 = (rows[:, n * DM_D + p * LANES:n * DM_D + (p + 1) * LANES]
                              for n in range(N_SCAN_STEP_ROWS, N_SCAN_ROWS))
            skt = token_major(sk_scr[kq, p])
            vp = v[:, p * LANES:(p + 1) * LANES]
            o = (token_major(pr_scr[kq, p]) - skt * c1 + vp * c2
                 - jnp.roll(skt, 1, axis=0) * x3 + jnp.roll(vp, 1, axis=0) * x4)
            o_ref[kq, :, p * LANES:(p + 1) * LANES] = o

    @pl.when(ci == pl.num_programs(1) - 1)
    def _():
        sout_ref[...] = s_scr[...]


def rwkv_scan(rows, v, s0, *, tok0, t, nb):
    n_seq = s0.shape[0]
    rows_per_seq = t >= DH_D
    t_c = DH_D if rows_per_seq else t
    n_chunks = t // t_c
    if rows_per_seq:
        def tok_spec(ncol):
            return [pl.BlockSpec((t_c, ncol),
                                 lambda g, c, kq=kq: (tok0 // t_c + (g * nb + kq) * n_chunks + c, 0))
                    for kq in range(nb)]
        n_in = nb
    else:
        def tok_spec(ncol):
            return [pl.BlockSpec((nb * t_c, ncol), lambda g, c: (tok0 // (nb * t_c) + g, 0))]
        n_in = 1
    st_shape = (nb,) + s0.shape[1:]
    st_spec = pl.BlockSpec(st_shape, lambda g, c: (g, 0, 0, 0))
    return pl.pallas_call(
        functools.partial(_rwkv_scan_kernel, nb=nb, t_c=t_c, rows_per_seq=rows_per_seq),
        grid=(n_seq // nb, n_chunks),
        in_specs=tok_spec(rows.shape[1]) + tok_spec(DM_D) + [st_spec],
        out_specs=[pl.BlockSpec((nb, t_c, DM_D), lambda g, c: (g, c, 0)), st_spec],
        out_shape=[jax.ShapeDtypeStruct((n_seq, t, DM_D), F32), jax.ShapeDtypeStruct(s0.shape, F32)],
        scratch_shapes=[pltpu.VMEM(st_shape, F32), pltpu.VMEM(st_shape, BF16),
                        pltpu.VMEM(st_shape, F32), pltpu.VMEM(st_shape, F32)],
        compiler_params=_params("parallel", "arbitrary"),
        name="rwkv_scan",
    )(*([rows] * n_in), *([v] * n_in), s0)


def _odd_out_kernel(x_ref, oc_ref, od_ref, gb_ref, lnw_ref, lnb_ref, seg_ref, w_ref, o_ref):
    seg = seg_ref[...]
    o = od_ref[...]
    d = o - _head_sum(o, seg) * (1.0 / DH_D)
    var = _head_sum(d * d, seg) * (1.0 / DH_D)
    od = d * lax.rsqrt(var + RWKV_LN_EPS) * lnw_ref[...] + lnb_ref[...]
    od = (od + gb_ref[:, DM_D:]) * gb_ref[:, 0:DM_D]
    a = jnp.concatenate([oc_ref[...], od], axis=-1).astype(BF16)
    o_ref[...] = x_ref[...] + _dot(a, w_ref[...])


def odd_out(x, o_c, o_d, gb, ln_w, ln_b, seg, w_out):
    nt = x.shape[0]
    tok = lambda n: pl.BlockSpec((TOKEN_TILE, n), lambda i: (i, 0))
    vec = _resident((1, DM_D))
    return pl.pallas_call(
        _odd_out_kernel,
        grid=(nt // TOKEN_TILE,),
        in_specs=[tok(D_MODEL), tok(HALF), tok(DM_D), tok(2 * DM_D), vec, vec,
                  _resident(seg.shape), _resident(w_out.shape)],
        out_specs=tok(D_MODEL),
        out_shape=jax.ShapeDtypeStruct(x.shape, F32),
        compiler_params=_params("parallel"),
        name="odd_out",
    )(x, o_c, o_d, gb, ln_w, ln_b, seg, w_out)


def _state_to_pairs(s):
    n_seq = s.shape[0]
    s = s.reshape(n_seq, N_PAIRS_D, 2, DH_D, DH_D)
    return jnp.transpose(s, (0, 1, 3, 2, 4)).reshape(n_seq, N_PAIRS_D, DH_D, LANES)


def _state_from_pairs(s):
    n_seq = s.shape[0]
    s = s.reshape(n_seq, N_PAIRS_D, DH_D, 2, DH_D)
    return jnp.transpose(s, (0, 1, 3, 2, 4)).reshape(n_seq, H_D, DH_D, DH_D)


def run_rwkv(z, n_p, t_p, n_s, t_s, state, shift, mu, w0, w2, a0, a2, g2, k_k, k_a, r_k, seg,
             *, nb_sample):
    np_tok = n_p * t_p
    zd_s = z[np_tok:, :N_COLS_RWKV].reshape(n_s, t_s, N_COLS_RWKV)
    prev_s = jnp.concatenate([shift[:, None, :], zd_s[:, :-1]], axis=1).reshape(n_s * t_s, N_COLS_RWKV)
    w2p = jnp.pad(w2, ((0, R_A), (0, 0))).astype(BF16)
    a2p = jnp.pad(a2, ((R_W, 0), (0, 0))).astype(BF16)
    row = lambda a: a.reshape(1, -1)
    rows, v, gb = rwkv_prep(z, prev_s, row(mu), row(w0), w2p, row(a0), a2p, g2.astype(BF16),
                            row(k_k), row(k_a), row(r_k), seg, n_prompt_tok=np_tok, t_prompt=t_p)
    zero = jnp.zeros((n_p, N_PAIRS_D, DH_D, LANES), F32)
    o_p, st_p = rwkv_scan(rows, v, zero, tok0=0, t=t_p, nb=n_p)
    o_s, st_s = rwkv_scan(rows, v, _state_to_pairs(state), tok0=np_tok, t=t_s, nb=nb_sample)
    o = jnp.concatenate([o_p.reshape(np_tok, DM_D), o_s.reshape(n_s * t_s, DM_D)], axis=0)
    return o, gb, _state_from_pairs(st_p), _state_from_pairs(st_s)


EVEN_CHUNK = 64
SB_BLOCK = 256
SB_GROUP = 4
RWKV_SAMPLE_GROUP = 8


def kernel(x_prompt, x_sample, p_prompt, p_sample, cache_k, cache_v, page_table, state_gla, state_hgrn, state_rwkv, state_rwkv_shift, ln_ffn1, ffn1_w_gate, ffn1_w_up, ffn1_w_down, ln_mix, ln_ffn2, ffn2_w_gate, ffn2_w_up, ffn2_w_down, ln_ple, ple_w_gate, ple_w_proj, w_in_even, w_out_even, gla_w_gk, gla_b_gk, gla_norm, hgrn_lb_logits, hgrn_norm, w_in_odd, w_out_odd, sb_bias, rwkv_mu, rwkv_w0, rwkv_w2, rwkv_a0, rwkv_a2, rwkv_g2, rwkv_k_k, rwkv_k_a, rwkv_r_k, rwkv_ln_w, rwkv_ln_b, final_norm):
    n_p, t_p, _ = x_prompt.shape
    n_s, t_s, _ = x_sample.shape
    np_tok, ns_tok = n_p * t_p, n_s * t_s
    assert ns_tok == TOKEN_TILE and t_p % TOKEN_TILE == 0 and t_s < SUBLANES
    bf = lambda w: w.astype(BF16)
    row = lambda a: a.reshape(1, -1)

    x = jnp.concatenate([x_prompt.reshape(np_tok, D_MODEL), x_sample.reshape(ns_tok, D_MODEL)], axis=0)
    p_all = jnp.concatenate([p_prompt.reshape(DEPTH, np_tok, PLE_DIM),
                             p_sample.reshape(DEPTH, ns_tok, PLE_DIM)], axis=1)
    sm = jax.nn.softmax(hgrn_lb_logits.astype(F32), axis=0)
    lower_bounds = jnp.concatenate([jnp.zeros_like(sm[:1]), jnp.cumsum(sm[1:], axis=0)], axis=0)
    head_id = jnp.arange(DM_D) // DH_D
    seg = (head_id[:, None] == head_id[None, :]).astype(BF16)
    n_phys = cache_k.shape[1]
    ck = jnp.transpose(cache_k, (0, 1, 3, 4, 2)).reshape(cache_k.shape[0], n_phys, HALF, PAGE_SIZE)
    cv = jnp.transpose(cache_v, (0, 1, 3, 4, 2)).reshape(cache_v.shape[0], n_phys, HALF, PAGE_SIZE)
    pt_flat = page_table.reshape(-1)
    q0 = N_COLS_RWKV
    k0, v0 = q0 + HALF, q0 + 2 * HALF

    k_rows_s, v_rows_s = [], []
    n_odd = cache_k.shape[0]
    kt_p, vt_p = jnp.zeros((n_odd, n_p, HALF, t_p), F32), jnp.zeros((n_odd, n_p, HALF, t_p), F32)

    def rows_from_feature_major(a):
        return jnp.transpose(a.reshape(n_odd, n_p, H_C, DH_C, t_p), (0, 1, 4, 2, 3))
    gla_p, hgrn_p = [], []
    gla_s, hgrn_s = jnp.zeros(state_gla.shape, F32), jnp.zeros(state_hgrn.shape, F32)
    rwkv_p, rwkv_s, shift_p, shift_s = [], [], [], []
    for i in range(DEPTH):
        j = i // 2
        x = ffn(x, row(ln_ffn1[i]), bf(ffn1_w_gate[i]), bf(ffn1_w_up[i]), bf(ffn1_w_down[i]))
        if i % 2 == 0:
            w_in, wgk, bgk = even_weights(w_in_even[j], gla_w_gk[j], gla_b_gk[j])
            z = rms_matmul(x, row(ln_mix[i]), w_in)
            o, (ga_p, hb_p), gla_s, hgrn_s = run_even_mixer(
                z, n_p, t_p, n_s, t_s, state_gla, state_hgrn, j, gla_s, hgrn_s, wgk, bgk,
                lower_bounds[j], gla_norm[j], hgrn_norm[j], chunk=EVEN_CHUNK)
            x = matmul_residual(x, o, bf(w_out_even[j]))
            gla_p.append(ga_p)
            hgrn_p.append(hb_p)
        else:
            w = w_in_odd[j]
            z, qkv, kt_p, vt_p = odd_in_proj(
                x, row(ln_mix[i]), bf(jnp.concatenate([w[:, 3 * HALF:], w[:, :3 * HALF]], axis=1)),
                kt_p, vt_p, j)
            oc_p = sb_attention_prompt(qkv, sb_bias[j], n_seq=n_p, t=t_p, q_col=0, blk=SB_BLOCK,
                                       group=SB_GROUP)
            z_s = z[np_tok:]
            q_s, kc_s, vc_s = (z_s[:, c:c + HALF].reshape(n_s, t_s, HALF) for c in (q0, k0, v0))
            bias_rows = jnp.broadcast_to(jnp.tile(sb_bias[j], t_s)[:, None], (t_s * H_C, PAGE_SIZE))
            oc_s = sb_attention_sample(q_s, kc_s, vc_s, ck, cv, j, pt_flat, bias_rows)
            o_c = jnp.concatenate([oc_p, oc_s.reshape(ns_tok, HALF)], axis=0)
            o_d, gb, sd_p, sd_s = run_rwkv(
                z, n_p, t_p, n_s, t_s, state_rwkv[j], state_rwkv_shift[j], rwkv_mu[j], rwkv_w0[j],
                rwkv_w2[j], rwkv_a0[j], rwkv_a2[j], rwkv_g2[j], rwkv_k_k[j], rwkv_k_a[j],
                rwkv_r_k[j], seg, nb_sample=RWKV_SAMPLE_GROUP)
            x = odd_out(x, o_c, o_d, gb, row(rwkv_ln_w[j]), row(rwkv_ln_b[j]), seg, bf(w_out_odd[j]))
            k_rows_s.append(kc_s.reshape(n_s, t_s, H_C, DH_C))
            v_rows_s.append(vc_s.reshape(n_s, t_s, H_C, DH_C))
            rwkv_p.append(sd_p)
            rwkv_s.append(sd_s)
            shift_p.append(jnp.concatenate(
                [lax.slice(z, ((b + 1) * t_p - 1, 0), ((b + 1) * t_p, N_COLS_RWKV)) for b in range(n_p)],
                axis=0))
            shift_s.append(z_s.reshape(n_s, t_s, -1)[:, -1, :N_COLS_RWKV])
        x = ffn(x, row(ln_ffn2[i]), bf(ffn2_w_gate[i]), bf(ffn2_w_up[i]), bf(ffn2_w_down[i]))
        x = ple(x, p_all[i], row(ln_ple[i]), bf(ple_w_gate[i]), bf(ple_w_proj[i]))

    y = final_rmsnorm(x, row(final_norm))
    return (y[:np_tok].reshape(n_p, t_p, D_MODEL), y[np_tok:].reshape(n_s, t_s, D_MODEL),
            rows_from_feature_major(kt_p), rows_from_feature_major(vt_p),
            jnp.stack(k_rows_s), jnp.stack(v_rows_s),
            jnp.stack(gla_p), gla_s, jnp.stack(hgrn_p), hgrn_s,
            jnp.stack(rwkv_p), jnp.stack(rwkv_s), jnp.stack(shift_p), jnp.stack(shift_s))
```

```python
import functools

import jax
import jax.numpy as jnp
from jax import lax
from jax.experimental import pallas as pl
from jax.experimental.pallas import tpu as pltpu

F32 = jnp.float32
BF16 = jnp.bfloat16

D_MODEL = 1024
DEPTH = 4
PAGE_SIZE = 128
HALF = D_MODEL // 2
D_FF = 2816
PLE_DIM = 256
RMS_EPS = 1e-6
H_A = 4
DK_A = HALF // H_A // 2
DV_A = HALF // H_A
GLA_GK_RANK = 16
GLA_GK_NORM = 16.0
H_B = 4
EXP_B = 128
DV_B = HALF // H_B
H_C = 8
DH_C = HALF // H_C
H_D = 8
DH_D = HALF // H_D
DM_D = H_D * DH_D
R_W = 64
R_A = 64
R_G = 128
RWKV_LN_EPS = 64e-5
N_COLS_RWKV = 3 * DM_D + R_W + R_A + R_G
N_COLS_ODD = 3 * HALF + N_COLS_RWKV

LANES = 128
SUBLANES = 8
MXU_DIM = 256
VMEM_LIMIT_BYTES = 56 * 1024 * 1024

TOKEN_TILE = 512
FFN_TOKEN_TILE = 768
N_EVEN_HEADS = H_A + H_B
N_COLS_EVEN_PAD = (4 * N_EVEN_HEADS + 1) * LANES
GATED_SUB_BLOCK = 16


def _params(*semantics):
    return pltpu.CompilerParams(dimension_semantics=semantics,
                                vmem_limit_bytes=VMEM_LIMIT_BYTES)


def _resident(shape):
    return pl.BlockSpec(shape, lambda *_: (0,) * len(shape), pipeline_mode=pl.Buffered(1))


def _rms(x, g):
    return x * lax.rsqrt(jnp.mean(x * x, axis=-1, keepdims=True) + RMS_EPS) * g


def _dot(a, b):
    return jnp.dot(a, b, preferred_element_type=F32)


def _ffn_kernel(x_ref, ln_ref, wg_ref, wu_ref, wd_ref, o_ref):
    x = x_ref[...]
    h = _rms(x, ln_ref[...]).astype(BF16)
    acc = jnp.zeros(x.shape, F32)
    for f0 in range(0, D_FF, MXU_DIM):
        g = _dot(h, wg_ref[:, f0:f0 + MXU_DIM])
        u = _dot(h, wu_ref[:, f0:f0 + MXU_DIM])
        a = (g * jax.nn.sigmoid(g) * u).astype(BF16)
        acc = acc + _dot(a, wd_ref[f0:f0 + MXU_DIM, :])
    o_ref[...] = x + 0.5 * acc


def ffn(x, ln, wg, wu, wd):
    nt = x.shape[0]
    tile = FFN_TOKEN_TILE if nt % FFN_TOKEN_TILE == 0 else TOKEN_TILE
    tok = pl.BlockSpec((tile, D_MODEL), lambda i: (i, 0))
    return pl.pallas_call(
        _ffn_kernel,
        grid=(nt // tile,),
        in_specs=[tok, _resident((1, D_MODEL)), _resident((D_MODEL, D_FF)),
                  _resident((D_MODEL, D_FF)), _resident((D_FF, D_MODEL))],
        out_specs=tok,
        out_shape=jax.ShapeDtypeStruct(x.shape, F32),
        compiler_params=_params("parallel"),
        name="ffn",
    )(x, ln, wg, wu, wd)


def _rms_matmul_kernel(x_ref, ln_ref, w_ref, o_ref, *maybe_ob_ref, n_chunk, bf16_from):
    h = _rms(x_ref[...], ln_ref[...]).astype(BF16)
    n = w_ref.shape[1]
    for n0 in range(0, n, n_chunk):
        n1 = min(n0 + n_chunk, n)
        o_ref[:, n0:n1] = _dot(h, w_ref[:, n0:n1])
    if maybe_ob_ref:
        maybe_ob_ref[0][...] = o_ref[:, bf16_from:].astype(BF16)


def rms_matmul(x, ln, w, bf16_from=None):
    nt, n = x.shape[0], w.shape[1]
    out_specs = [pl.BlockSpec((TOKEN_TILE, n), lambda i: (i, 0))]
    out_shape = [jax.ShapeDtypeStruct((nt, n), F32)]
    if bf16_from is not None:
        out_specs.append(pl.BlockSpec((TOKEN_TILE, n - bf16_from), lambda i: (i, 0)))
        out_shape.append(jax.ShapeDtypeStruct((nt, n - bf16_from), BF16))
    out = pl.pallas_call(
        functools.partial(_rms_matmul_kernel, n_chunk=2 * MXU_DIM, bf16_from=bf16_from),
        grid=(nt // TOKEN_TILE,),
        in_specs=[pl.BlockSpec((TOKEN_TILE, D_MODEL), lambda i: (i, 0)),
                  _resident((1, D_MODEL)), _resident((D_MODEL, n))],
        out_specs=out_specs,
        out_shape=out_shape,
        compiler_params=_params("parallel"),
        name="rms_matmul",
    )(x, ln, w)
    return out[0] if bf16_from is None else out


def _odd_in_proj_kernel(x_ref, ln_ref, w_ref, *refs, n_chunk, q0, n_feature_major_tiles, n_aliased):
    o_ref, ob_ref, kt_ref, vt_ref = refs[n_aliased:]
    h = _rms(x_ref[...], ln_ref[...]).astype(BF16)
    n = w_ref.shape[1]
    for n0 in range(0, n, n_chunk):
        n1 = min(n0 + n_chunk, n)
        o_ref[:, n0:n1] = _dot(h, w_ref[:, n0:n1])
    ob_ref[...] = o_ref[:, q0:].astype(BF16)

    @pl.when(pl.program_id(0) < n_feature_major_tiles)
    def _():
        kt_ref[...] = o_ref[:, q0 + HALF:q0 + 2 * HALF].T
        vt_ref[...] = o_ref[:, q0 + 2 * HALF:q0 + 3 * HALF].T


def odd_in_proj(x, ln, w, bufs, buf_shape, layer):
    nt, n = x.shape[0], w.shape[1]
    q0 = n - 3 * HALF
    _, n_seq, _, t = buf_shape
    tiles_per_seq = t // TOKEN_TILE
    n_fm = n_seq * tiles_per_seq
    tok = lambda ncol: pl.BlockSpec((TOKEN_TILE, ncol), lambda i: (i, 0))

    def fm_index(i):
        i = jnp.minimum(i, n_fm - 1)
        return layer, i // tiles_per_seq, 0, i % tiles_per_seq

    fm = pl.BlockSpec((None, None, HALF, TOKEN_TILE), fm_index)
    any_spec = pl.BlockSpec(memory_space=pl.ANY)
    return pl.pallas_call(
        functools.partial(_odd_in_proj_kernel, n_chunk=2 * MXU_DIM, q0=q0,
                          n_feature_major_tiles=n_fm, n_aliased=len(bufs)),
        grid=(nt // TOKEN_TILE,),
        in_specs=[tok(D_MODEL), _resident((1, D_MODEL)), _resident((D_MODEL, n))]
        + [any_spec] * len(bufs),
        out_specs=[tok(n), tok(n - q0), fm, fm],
        out_shape=[jax.ShapeDtypeStruct((nt, n), F32), jax.ShapeDtypeStruct((nt, n - q0), BF16),
                   jax.ShapeDtypeStruct(buf_shape, F32), jax.ShapeDtypeStruct(buf_shape, F32)],
        input_output_aliases={3 + k: 2 + k for k in range(len(bufs))},
        compiler_params=_params("arbitrary"),
        name="odd_in_proj",
    )(x, ln, w, *bufs)


def _matmul_residual_kernel(x_ref, a_ref, w_ref, o_ref):
    o_ref[...] = x_ref[...] + _dot(a_ref[...].astype(BF16), w_ref[...])


def matmul_residual(x, a, w):
    nt, k = a.shape
    return pl.pallas_call(
        _matmul_residual_kernel,
        grid=(nt // TOKEN_TILE,),
        in_specs=[pl.BlockSpec((TOKEN_TILE, D_MODEL), lambda i: (i, 0)),
                  pl.BlockSpec((TOKEN_TILE, k), lambda i: (i, 0)),
                  _resident((k, D_MODEL))],
        out_specs=pl.BlockSpec((TOKEN_TILE, D_MODEL), lambda i: (i, 0)),
        out_shape=jax.ShapeDtypeStruct(x.shape, F32),
        compiler_params=_params("parallel"),
        name="matmul_residual",
    )(x, a, w)


def _ple_kernel(x_ref, p_ref, ln_ref, wg_ref, wp_ref, o_ref):
    x = x_ref[...]
    h = _rms(x, ln_ref[...]).astype(BF16)
    gate = jax.nn.sigmoid(_dot(h, wg_ref[...]))
    o_ref[...] = x + gate * _dot(p_ref[...].astype(BF16), wp_ref[...])


def ple(x, p, ln, wg, wp):
    nt = x.shape[0]
    tok = pl.BlockSpec((TOKEN_TILE, D_MODEL), lambda i: (i, 0))
    return pl.pallas_call(
        _ple_kernel,
        grid=(nt // TOKEN_TILE,),
        in_specs=[tok, pl.BlockSpec((TOKEN_TILE, PLE_DIM), lambda i: (i, 0)),
                  _resident((1, D_MODEL)), _resident((D_MODEL, D_MODEL)),
                  _resident((PLE_DIM, D_MODEL))],
        out_specs=tok,
        out_shape=jax.ShapeDtypeStruct(x.shape, F32),
        compiler_params=_params("parallel"),
        name="ple",
    )(x, p, ln, wg, wp)


def _final_norm_kernel(x_ref, ln_ref, o_ref):
    o_ref[...] = _rms(x_ref[...], ln_ref[...])


def final_rmsnorm(x, ln):
    nt = x.shape[0]
    tok = pl.BlockSpec((TOKEN_TILE, D_MODEL), lambda i: (i, 0))
    return pl.pallas_call(
        _final_norm_kernel,
        grid=(nt // TOKEN_TILE,),
        in_specs=[tok, _resident((1, D_MODEL))],
        out_specs=tok,
        out_shape=jax.ShapeDtypeStruct(x.shape, F32),
        compiler_params=_params("parallel"),
        name="final_norm",
    )(x, ln)


def _log_sigmoid(x):
    return jnp.minimum(x, 0.0) - jnp.log1p(jnp.exp(-jnp.abs(x)))


def _split_bf16(x):
    hi = x.astype(BF16)
    lo = (x - hi.astype(F32)).astype(BF16)
    return hi, lo


def _gated_chunks(heads, st_ref, tri):
    c = heads[0][0].shape[0]
    n = len(heads)
    sub = min(c, GATED_SUB_BLOCK)
    row = lax.broadcasted_iota(jnp.int32, (sub, 1), 0)
    tn_dims = (((0,), (0,)), ((), ()))

    bs = []
    for q, k, v, g in heads:
        g_hi, g_lo = _split_bf16(g * LOG2E)
        bs.append(_dot(tri, g_hi) + _dot(tri, g_lo))
    sts = [st_ref[h] for h in range(n)]
    o_inter = [lax.dot_general((heads[h][0] * jnp.exp2(bs[h])).astype(BF16), sts[h].astype(BF16),
                               _NT_DIMS, preferred_element_type=F32) for h in range(n)]

    scores = [[None] * (c // sub) for _ in range(n)]
    for h, (q, k, v, g) in enumerate(heads):
        b = bs[h]
        for i, r0 in enumerate(range(sub, c, sub), start=1):
            b_ref = b[r0:r0 + 1, :]
            q_rel = (q[r0:r0 + sub, :] * jnp.exp2(b[r0:r0 + sub, :] - b_ref)).astype(BF16)
            k_rel = (k[:r0, :] * jnp.exp2(b_ref - b[:r0, :])).astype(BF16)
            scores[h][i] = lax.dot_general(q_rel, k_rel, _NT_DIMS, preferred_element_type=F32)
    off_diag = [[None] * (c // sub) for _ in range(n)]
    for h, (q, k, v, g) in enumerate(heads):
        for i, r0 in enumerate(range(sub, c, sub), start=1):
            off_diag[h][i] = _dot(scores[h][i].astype(BF16), v[:r0, :].astype(BF16))

    for h, (q, k, v, g) in enumerate(heads):
        b_last = bs[h][c - 1:c, :]
        ke = (k * jnp.exp2(b_last - bs[h])).astype(BF16)
        st_ref[h] = sts[h] * jnp.exp2(b_last) + lax.dot_general(
            v.astype(BF16), ke, tn_dims, preferred_element_type=F32)

    outs = []
    for h, (q, k, v, g) in enumerate(heads):
        parts = []
        for i, r0 in enumerate(range(0, c, sub)):
            ki, vi, bi = (a[r0:r0 + sub, :] for a in (k, v, bs[h]))
            oi = o_inter[h][r0:r0 + sub, :]
            if i > 0:
                oi = oi + off_diag[h][i]
            for p0 in range(0, sub, SUBLANES):
                qp = q[r0 + p0:r0 + p0 + SUBLANES, :]
                bp = bi[p0:p0 + SUBLANES, :]
                op = oi[p0:p0 + SUBLANES, :]
                for s in range(min(sub, p0 + SUBLANES)):
                    d = bp - bi[s:s + 1, :]
                    if s > p0:
                        d = jnp.where(row[p0:p0 + SUBLANES] >= s, d, -1e30)
                    a_s = jnp.sum(qp * ki[s:s + 1, :] * jnp.exp2(d), axis=-1, keepdims=True)
                    op = op + a_s * vi[s:s + 1, :]
                parts.append(op)
        outs.append(jnp.concatenate(parts, axis=0) if len(parts) > 1 else parts[0])
    return outs


def _even_heads(z_ref, wgk_ref, bgk_ref, lb_ref, na_ref, nb_ref):
    def blk(i):
        return z_ref[:, i * LANES:(i + 1) * LANES]

    heads, gains, gates = [], [], []
    gk = _log_sigmoid(_dot(blk(4 * N_EVEN_HEADS).astype(BF16), wgk_ref[...]) + bgk_ref[...])
    gk = gk * (1.0 / GLA_GK_NORM)
    for h in range(H_A):
        heads.append((blk(h) * DK_A ** -0.5, blk(H_A + h), blk(2 * H_A + h),
                      gk[:, h * LANES:(h + 1) * LANES]))
        gains.append(na_ref[...])
        gates.append(blk(3 * H_A + h))
    base = 4 * H_A
    for h in range(H_B):
        lb = lb_ref[:, h * LANES:(h + 1) * LANES]
        fb = blk(base + H_B + h)
        a = jnp.log1p(-lb) + _log_sigmoid(fb)
        log_lb = jnp.log(lb)
        g = jnp.maximum(a, log_lb) + jnp.log1p(jnp.exp(-jnp.abs(a - log_lb)))
        heads.append((blk(base + h), (1.0 - lb) * jax.nn.sigmoid(-fb), blk(base + 2 * H_B + h), g))
        gains.append(nb_ref[...])
        gates.append(blk(base + 3 * H_B + h))
    return heads, gains, gates


def _even_finish(o, gain, gate):
    o = o * lax.rsqrt(jnp.mean(o * o, axis=-1, keepdims=True) + RMS_EPS) * gain
    return o * (gate * jax.nn.sigmoid(gate))


def _lower_tri(c):
    return (lax.broadcasted_iota(jnp.int32, (c, c), 0)
            >= lax.broadcasted_iota(jnp.int32, (c, c), 1)).astype(BF16)


def _even_mixer_kernel(z_ref, s0_ref, wgk_ref, bgk_ref, lb_ref, na_ref, nb_ref,
                       o_ref, sout_ref, st_ref):
    ci = pl.program_id(1)

    @pl.when(ci == 0)
    def _():
        st_ref[...] = s0_ref[0]

    heads, gains, gates = _even_heads(z_ref, wgk_ref, bgk_ref, lb_ref, na_ref, nb_ref)
    for h, o in enumerate(_gated_chunks(heads, st_ref, _lower_tri(z_ref.shape[0]))):
        o_ref[:, h * LANES:(h + 1) * LANES] = _even_finish(o, gains[h], gates[h])

    @pl.when(ci == pl.num_programs(1) - 1)
    def _():
        sout_ref[0] = st_ref[...]


def _even_mixer_short_kernel(z_ref, sg_ref, sh_ref, wgk_ref, bgk_ref, lb_ref, na_ref, nb_ref,
                             *refs, t_s, n_aliased):
    o_ref, sg_out_ref, sh_out_ref, st_ref = refs[n_aliased:]
    c = z_ref.shape[0]
    spb = c // t_s
    row = lax.broadcasted_iota(jnp.int32, (c, 1), 0)
    zeros = jnp.zeros((LANES, LANES - DK_A), F32)
    for q in range(spb):
        for h in range(H_A):
            st_ref[q * N_EVEN_HEADS + h] = jnp.concatenate([sg_ref[q, h].T, zeros], axis=1)
        for h in range(H_B):
            st_ref[q * N_EVEN_HEADS + H_A + h] = sh_ref[q, h].T

    heads, gains, gates = _even_heads(z_ref, wgk_ref, bgk_ref, lb_ref, na_ref, nb_ref)
    all_heads = []
    for q in range(spb):
        mine = (row // t_s) == q
        all_heads += [(hq, jnp.where(mine, hk, 0.0), hv, jnp.where(mine, hg, 0.0))
                      for hq, hk, hv, hg in heads]
    outs = _gated_chunks(all_heads, st_ref, _lower_tri(c))
    for h in range(N_EVEN_HEADS):
        o = outs[h]
        for q in range(1, spb):
            o = jnp.where((row // t_s) == q, outs[q * N_EVEN_HEADS + h], o)
        o_ref[:, h * LANES:(h + 1) * LANES] = _even_finish(o, gains[h], gates[h])

    for q in range(spb):
        for h in range(H_A):
            sg_out_ref[q, h] = st_ref[q * N_EVEN_HEADS + h].T[:DK_A, :]
        for h in range(H_B):
            sh_out_ref[q, h] = st_ref[q * N_EVEN_HEADS + H_A + h].T


def even_mixer(z, s0, wgk, bgk, lb, norm_a, norm_b, *, n_seq, t, chunk):
    nt, ncol = z.shape
    n_chunks = t // chunk
    st_shape = (N_EVEN_HEADS, LANES, LANES)
    return pl.pallas_call(
        _even_mixer_kernel,
        grid=(n_seq, n_chunks),
        in_specs=[pl.BlockSpec((chunk, ncol), lambda b, c: (b * n_chunks + c, 0)),
                  pl.BlockSpec((1,) + st_shape, lambda b, c: (b, 0, 0, 0)),
                  _resident(wgk.shape), _resident(bgk.shape), _resident(lb.shape),
                  _resident(norm_a.shape), _resident(norm_b.shape)],
        out_specs=[pl.BlockSpec((chunk, D_MODEL), lambda b, c: (b * n_chunks + c, 0)),
                   pl.BlockSpec((1,) + st_shape, lambda b, c: (b, 0, 0, 0))],
        out_shape=[jax.ShapeDtypeStruct((nt, D_MODEL), F32),
                   jax.ShapeDtypeStruct((n_seq,) + st_shape, F32)],
        scratch_shapes=[pltpu.VMEM(st_shape, F32)],
        compiler_params=_params("parallel", "arbitrary"),
        name="even_mixer",
    )(z, s0, wgk, bgk, lb, norm_a, norm_b)


def even_mixer_short(z, state_gla, state_hgrn, layer, wgk, bgk, lb, norm_a, norm_b,
                     o_buf, gla_buf, hgrn_buf, *, tok0, n_seq, t_s):
    ncol = z.shape[1]
    spb = SUBLANES // t_s

    def st_spec(a):
        return pl.BlockSpec((None, spb) + a.shape[2:], lambda g: (layer, g, 0, 0, 0))

    any_spec = pl.BlockSpec(memory_space=pl.ANY)
    row_blk = lambda n: pl.BlockSpec((SUBLANES, n), lambda g: (tok0 // SUBLANES + g, 0))
    bufs = [o_buf] if gla_buf is None else [o_buf, gla_buf, hgrn_buf]
    n_fixed = 8
    return pl.pallas_call(
        functools.partial(_even_mixer_short_kernel, t_s=t_s, n_aliased=len(bufs)),
        grid=(n_seq // spb,),
        in_specs=[row_blk(ncol), st_spec(state_gla), st_spec(state_hgrn),
                  _resident(wgk.shape), _resident(bgk.shape), _resident(lb.shape),
                  _resident(norm_a.shape), _resident(norm_b.shape)] + [any_spec] * len(bufs),
        out_specs=[row_blk(D_MODEL), st_spec(state_gla), st_spec(state_hgrn)],
        out_shape=[jax.ShapeDtypeStruct(o_buf.shape, F32), jax.ShapeDtypeStruct(state_gla.shape, F32),
                   jax.ShapeDtypeStruct(state_hgrn.shape, F32)],
        input_output_aliases={n_fixed + k: k for k in range(len(bufs))},
        scratch_shapes=[pltpu.VMEM((spb * N_EVEN_HEADS, LANES, LANES), F32)],
        compiler_params=_params("parallel"),
        name="even_mixer_short",
    )(z, state_gla, state_hgrn, wgk, bgk, lb, norm_a, norm_b, *bufs)


def _pad_heads(w, n_heads):
    d = w.shape[-1] // n_heads
    w = w.reshape(w.shape[:-1] + (n_heads, d))
    w = jnp.pad(w, [(0, 0)] * (w.ndim - 1) + [(0, LANES - d)])
    return w.reshape(w.shape[:-2] + (n_heads * LANES,))


def _split_cols(w, sizes):
    out, o = [], 0
    for s in sizes:
        out.append(w[..., o:o + s])
        o += s
    return out


def even_weights(w_in, w_gk, b_gk):
    ka = H_A * DK_A
    qa, kk, va, gk_in, ga, qb, fb, ib, gb = _split_cols(
        w_in, (ka, ka, HALF, GLA_GK_RANK, HALF, HALF, HALF, HALF, HALF))
    gk_in = jnp.pad(gk_in, ((0, 0), (0, LANES - GLA_GK_RANK)))
    w = jnp.concatenate([_pad_heads(qa, H_A), _pad_heads(kk, H_A), va, ga, qb, fb, ib, gb, gk_in],
                        axis=-1).astype(BF16)
    wgk = jnp.pad(_pad_heads(w_gk, H_A), ((0, LANES - GLA_GK_RANK), (0, 0))).astype(BF16)
    bgk = _pad_heads(b_gk[None, :], H_A)
    return w, wgk, bgk


def run_even_mixer(z, n_p, t_p, n_s, t_s, state_gla, state_hgrn, layer, gla_buf, hgrn_buf,
                   wgk, bgk, lb, norm_a, norm_b, *, chunk):
    args = (wgk, bgk, lb[None, :], norm_a[None, :], norm_b[None, :])
    zero = jnp.zeros((n_p, N_EVEN_HEADS, LANES, LANES), F32)
    o_buf, st_p = even_mixer(z, zero, *args, n_seq=n_p, t=t_p, chunk=chunk)
    o, gla_buf, hgrn_buf = even_mixer_short(z, state_gla, state_hgrn, layer, *args,
                                            o_buf, gla_buf, hgrn_buf,
                                            tok0=n_p * t_p, n_seq=n_s, t_s=t_s)
    gla_p = jnp.swapaxes(st_p[:, :H_A], -1, -2)[:, :, :DK_A, :]
    hgrn_p = jnp.swapaxes(st_p[:, H_A:], -1, -2)
    return o, (gla_p, hgrn_p), gla_buf, hgrn_buf


_NT_DIMS = (((1,), (1,)), ((), ()))
LOG2E = 1.4426950408889634


def _neg_softplus(z):
    return -(jnp.maximum(z, 0.0) + jnp.log1p(jnp.exp(-jnp.abs(z))))


def _suffix_tri(n):
    return (lax.broadcasted_iota(jnp.int32, (n, n), 0)
            >= lax.broadcasted_iota(jnp.int32, (n, n), 1)).astype(BF16)


def _sb_prompt_kernel(bias_ref, q_ref, k_ref, v_ref, o_ref, u_ref, tot_ref, carry_ref, acc_ref,
                      *, blk, group):
    pair = pl.program_id(1)
    qi = pl.program_id(2)
    lane = lax.broadcasted_iota(jnp.int32, (1, LANES), 1)
    head_a = lane < DH_C
    q2 = q_ref[...].astype(F32) * (DH_C ** -0.5 * LOG2E)
    qs = jnp.concatenate([jnp.where(head_a, q2, 0.0), jnp.where(head_a, 0.0, q2)], axis=0).astype(BF16)
    row = lax.broadcasted_iota(jnp.int32, (2 * blk, 1), 0)
    bias2 = jnp.where(row < blk, bias_ref[2 * pair], bias_ref[2 * pair + 1]) * LOG2E
    neg_tri = -_suffix_tri(blk)
    strict = lax.broadcasted_iota(jnp.int32, (2 * blk, blk), 1) < (row % blk)

    def rows_of(ref, j):
        return ref[pl.ds(pl.multiple_of(j * blk, blk), blk), :]

    def scores(slot, j, diagonal=False):
        b2 = jnp.where(j >= 0, bias2, -1e30)
        z2 = lax.dot_general(qs, rows_of(k_ref, jnp.maximum(j, 0)), _NT_DIMS,
                             preferred_element_type=F32) + b2
        if diagonal:
            z2 = jnp.where(strict, z2, -1e30)
        sp2 = jnp.maximum(z2, 0.0) + jnp.log2(1.0 + jnp.exp2(-jnp.abs(z2)))
        incl = _dot(sp2.astype(BF16), neg_tri)
        u_ref[slot] = z2 + incl
        tot_ref[slot] = incl[:, 0:1]

    def accumulate(slot, j):
        w = jnp.exp2(u_ref[slot] + carry_ref[...])
        acc_ref[...] += _dot(w.astype(BF16), rows_of(v_ref, jnp.maximum(j, 0)))
        carry_ref[...] += tot_ref[slot]

    def body(k, _):
        j = qi - group * k
        for s in range(group):
            accumulate(s, j + group - s)
        for s in range(group):
            scores(s, j - s)
        return 0

    carry_ref[...] = jnp.zeros(carry_ref.shape, F32)
    acc_ref[...] = jnp.zeros(acc_ref.shape, F32)
    for s in range(group):
        scores(s, qi - s, diagonal=(s == 0))
    lax.fori_loop(1, qi // group + 1, body, 0)
    for s in range(group):
        accumulate(s, qi % group - s)
    o_ref[...] = jnp.where(head_a, acc_ref[:blk, :], acc_ref[blk:, :])


def sb_attention_prompt(z, bias, *, n_seq, t, q_col, blk, group):
    nq = t // blk
    n_pairs = H_C // 2
    return pl.pallas_call(
        functools.partial(_sb_prompt_kernel, blk=blk, group=group),
        grid=(n_seq, n_pairs, nq),
        in_specs=[pl.BlockSpec(memory_space=pltpu.SMEM),
                  pl.BlockSpec((blk, LANES), lambda b, p, i: (b * nq + i, q_col + p)),
                  pl.BlockSpec((t, LANES), lambda b, p, i: (b, q_col + n_pairs + p)),
                  pl.BlockSpec((t, LANES), lambda b, p, i: (b, q_col + 2 * n_pairs + p))],
        out_specs=pl.BlockSpec((blk, LANES), lambda b, p, i: (b * nq + i, p)),
        out_shape=jax.ShapeDtypeStruct((n_seq * t, HALF), F32),
        scratch_shapes=[pltpu.VMEM((group, 2 * blk, blk), F32), pltpu.VMEM((group, 2 * blk, 1), F32),
                        pltpu.VMEM((2 * blk, 1), F32), pltpu.VMEM((2 * blk, LANES), F32)],
        compiler_params=_params("parallel", "parallel", "arbitrary"),
        name="sb_attention_prompt",
    )(bias, z, z, z)


def _sb_sample_kernel(pt_ref, q_ref, kn_ref, vn_ref, bias_ref, *refs, n_pages, t_s):
    del pt_ref
    k_pages, v_pages, o_ref = refs[:n_pages], refs[n_pages:2 * n_pages], refs[2 * n_pages]
    rows = t_s * H_C
    q = q_ref[0] * DH_C ** -0.5
    qx = jnp.concatenate([jnp.broadcast_to(q[i:i + 1, :], (H_C, HALF)) for i in range(t_s)], axis=0)
    r_id = lax.broadcasted_iota(jnp.int32, (rows, HALF), 0)
    l_id = lax.broadcasted_iota(jnp.int32, (rows, HALF), 1)
    head_lanes = (l_id // DH_C) == (r_id % H_C)
    qx = jnp.where(head_lanes, qx, 0.0)
    bias = bias_ref[...]
    q_idx = lax.broadcasted_iota(jnp.int32, (rows, 1), 0) // H_C

    carry = jnp.zeros((rows, 1), F32)
    acc = jnp.zeros((rows, HALF), F32)
    kn, vn = kn_ref[0], vn_ref[0]
    for j in reversed(range(t_s)):
        visible = q_idx > j
        z = jnp.sum(qx * kn[j:j + 1, :], axis=-1, keepdims=True) + bias[:, 0:1]
        lk = jnp.where(visible, _neg_softplus(z), 0.0)
        w = jnp.where(visible, jnp.exp(z + lk + carry), 0.0)
        acc = acc + w * vn[j:j + 1, :]
        carry = carry + lk

    qx = qx.astype(BF16)
    tri = _suffix_tri(PAGE_SIZE)
    zs = [_dot(qx, k_pages[p][...].astype(BF16)) + bias for p in range(n_pages)]
    incls = [_dot(_neg_softplus(z).astype(BF16), tri) for z in zs]
    ws = [None] * n_pages
    for p in reversed(range(n_pages)):
        ws[p] = jnp.exp(zs[p] + incls[p] + carry).astype(BF16)
        carry = carry + incls[p][:, 0:1]
    for p in range(n_pages):
        acc = acc + lax.dot_general(ws[p], v_pages[p][...].astype(BF16), _NT_DIMS,
                                    preferred_element_type=F32)

    acc = jnp.where(head_lanes, acc, 0.0)
    for i in range(t_s):
        o_ref[0, i:i + 1, :] = jnp.sum(acc[i * H_C:(i + 1) * H_C, :], axis=0, keepdims=True)


def sb_attention_sample(q, k_new, v_new, cache_k, cache_v, layer, page_table, bias_rows):
    n_seq, t_s, _ = q.shape
    n_pages = page_table.shape[0] // n_seq
    tok = pl.BlockSpec((1, t_s, HALF), lambda b, pt: (b, 0, 0))

    def page_spec(p):
        return pl.BlockSpec((None, None, HALF, PAGE_SIZE),
                            lambda b, pt: (layer, pt[b * n_pages + p], 0, 0))

    pages = [page_spec(p) for p in range(n_pages)]
    return pl.pallas_call(
        functools.partial(_sb_sample_kernel, n_pages=n_pages, t_s=t_s),
        grid_spec=pltpu.PrefetchScalarGridSpec(
            num_scalar_prefetch=1,
            grid=(n_seq,),
            in_specs=[tok, tok, tok,
                      pl.BlockSpec(bias_rows.shape, lambda b, pt: (0, 0))] + pages + pages,
            out_specs=tok),
        out_shape=jax.ShapeDtypeStruct((n_seq, t_s, HALF), F32),
        compiler_params=_params("parallel"),
        name="sb_attention_sample",
    )(page_table, q, k_new, v_new, bias_rows, *([cache_k] * n_pages), *([cache_v] * n_pages))


N_SCAN_ROWS = 13
N_SCAN_STEP_ROWS = 9
RWKV_STEPS_PER_TRIP = 8
N_PAIRS_D = H_D // 2


def _head_sum(x, seg):
    hi, lo = _split_bf16(x)
    return _dot(hi, seg) + _dot(lo, seg)


def _rwkv_prep_kernel(zd_ref, zb_ref, prev_s_ref, mu_ref, w0_ref, w2_ref, a0_ref, a2_ref, g2_ref,
                      kk_ref, ka_ref, rk_ref, seg_ref, rows_ref, v_ref, gb_ref,
                      *, tiles_per_seq, n_prompt_tiles):
    i = pl.program_id(0)
    z = zd_ref[...]
    tm = z.shape[0]
    boundary = jnp.where(i % tiles_per_seq == 0, 0.0, 1.0) * zb_ref[SUBLANES - 1:SUBLANES, :]
    row = lax.broadcasted_iota(jnp.int32, (tm, 1), 0)
    prev = jnp.where(row == 0, boundary, pltpu.roll(z, 1, axis=0))
    prev = jnp.where(i >= n_prompt_tiles, prev_s_ref[...], prev)
    zs = z + (prev - z) * mu_ref[...]
    r, k, v = zs[:, 0:DM_D], zs[:, DM_D:2 * DM_D], zs[:, 2 * DM_D:3 * DM_D]
    u = zs[:, 3 * DM_D:3 * DM_D + R_W + R_A]
    g_in = zs[:, 3 * DM_D + R_W + R_A:]
    seg = seg_ref[...]
    w = w0_ref[...] + _dot(jnp.tanh(u).astype(BF16), w2_ref[...])
    decay = jnp.exp(-jnp.exp(_log_sigmoid(w) - 0.5))
    a = jax.nn.sigmoid(a0_ref[...] + _dot(u.astype(BF16), a2_ref[...]))
    g = _dot(jax.nn.sigmoid(g_in).astype(BF16), g2_ref[...])
    kk = k * kk_ref[...]
    kk = kk * lax.rsqrt(jnp.maximum(_head_sum(kk * kk, seg), 1e-24))
    kp = k * (1.0 + (a - 1.0) * ka_ref[...])
    beta = kk * a
    wr = decay * r
    odd = (row % 2) == 1
    w_a, beta_a, kp_a = (pltpu.roll(t, 1, axis=0) for t in (decay, beta, kp))
    outs = (jnp.where(odd, w_a * kk, kk), jnp.where(odd, w_a * wr, wr),
            w_a * decay, beta_a * decay, kp_a * decay, beta, kp,
            _head_sum(beta_a * kk, seg), _head_sum(kp_a * kk, seg),
            jnp.where(odd, _head_sum(beta_a * wr, seg), 0.0),
            jnp.where(odd, _head_sum(kp_a * wr, seg), 0.0),
            _head_sum(beta * r, seg), _head_sum(kp * r, seg))
    for n, val in enumerate(outs):
        rows_ref[:, n * DM_D:(n + 1) * DM_D] = val
    v_ref[...] = v
    gb_ref[:, 0:DM_D] = g
    gb_ref[:, DM_D:] = _head_sum(r * kp * rk_ref[...], seg) * v


def rwkv_prep(z, prev_s, mu, w0, w2, a0, a2, g2, k_k, k_a, r_k, seg, *, n_prompt_tok, t_prompt):
    nt = z.shape[0]
    tm = TOKEN_TILE
    vec = _resident((1, DM_D))
    tok = lambda n: pl.BlockSpec((tm, n), lambda i: (i, 0))
    return pl.pallas_call(
        functools.partial(_rwkv_prep_kernel, tiles_per_seq=t_prompt // tm,
                          n_prompt_tiles=n_prompt_tok // tm),
        grid=(nt // tm,),
        in_specs=[tok(N_COLS_RWKV),
                  pl.BlockSpec((SUBLANES, N_COLS_RWKV),
                               lambda i: (jnp.maximum(i * (tm // SUBLANES) - 1, 0), 0)),
                  _resident(prev_s.shape), _resident((1, N_COLS_RWKV)),
                  vec, _resident(w2.shape), vec, _resident(a2.shape), _resident(g2.shape),
                  vec, vec, vec, _resident(seg.shape)],
        out_specs=[tok(N_SCAN_ROWS * DM_D), tok(DM_D), tok(2 * DM_D)],
        out_shape=[jax.ShapeDtypeStruct((nt, N_SCAN_ROWS * DM_D), F32),
                   jax.ShapeDtypeStruct((nt, DM_D), F32),
                   jax.ShapeDtypeStruct((nt, 2 * DM_D), F32)],
        compiler_params=_params("parallel"),
        name="rwkv_prep",
    )(z, z, prev_s, mu, w0, w2, a0, a2, g2, k_k, k_a, r_k, seg)


def _rwkv_scan_kernel(*refs, nb, t_c, rows_per_seq):
    n_in = nb if rows_per_seq else 1
    rows_refs, v_refs = refs[:n_in], refs[n_in:2 * n_in]
    s0_ref, o_ref, sout_ref, s_scr, vt_scr, sk_scr, pr_scr = refs[2 * n_in:]
    ci = pl.program_id(1)

    @pl.when(ci == 0)
    def _():
        s_scr[...] = s0_ref[...]

    def seq_rows(ref_list, kq, start, size):
        if rows_per_seq:
            return ref_list[kq][pl.ds(start, size), :]
        return ref_list[0][pl.ds(kq * t_c + start, size), :]

    lane = lax.broadcasted_iota(jnp.int32, (1, LANES), 1)
    head_a = lane < DH_D
    ones_bd = ((lax.broadcasted_iota(jnp.int32, (LANES, LANES), 0) // DH_D)
               == (lax.broadcasted_iota(jnp.int32, (LANES, LANES), 1) // DH_D)).astype(BF16)

    spb = 1 if rows_per_seq else nb
    n_sel = spb * t_c
    sel = ((lax.broadcasted_iota(jnp.int32, (n_sel, spb * LANES), 0) // t_c
            == lax.broadcasted_iota(jnp.int32, (n_sel, spb * LANES), 1) // LANES)
           & (lax.broadcasted_iota(jnp.int32, (n_sel, spb * LANES), 0) % t_c
              == lax.broadcasted_iota(jnp.int32, (n_sel, spb * LANES), 1) % DH_D)).astype(BF16)
    for blk_i in range(n_in):
        for p in range(N_PAIRS_D):
            vp = v_refs[blk_i][:, p * LANES:(p + 1) * LANES].astype(BF16)
            vt = lax.dot_general(vp, sel, (((0,), (0,)), ((), ())), preferred_element_type=F32)
            for q in range(spb):
                blk = vt[:, q * LANES:(q + 1) * LANES]
                vt_scr[blk_i * spb + q, p] = jnp.where(head_a, blk[:DH_D], blk[DH_D:]).astype(BF16)
    sk_scr[...] = jnp.zeros(sk_scr.shape, F32)
    pr_scr[...] = jnp.zeros(pr_scr.shape, F32)

    def body(i, carry):
        ta = 2 * i
        here_a = (lane % DH_D) == ta
        here_b = (lane % DH_D) == ta + 1
        ha16, hb16 = here_a.astype(BF16), here_b.astype(BF16)
        for kq in range(nb):
            ra = seq_rows(rows_refs, kq, ta, 1)
            rb = seq_rows(rows_refs, kq, ta + 1, 1)
            for p in range(N_PAIRS_D):
                def col(r, n):
                    return r[:, n * DM_D + p * LANES:n * DM_D + (p + 1) * LANES]
                w2, b1, k1, beta_b, kp_b, x1, x2 = (col(rb, n) for n in range(2, N_SCAN_STEP_ROWS))
                sp = s_scr[kq, p]
                s16 = sp.astype(BF16)
                vt = vt_scr[kq, p]
                lhs = jnp.concatenate(
                    [s16 * col(ra, 0).astype(BF16), s16 * col(ra, 1).astype(BF16),
                     s16 * col(rb, 0).astype(BF16), s16 * col(rb, 1).astype(BF16),
                     vt * ha16, vt * hb16], axis=0)
                red = _dot(lhs, ones_bd)
                skk_a, pr_a, r3, pr_b, v_a, v_b = (red[n * DH_D:(n + 1) * DH_D] for n in range(6))
                skk_b = r3 - skk_a * x1 + v_a * x2
                s_scr[kq, p] = sp * w2 - skk_a * b1 + v_a * k1 - skk_b * beta_b + v_b * kp_b
                sk_scr[kq, p] = jnp.where(here_a, skk_a, jnp.where(here_b, skk_b, sk_scr[kq, p]))
                pr_scr[kq, p] = jnp.where(here_a, pr_a, jnp.where(here_b, pr_b, pr_scr[kq, p]))
        return carry

    steps_per_trip = min(RWKV_STEPS_PER_TRIP, t_c // 2)

    def body_unrolled(j, carry):
        for u in range(steps_per_trip):
            body(steps_per_trip * j + u, carry)
        return carry

    lax.fori_loop(0, t_c // (2 * steps_per_trip), body_unrolled, 0)

    def token_major(tile):
        x = tile.T
        return jnp.concatenate([x[0:t_c], x[DH_D:DH_D + t_c]], axis=1)

    for kq in range(nb):
        rows = seq_rows(rows_refs, kq, 0, t_c)
        v = seq_rows(v_refs, kq, 0, t_c)
        for p in range(N_PAIRS_D):
            x3, x4, c1, c2 = (rows[:, n * DM_D + p * LANES:n * DM_D + (p + 1) * LANES]
                              for n in range(N_SCAN_STEP_ROWS, N_SCAN_ROWS))
            skt = token_major(sk_scr[kq, p])
            vp = v[:, p * LANES:(p + 1) * LANES]
            o = (token_major(pr_scr[kq, p]) - skt * c1 + vp * c2
                 - jnp.roll(skt, 1, axis=0) * x3 + jnp.roll(vp, 1, axis=0) * x4)
            o_ref[kq, :, p * LANES:(p + 1) * LANES] = o

    @pl.when(ci == pl.num_programs(1) - 1)
    def _():
        sout_ref[...] = s_scr[...]


def rwkv_scan(rows, v, s0, *, tok0, t, nb):
    n_seq = s0.shape[0]
    rows_per_seq = t >= DH_D
    t_c = DH_D if rows_per_seq else t
    n_chunks = t // t_c
    if rows_per_seq:
        def tok_spec(ncol):
            return [pl.BlockSpec((t_c, ncol),
                                 lambda g, c, kq=kq: (tok0 // t_c + (g * nb + kq) * n_chunks + c, 0))
                    for kq in range(nb)]
        n_in = nb
    else:
        def tok_spec(ncol):
            return [pl.BlockSpec((nb * t_c, ncol), lambda g, c: (tok0 // (nb * t_c) + g, 0))]
        n_in = 1
    st_shape = (nb,) + s0.shape[1:]
    st_spec = pl.BlockSpec(st_shape, lambda g, c: (g, 0, 0, 0))
    return pl.pallas_call(
        functools.partial(_rwkv_scan_kernel, nb=nb, t_c=t_c, rows_per_seq=rows_per_seq),
        grid=(n_seq // nb, n_chunks),
        in_specs=tok_spec(rows.shape[1]) + tok_spec(DM_D) + [st_spec],
        out_specs=[pl.BlockSpec((nb, t_c, DM_D), lambda g, c: (g, c, 0)), st_spec],
        out_shape=[jax.ShapeDtypeStruct((n_seq, t, DM_D), F32), jax.ShapeDtypeStruct(s0.shape, F32)],
        scratch_shapes=[pltpu.VMEM(st_shape, F32), pltpu.VMEM(st_shape, BF16),
                        pltpu.VMEM(st_shape, F32), pltpu.VMEM(st_shape, F32)],
        compiler_params=_params("parallel", "arbitrary"),
        name="rwkv_scan",
    )(*([rows] * n_in), *([v] * n_in), s0)


def _odd_out_kernel(x_ref, oc_ref, oc_tail_ref, od_ref, od_tail_ref, gb_ref, lnw_ref, lnb_ref,
                    seg_ref, w_ref, o_ref, *, n_lead_tiles):
    seg = seg_ref[...]
    tail = pl.program_id(0) >= n_lead_tiles
    o = jnp.where(tail, od_tail_ref[...], od_ref[...])
    oc = jnp.where(tail, oc_tail_ref[...], oc_ref[...])
    d = o - _head_sum(o, seg) * (1.0 / DH_D)
    var = _head_sum(d * d, seg) * (1.0 / DH_D)
    od = d * lax.rsqrt(var + RWKV_LN_EPS) * lnw_ref[...] + lnb_ref[...]
    od = (od + gb_ref[:, DM_D:]) * gb_ref[:, 0:DM_D]
    a = jnp.concatenate([oc, od], axis=-1).astype(BF16)
    o_ref[...] = x_ref[...] + _dot(a, w_ref[...])


def odd_out(x, o_c, o_c_tail, o_d, o_d_tail, gb, ln_w, ln_b, seg, w_out):
    nt = x.shape[0]
    n_lead_tiles = o_c.shape[0] // TOKEN_TILE
    assert nt == o_c.shape[0] + TOKEN_TILE and o_c_tail.shape[0] == TOKEN_TILE
    tok = lambda n: pl.BlockSpec((TOKEN_TILE, n), lambda i: (i, 0))
    lead = pl.BlockSpec((TOKEN_TILE, HALF), lambda i: (jnp.minimum(i, n_lead_tiles - 1), 0))
    tail = _resident((TOKEN_TILE, HALF))
    vec = _resident((1, DM_D))
    return pl.pallas_call(
        functools.partial(_odd_out_kernel, n_lead_tiles=n_lead_tiles),
        grid=(nt // TOKEN_TILE,),
        in_specs=[tok(D_MODEL), lead, tail, lead, tail, tok(2 * DM_D), vec, vec,
                  _resident(seg.shape), _resident(w_out.shape)],
        out_specs=tok(D_MODEL),
        out_shape=jax.ShapeDtypeStruct(x.shape, F32),
        compiler_params=_params("parallel"),
        name="odd_out",
    )(x, o_c, o_c_tail, o_d, o_d_tail, gb, ln_w, ln_b, seg, w_out)


def _state_to_pairs(s):
    n_seq = s.shape[0]
    s = s.reshape(n_seq, N_PAIRS_D, 2, DH_D, DH_D)
    return jnp.transpose(s, (0, 1, 3, 2, 4)).reshape(n_seq, N_PAIRS_D, DH_D, LANES)


def _state_from_pairs(s):
    n_seq = s.shape[0]
    s = s.reshape(n_seq, N_PAIRS_D, DH_D, 2, DH_D)
    return jnp.transpose(s, (0, 1, 3, 2, 4)).reshape(n_seq, H_D, DH_D, DH_D)


def run_rwkv(z, n_p, t_p, n_s, t_s, state, shift, mu, w0, w2, a0, a2, g2, k_k, k_a, r_k, seg,
             *, nb_sample):
    np_tok = n_p * t_p
    zd_s = z[np_tok:, :N_COLS_RWKV].reshape(n_s, t_s, N_COLS_RWKV)
    prev_s = jnp.concatenate([shift[:, None, :], zd_s[:, :-1]], axis=1).reshape(n_s * t_s, N_COLS_RWKV)
    w2p = jnp.pad(w2, ((0, R_A), (0, 0))).astype(BF16)
    a2p = jnp.pad(a2, ((R_W, 0), (0, 0))).astype(BF16)
    row = lambda a: a.reshape(1, -1)
    rows, v, gb = rwkv_prep(z, prev_s, row(mu), row(w0), w2p, row(a0), a2p, g2.astype(BF16),
                            row(k_k), row(k_a), row(r_k), seg, n_prompt_tok=np_tok, t_prompt=t_p)
    zero = jnp.zeros((n_p, N_PAIRS_D, DH_D, LANES), F32)
    o_p, st_p = rwkv_scan(rows, v, zero, tok0=0, t=t_p, nb=n_p)
    o_s, st_s = rwkv_scan(rows, v, _state_to_pairs(state), tok0=np_tok, t=t_s, nb=nb_sample)
    return (o_p.reshape(np_tok, DM_D), o_s.reshape(n_s * t_s, DM_D), gb,
            _state_from_pairs(st_p), _state_from_pairs(st_s))


EVEN_CHUNK = 64
SB_BLOCK = 256
SB_GROUP = 4
RWKV_SAMPLE_GROUP = 8


def kernel(x_prompt, x_sample, p_prompt, p_sample, cache_k, cache_v, page_table, state_gla, state_hgrn, state_rwkv, state_rwkv_shift, ln_ffn1, ffn1_w_gate, ffn1_w_up, ffn1_w_down, ln_mix, ln_ffn2, ffn2_w_gate, ffn2_w_up, ffn2_w_down, ln_ple, ple_w_gate, ple_w_proj, w_in_even, w_out_even, gla_w_gk, gla_b_gk, gla_norm, hgrn_lb_logits, hgrn_norm, w_in_odd, w_out_odd, sb_bias, rwkv_mu, rwkv_w0, rwkv_w2, rwkv_a0, rwkv_a2, rwkv_g2, rwkv_k_k, rwkv_k_a, rwkv_r_k, rwkv_ln_w, rwkv_ln_b, final_norm):
    n_p, t_p, _ = x_prompt.shape
    n_s, t_s, _ = x_sample.shape
    np_tok, ns_tok = n_p * t_p, n_s * t_s
    assert ns_tok == TOKEN_TILE and t_p % TOKEN_TILE == 0 and t_s < SUBLANES
    bf = lambda w: w.astype(BF16)
    row = lambda a: a.reshape(1, -1)

    x = jnp.concatenate([x_prompt.reshape(np_tok, D_MODEL), x_sample.reshape(ns_tok, D_MODEL)], axis=0)
    p_all = jnp.concatenate([p_prompt.reshape(DEPTH, np_tok, PLE_DIM),
                             p_sample.reshape(DEPTH, ns_tok, PLE_DIM)], axis=1)
    sm = jax.nn.softmax(hgrn_lb_logits.astype(F32), axis=0)
    lower_bounds = jnp.concatenate([jnp.zeros_like(sm[:1]), jnp.cumsum(sm[1:], axis=0)], axis=0)
    head_id = jnp.arange(DM_D) // DH_D
    seg = (head_id[:, None] == head_id[None, :]).astype(BF16)
    n_phys = cache_k.shape[1]
    ck = jnp.transpose(cache_k, (0, 1, 3, 4, 2)).reshape(cache_k.shape[0], n_phys, HALF, PAGE_SIZE)
    cv = jnp.transpose(cache_v, (0, 1, 3, 4, 2)).reshape(cache_v.shape[0], n_phys, HALF, PAGE_SIZE)
    pt_flat = page_table.reshape(-1)
    q0 = N_COLS_RWKV
    k0, v0 = q0 + HALF, q0 + 2 * HALF

    k_rows_s, v_rows_s = [], []
    n_odd = cache_k.shape[0]
    kv_t = ()

    def rows_from_feature_major(a):
        return jnp.transpose(a.reshape(n_odd, n_p, H_C, DH_C, t_p), (0, 1, 4, 2, 3))
    gla_p, hgrn_p = [], []
    gla_s = hgrn_s = None
    rwkv_p, rwkv_s, shift_p, shift_s = [], [], [], []
    for i in range(DEPTH):
        j = i // 2
        x = ffn(x, row(ln_ffn1[i]), bf(ffn1_w_gate[i]), bf(ffn1_w_up[i]), bf(ffn1_w_down[i]))
        if i % 2 == 0:
            w_in, wgk, bgk = even_weights(w_in_even[j], gla_w_gk[j], gla_b_gk[j])
            z = rms_matmul(x, row(ln_mix[i]), w_in)
            o, (ga_p, hb_p), gla_s, hgrn_s = run_even_mixer(
                z, n_p, t_p, n_s, t_s, state_gla, state_hgrn, j, gla_s, hgrn_s, wgk, bgk,
                lower_bounds[j], gla_norm[j], hgrn_norm[j], chunk=EVEN_CHUNK)
            x = matmul_residual(x, o, bf(w_out_even[j]))
            gla_p.append(ga_p)
            hgrn_p.append(hb_p)
        else:
            w = w_in_odd[j]
            z, qkv, *kv_t = odd_in_proj(
                x, row(ln_mix[i]), bf(jnp.concatenate([w[:, 3 * HALF:], w[:, :3 * HALF]], axis=1)),
                kv_t, (n_odd, n_p, HALF, t_p), j)
            oc_p = sb_attention_prompt(qkv, sb_bias[j], n_seq=n_p, t=t_p, q_col=0, blk=SB_BLOCK,
                                       group=SB_GROUP)
            z_s = z[np_tok:]
            q_s, kc_s, vc_s = (z_s[:, c:c + HALF].reshape(n_s, t_s, HALF) for c in (q0, k0, v0))
            bias_rows = jnp.broadcast_to(jnp.tile(sb_bias[j], t_s)[:, None], (t_s * H_C, PAGE_SIZE))
            oc_s = sb_attention_sample(q_s, kc_s, vc_s, ck, cv, j, pt_flat, bias_rows)
            od_p, od_s, gb, sd_p, sd_s = run_rwkv(
                z, n_p, t_p, n_s, t_s, state_rwkv[j], state_rwkv_shift[j], rwkv_mu[j], rwkv_w0[j],
                rwkv_w2[j], rwkv_a0[j], rwkv_a2[j], rwkv_g2[j], rwkv_k_k[j], rwkv_k_a[j],
                rwkv_r_k[j], seg, nb_sample=RWKV_SAMPLE_GROUP)
            x = odd_out(x, oc_p, oc_s.reshape(ns_tok, HALF), od_p, od_s, gb,
                        row(rwkv_ln_w[j]), row(rwkv_ln_b[j]), seg, bf(w_out_odd[j]))
            k_rows_s.append(kc_s.reshape(n_s, t_s, H_C, DH_C))
            v_rows_s.append(vc_s.reshape(n_s, t_s, H_C, DH_C))
            rwkv_p.append(sd_p)
            rwkv_s.append(sd_s)
            shift_p.append(jnp.concatenate(
                [lax.slice(z, ((b + 1) * t_p - 1, 0), ((b + 1) * t_p, N_COLS_RWKV)) for b in range(n_p)],
                axis=0))
            shift_s.append(z_s.reshape(n_s, t_s, -1)[:, -1, :N_COLS_RWKV])
        x = ffn(x, row(ln_ffn2[i]), bf(ffn2_w_gate[i]), bf(ffn2_w_up[i]), bf(ffn2_w_down[i]))
        x = ple(x, p_all[i], row(ln_ple[i]), bf(ple_w_gate[i]), bf(ple_w_proj[i]))

    y = final_rmsnorm(x, row(final_norm))
    return (y[:np_tok].reshape(n_p, t_p, D_MODEL), y[np_tok:].reshape(n_s, t_s, D_MODEL),
            rows_from_feature_major(kv_t[0]), rows_from_feature_major(kv_t[1]),
            jnp.stack(k_rows_s), jnp.stack(v_rows_s),
            jnp.stack(gla_p), gla_s, jnp.stack(hgrn_p), hgrn_s,
            jnp.stack(rwkv_p), jnp.stack(rwkv_s), jnp.stack(shift_p), jnp.stack(shift_s))
```

```python
import functools

import jax
import jax.numpy as jnp
from jax import lax
from jax.experimental import pallas as pl
from jax.experimental.pallas import tpu as pltpu

F32 = jnp.float32
BF16 = jnp.bfloat16

D_MODEL = 1024
DEPTH = 4
PAGE_SIZE = 128
HALF = D_MODEL // 2
D_FF = 2816
PLE_DIM = 256
RMS_EPS = 1e-6
H_A = 4
DK_A = HALF // H_A // 2
DV_A = HALF // H_A
GLA_GK_RANK = 16
GLA_GK_NORM = 16.0
H_B = 4
EXP_B = 128
DV_B = HALF // H_B
H_C = 8
DH_C = HALF // H_C
H_D = 8
DH_D = HALF // H_D
DM_D = H_D * DH_D
R_W = 64
R_A = 64
R_G = 128
RWKV_LN_EPS = 64e-5
N_COLS_RWKV = 3 * DM_D + R_W + R_A + R_G
N_COLS_ODD = 3 * HALF + N_COLS_RWKV

LANES = 128
SUBLANES = 8
MXU_DIM = 256
VMEM_LIMIT_BYTES = 56 * 1024 * 1024

TOKEN_TILE = 512
FFN_TOKEN_TILE = 768
N_EVEN_HEADS = H_A + H_B
N_COLS_EVEN_PAD = (4 * N_EVEN_HEADS + 1) * LANES
GATED_SUB_BLOCK = 16


def _params(*semantics):
    return pltpu.CompilerParams(dimension_semantics=semantics,
                                vmem_limit_bytes=VMEM_LIMIT_BYTES)


def _resident(shape):
    return pl.BlockSpec(shape, lambda *_: (0,) * len(shape), pipeline_mode=pl.Buffered(1))


def _rms(x, g):
    return x * lax.rsqrt(jnp.mean(x * x, axis=-1, keepdims=True) + RMS_EPS) * g


def _dot(a, b):
    return jnp.dot(a, b, preferred_element_type=F32)


def _ffn_kernel(x_ref, ln_ref, wg_ref, wu_ref, wd_ref, o_ref):
    x = x_ref[...]
    h = _rms(x, ln_ref[...]).astype(BF16)
    acc = jnp.zeros(x.shape, F32)
    for f0 in range(0, D_FF, MXU_DIM):
        g = _dot(h, wg_ref[:, f0:f0 + MXU_DIM])
        u = _dot(h, wu_ref[:, f0:f0 + MXU_DIM])
        a = (g * jax.nn.sigmoid(g) * u).astype(BF16)
        acc = acc + _dot(a, wd_ref[f0:f0 + MXU_DIM, :])
    o_ref[...] = x + 0.5 * acc


def ffn(x, ln, wg, wu, wd):
    nt = x.shape[0]
    tile = FFN_TOKEN_TILE if nt % FFN_TOKEN_TILE == 0 else TOKEN_TILE
    tok = pl.BlockSpec((tile, D_MODEL), lambda i: (i, 0))
    return pl.pallas_call(
        _ffn_kernel,
        grid=(nt // tile,),
        in_specs=[tok, _resident((1, D_MODEL)), _resident((D_MODEL, D_FF)),
                  _resident((D_MODEL, D_FF)), _resident((D_FF, D_MODEL))],
        out_specs=tok,
        out_shape=jax.ShapeDtypeStruct(x.shape, F32),
        compiler_params=_params("parallel"),
        name="ffn",
    )(x, ln, wg, wu, wd)


def _rms_matmul_kernel(x_ref, ln_ref, w_ref, o_ref, *maybe_ob_ref, n_chunk, bf16_from):
    h = _rms(x_ref[...], ln_ref[...]).astype(BF16)
    n = w_ref.shape[1]
    for n0 in range(0, n, n_chunk):
        n1 = min(n0 + n_chunk, n)
        o_ref[:, n0:n1] = _dot(h, w_ref[:, n0:n1])
    if maybe_ob_ref:
        maybe_ob_ref[0][...] = o_ref[:, bf16_from:].astype(BF16)


def rms_matmul(x, ln, w, bf16_from=None):
    nt, n = x.shape[0], w.shape[1]
    out_specs = [pl.BlockSpec((TOKEN_TILE, n), lambda i: (i, 0))]
    out_shape = [jax.ShapeDtypeStruct((nt, n), F32)]
    if bf16_from is not None:
        out_specs.append(pl.BlockSpec((TOKEN_TILE, n - bf16_from), lambda i: (i, 0)))
        out_shape.append(jax.ShapeDtypeStruct((nt, n - bf16_from), BF16))
    out = pl.pallas_call(
        functools.partial(_rms_matmul_kernel, n_chunk=2 * MXU_DIM, bf16_from=bf16_from),
        grid=(nt // TOKEN_TILE,),
        in_specs=[pl.BlockSpec((TOKEN_TILE, D_MODEL), lambda i: (i, 0)),
                  _resident((1, D_MODEL)), _resident((D_MODEL, n))],
        out_specs=out_specs,
        out_shape=out_shape,
        compiler_params=_params("parallel"),
        name="rms_matmul",
    )(x, ln, w)
    return out[0] if bf16_from is None else out


def _odd_in_proj_kernel(x_ref, ln_ref, w_ref, *refs, n_chunk, q0, n_feature_major_tiles, n_aliased):
    o_ref, ob_ref, kt_ref, vt_ref = refs[n_aliased:]
    h = _rms(x_ref[...], ln_ref[...]).astype(BF16)
    n = w_ref.shape[1]
    for n0 in range(0, n, n_chunk):
        n1 = min(n0 + n_chunk, n)
        o_ref[:, n0:n1] = _dot(h, w_ref[:, n0:n1])
    ob_ref[...] = o_ref[:, q0:].astype(BF16)

    @pl.when(pl.program_id(0) < n_feature_major_tiles)
    def _():
        kt_ref[...] = o_ref[:, q0 + HALF:q0 + 2 * HALF].T
        vt_ref[...] = o_ref[:, q0 + 2 * HALF:q0 + 3 * HALF].T


def odd_in_proj(x, ln, w, bufs, buf_shape, layer):
    nt, n = x.shape[0], w.shape[1]
    q0 = n - 3 * HALF
    _, n_seq, _, t = buf_shape
    tiles_per_seq = t // TOKEN_TILE
    n_fm = n_seq * tiles_per_seq
    tok = lambda ncol: pl.BlockSpec((TOKEN_TILE, ncol), lambda i: (i, 0))

    def fm_index(i):
        i = jnp.minimum(i, n_fm - 1)
        return layer, i // tiles_per_seq, 0, i % tiles_per_seq

    fm = pl.BlockSpec((None, None, HALF, TOKEN_TILE), fm_index)
    any_spec = pl.BlockSpec(memory_space=pl.ANY)
    return pl.pallas_call(
        functools.partial(_odd_in_proj_kernel, n_chunk=2 * MXU_DIM, q0=q0,
                          n_feature_major_tiles=n_fm, n_aliased=len(bufs)),
        grid=(nt // TOKEN_TILE,),
        in_specs=[tok(D_MODEL), _resident((1, D_MODEL)), _resident((D_MODEL, n))]
        + [any_spec] * len(bufs),
        out_specs=[tok(n), tok(n - q0), fm, fm],
        out_shape=[jax.ShapeDtypeStruct((nt, n), F32), jax.ShapeDtypeStruct((nt, n - q0), BF16),
                   jax.ShapeDtypeStruct(buf_shape, F32), jax.ShapeDtypeStruct(buf_shape, F32)],
        input_output_aliases={3 + k: 2 + k for k in range(len(bufs))},
        compiler_params=_params("arbitrary"),
        name="odd_in_proj",
    )(x, ln, w, *bufs)


def _matmul_residual_kernel(x_ref, a_ref, w_ref, o_ref):
    o_ref[...] = x_ref[...] + _dot(a_ref[...].astype(BF16), w_ref[...])


def matmul_residual(x, a, w):
    nt, k = a.shape
    return pl.pallas_call(
        _matmul_residual_kernel,
        grid=(nt // TOKEN_TILE,),
        in_specs=[pl.BlockSpec((TOKEN_TILE, D_MODEL), lambda i: (i, 0)),
                  pl.BlockSpec((TOKEN_TILE, k), lambda i: (i, 0)),
                  _resident((k, D_MODEL))],
        out_specs=pl.BlockSpec((TOKEN_TILE, D_MODEL), lambda i: (i, 0)),
        out_shape=jax.ShapeDtypeStruct(x.shape, F32),
        compiler_params=_params("parallel"),
        name="matmul_residual",
    )(x, a, w)


def _ple_kernel(x_ref, p_ref, p_tail_ref, ln_ref, wg_ref, wp_ref, *refs, n_lead_tiles, final):
    tail = pl.program_id(0) >= n_lead_tiles
    x = x_ref[...]
    h = _rms(x, ln_ref[...]).astype(BF16)
    gate = jax.nn.sigmoid(_dot(h, wg_ref[...]))
    p = jnp.where(tail, p_tail_ref[...], p_ref[...]).astype(BF16)
    o = x + gate * _dot(p, wp_ref[...])
    if not final:
        refs[0][...] = o
        return
    fn_ref, y_ref, y_tail_ref = refs
    y = _rms(o, fn_ref[...])

    @pl.when(jnp.logical_not(tail))
    def _():
        y_ref[...] = y

    @pl.when(tail)
    def _():
        y_tail_ref[...] = y


def ple(x, p_lead, p_tail, layer, ln, wg, wp, final_gain=None):
    nt = x.shape[0]
    n_lead = p_lead.shape[1]
    n_lead_tiles = n_lead // TOKEN_TILE
    assert nt == n_lead + TOKEN_TILE and p_tail.shape[1] == TOKEN_TILE
    tok = pl.BlockSpec((TOKEN_TILE, D_MODEL), lambda i: (i, 0))
    lead = lambda n: pl.BlockSpec((TOKEN_TILE, n), lambda i: (jnp.minimum(i, n_lead_tiles - 1), 0))
    in_specs = [tok,
                pl.BlockSpec((None, TOKEN_TILE, PLE_DIM),
                             lambda i: (layer, jnp.minimum(i, n_lead_tiles - 1), 0)),
                pl.BlockSpec((None, TOKEN_TILE, PLE_DIM), lambda i: (layer, 0, 0)),
                _resident((1, D_MODEL)), _resident((D_MODEL, D_MODEL)), _resident((PLE_DIM, D_MODEL))]
    args = [x, p_lead, p_tail, ln, wg, wp]
    final = final_gain is not None
    if final:
        in_specs.append(_resident((1, D_MODEL)))
        args.append(final_gain)
        out_specs = [lead(D_MODEL), pl.BlockSpec((TOKEN_TILE, D_MODEL), lambda i: (0, 0))]
        out_shape = [jax.ShapeDtypeStruct((n_lead, D_MODEL), F32),
                     jax.ShapeDtypeStruct((TOKEN_TILE, D_MODEL), F32)]
    else:
        out_specs, out_shape = tok, jax.ShapeDtypeStruct(x.shape, F32)
    return pl.pallas_call(
        functools.partial(_ple_kernel, n_lead_tiles=n_lead_tiles, final=final),
        grid=(nt // TOKEN_TILE,),
        in_specs=in_specs,
        out_specs=out_specs,
        out_shape=out_shape,
        compiler_params=_params("arbitrary" if final else "parallel"),
        name="ple",
    )(*args)


def _log_sigmoid(x):
    return jnp.minimum(x, 0.0) - jnp.log1p(jnp.exp(-jnp.abs(x)))


def _split_bf16(x):
    hi = x.astype(BF16)
    lo = (x - hi.astype(F32)).astype(BF16)
    return hi, lo


def _gated_chunks(heads, st_ref, tri):
    c = heads[0][0].shape[0]
    n = len(heads)
    sub = min(c, GATED_SUB_BLOCK)
    row = lax.broadcasted_iota(jnp.int32, (sub, 1), 0)
    tn_dims = (((0,), (0,)), ((), ()))

    bs = []
    for q, k, v, g in heads:
        g_hi, g_lo = _split_bf16(g * LOG2E)
        bs.append(_dot(tri, g_hi) + _dot(tri, g_lo))
    sts = [st_ref[h] for h in range(n)]
    o_inter = [lax.dot_general((heads[h][0] * jnp.exp2(bs[h])).astype(BF16), sts[h].astype(BF16),
                               _NT_DIMS, preferred_element_type=F32) for h in range(n)]

    scores = [[None] * (c // sub) for _ in range(n)]
    for h, (q, k, v, g) in enumerate(heads):
        b = bs[h]
        for i, r0 in enumerate(range(sub, c, sub), start=1):
            b_ref = b[r0:r0 + 1, :]
            q_rel = (q[r0:r0 + sub, :] * jnp.exp2(b[r0:r0 + sub, :] - b_ref)).astype(BF16)
            k_rel = (k[:r0, :] * jnp.exp2(b_ref - b[:r0, :])).astype(BF16)
            scores[h][i] = lax.dot_general(q_rel, k_rel, _NT_DIMS, preferred_element_type=F32)
    off_diag = [[None] * (c // sub) for _ in range(n)]
    for h, (q, k, v, g) in enumerate(heads):
        for i, r0 in enumerate(range(sub, c, sub), start=1):
            off_diag[h][i] = _dot(scores[h][i].astype(BF16), v[:r0, :].astype(BF16))

    for h, (q, k, v, g) in enumerate(heads):
        b_last = bs[h][c - 1:c, :]
        ke = (k * jnp.exp2(b_last - bs[h])).astype(BF16)
        st_ref[h] = sts[h] * jnp.exp2(b_last) + lax.dot_general(
            v.astype(BF16), ke, tn_dims, preferred_element_type=F32)

    outs = []
    for h, (q, k, v, g) in enumerate(heads):
        parts = []
        for i, r0 in enumerate(range(0, c, sub)):
            ki, vi, bi = (a[r0:r0 + sub, :] for a in (k, v, bs[h]))
            oi = o_inter[h][r0:r0 + sub, :]
            if i > 0:
                oi = oi + off_diag[h][i]
            for p0 in range(0, sub, SUBLANES):
                qp = q[r0 + p0:r0 + p0 + SUBLANES, :]
                bp = bi[p0:p0 + SUBLANES, :]
                op = oi[p0:p0 + SUBLANES, :]
                for s in range(min(sub, p0 + SUBLANES)):
                    d = bp - bi[s:s + 1, :]
                    if s > p0:
                        d = jnp.where(row[p0:p0 + SUBLANES] >= s, d, -1e30)
                    a_s = jnp.sum(qp * ki[s:s + 1, :] * jnp.exp2(d), axis=-1, keepdims=True)
                    op = op + a_s * vi[s:s + 1, :]
                parts.append(op)
        outs.append(jnp.concatenate(parts, axis=0) if len(parts) > 1 else parts[0])
    return outs


def _even_heads(z_ref, wgk_ref, bgk_ref, lb_ref, na_ref, nb_ref):
    def blk(i):
        return z_ref[:, i * LANES:(i + 1) * LANES]

    heads, gains, gates = [], [], []
    gk = _log_sigmoid(_dot(blk(4 * N_EVEN_HEADS).astype(BF16), wgk_ref[...]) + bgk_ref[...])
    gk = gk * (1.0 / GLA_GK_NORM)
    for h in range(H_A):
        heads.append((blk(h) * DK_A ** -0.5, blk(H_A + h), blk(2 * H_A + h),
                      gk[:, h * LANES:(h + 1) * LANES]))
        gains.append(na_ref[...])
        gates.append(blk(3 * H_A + h))
    base = 4 * H_A
    for h in range(H_B):
        lb = lb_ref[:, h * LANES:(h + 1) * LANES]
        fb = blk(base + H_B + h)
        a = jnp.log1p(-lb) + _log_sigmoid(fb)
        log_lb = jnp.log(lb)
        g = jnp.maximum(a, log_lb) + jnp.log1p(jnp.exp(-jnp.abs(a - log_lb)))
        heads.append((blk(base + h), (1.0 - lb) * jax.nn.sigmoid(-fb), blk(base + 2 * H_B + h), g))
        gains.append(nb_ref[...])
        gates.append(blk(base + 3 * H_B + h))
    return heads, gains, gates


def _even_finish(o, gain, gate):
    o = o * lax.rsqrt(jnp.mean(o * o, axis=-1, keepdims=True) + RMS_EPS) * gain
    return o * (gate * jax.nn.sigmoid(gate))


def _lower_tri(c):
    return (lax.broadcasted_iota(jnp.int32, (c, c), 0)
            >= lax.broadcasted_iota(jnp.int32, (c, c), 1)).astype(BF16)


def _even_mixer_kernel(z_ref, s0_ref, wgk_ref, bgk_ref, lb_ref, na_ref, nb_ref,
                       o_ref, sout_ref, st_ref):
    ci = pl.program_id(1)

    @pl.when(ci == 0)
    def _():
        st_ref[...] = s0_ref[0]

    heads, gains, gates = _even_heads(z_ref, wgk_ref, bgk_ref, lb_ref, na_ref, nb_ref)
    for h, o in enumerate(_gated_chunks(heads, st_ref, _lower_tri(z_ref.shape[0]))):
        o_ref[:, h * LANES:(h + 1) * LANES] = _even_finish(o, gains[h], gates[h])

    @pl.when(ci == pl.num_programs(1) - 1)
    def _():
        sout_ref[0] = st_ref[...]


def _even_mixer_short_kernel(z_ref, sg_ref, sh_ref, wgk_ref, bgk_ref, lb_ref, na_ref, nb_ref,
                             *refs, t_s, n_aliased):
    o_ref, sg_out_ref, sh_out_ref, st_ref = refs[n_aliased:]
    c = z_ref.shape[0]
    spb = c // t_s
    row = lax.broadcasted_iota(jnp.int32, (c, 1), 0)
    zeros = jnp.zeros((LANES, LANES - DK_A), F32)
    for q in range(spb):
        for h in range(H_A):
            st_ref[q * N_EVEN_HEADS + h] = jnp.concatenate([sg_ref[q, h].T, zeros], axis=1)
        for h in range(H_B):
            st_ref[q * N_EVEN_HEADS + H_A + h] = sh_ref[q, h].T

    heads, gains, gates = _even_heads(z_ref, wgk_ref, bgk_ref, lb_ref, na_ref, nb_ref)
    all_heads = []
    for q in range(spb):
        mine = (row // t_s) == q
        all_heads += [(hq, jnp.where(mine, hk, 0.0), hv, jnp.where(mine, hg, 0.0))
                      for hq, hk, hv, hg in heads]
    outs = _gated_chunks(all_heads, st_ref, _lower_tri(c))
    for h in range(N_EVEN_HEADS):
        o = outs[h]
        for q in range(1, spb):
            o = jnp.where((row // t_s) == q, outs[q * N_EVEN_HEADS + h], o)
        o_ref[:, h * LANES:(h + 1) * LANES] = _even_finish(o, gains[h], gates[h])

    for q in range(spb):
        for h in range(H_A):
            sg_out_ref[q, h] = st_ref[q * N_EVEN_HEADS + h].T[:DK_A, :]
        for h in range(H_B):
            sh_out_ref[q, h] = st_ref[q * N_EVEN_HEADS + H_A + h].T


def even_mixer(z, s0, wgk, bgk, lb, norm_a, norm_b, *, n_seq, t, chunk):
    nt, ncol = z.shape
    n_chunks = t // chunk
    st_shape = (N_EVEN_HEADS, LANES, LANES)
    return pl.pallas_call(
        _even_mixer_kernel,
        grid=(n_seq, n_chunks),
        in_specs=[pl.BlockSpec((chunk, ncol), lambda b, c: (b * n_chunks + c, 0)),
                  pl.BlockSpec((1,) + st_shape, lambda b, c: (b, 0, 0, 0)),
                  _resident(wgk.shape), _resident(bgk.shape), _resident(lb.shape),
                  _resident(norm_a.shape), _resident(norm_b.shape)],
        out_specs=[pl.BlockSpec((chunk, D_MODEL), lambda b, c: (b * n_chunks + c, 0)),
                   pl.BlockSpec((1,) + st_shape, lambda b, c: (b, 0, 0, 0))],
        out_shape=[jax.ShapeDtypeStruct((nt, D_MODEL), F32),
                   jax.ShapeDtypeStruct((n_seq,) + st_shape, F32)],
        scratch_shapes=[pltpu.VMEM(st_shape, F32)],
        compiler_params=_params("parallel", "arbitrary"),
        name="even_mixer",
    )(z, s0, wgk, bgk, lb, norm_a, norm_b)


def even_mixer_short(z, state_gla, state_hgrn, layer, wgk, bgk, lb, norm_a, norm_b,
                     o_buf, gla_buf, hgrn_buf, *, tok0, n_seq, t_s):
    ncol = z.shape[1]
    spb = SUBLANES // t_s

    def st_spec(a):
        return pl.BlockSpec((None, spb) + a.shape[2:], lambda g: (layer, g, 0, 0, 0))

    any_spec = pl.BlockSpec(memory_space=pl.ANY)
    row_blk = lambda n: pl.BlockSpec((SUBLANES, n), lambda g: (tok0 // SUBLANES + g, 0))
    bufs = [o_buf] if gla_buf is None else [o_buf, gla_buf, hgrn_buf]
    n_fixed = 8
    return pl.pallas_call(
        functools.partial(_even_mixer_short_kernel, t_s=t_s, n_aliased=len(bufs)),
        grid=(n_seq // spb,),
        in_specs=[row_blk(ncol), st_spec(state_gla), st_spec(state_hgrn),
                  _resident(wgk.shape), _resident(bgk.shape), _resident(lb.shape),
                  _resident(norm_a.shape), _resident(norm_b.shape)] + [any_spec] * len(bufs),
        out_specs=[row_blk(D_MODEL), st_spec(state_gla), st_spec(state_hgrn)],
        out_shape=[jax.ShapeDtypeStruct(o_buf.shape, F32), jax.ShapeDtypeStruct(state_gla.shape, F32),
                   jax.ShapeDtypeStruct(state_hgrn.shape, F32)],
        input_output_aliases={n_fixed + k: k for k in range(len(bufs))},
        scratch_shapes=[pltpu.VMEM((spb * N_EVEN_HEADS, LANES, LANES), F32)],
        compiler_params=_params("parallel"),
        name="even_mixer_short",
    )(z, state_gla, state_hgrn, wgk, bgk, lb, norm_a, norm_b, *bufs)


def _pad_heads(w, n_heads):
    d = w.shape[-1] // n_heads
    w = w.reshape(w.shape[:-1] + (n_heads, d))
    w = jnp.pad(w, [(0, 0)] * (w.ndim - 1) + [(0, LANES - d)])
    return w.reshape(w.shape[:-2] + (n_heads * LANES,))


def _split_cols(w, sizes):
    out, o = [], 0
    for s in sizes:
        out.append(w[..., o:o + s])
        o += s
    return out


def even_weights(w_in, w_gk, b_gk):
    ka = H_A * DK_A
    qa, kk, va, gk_in, ga, qb, fb, ib, gb = _split_cols(
        w_in, (ka, ka, HALF, GLA_GK_RANK, HALF, HALF, HALF, HALF, HALF))
    gk_in = jnp.pad(gk_in, ((0, 0), (0, LANES - GLA_GK_RANK)))
    w = jnp.concatenate([_pad_heads(qa, H_A), _pad_heads(kk, H_A), va, ga, qb, fb, ib, gb, gk_in],
                        axis=-1).astype(BF16)
    wgk = jnp.pad(_pad_heads(w_gk, H_A), ((0, LANES - GLA_GK_RANK), (0, 0))).astype(BF16)
    bgk = _pad_heads(b_gk[None, :], H_A)
    return w, wgk, bgk


def run_even_mixer(z, n_p, t_p, n_s, t_s, state_gla, state_hgrn, layer, gla_buf, hgrn_buf,
                   wgk, bgk, lb, norm_a, norm_b, *, chunk):
    args = (wgk, bgk, lb[None, :], norm_a[None, :], norm_b[None, :])
    zero = jnp.zeros((n_p, N_EVEN_HEADS, LANES, LANES), F32)
    o_buf, st_p = even_mixer(z, zero, *args, n_seq=n_p, t=t_p, chunk=chunk)
    o, gla_buf, hgrn_buf = even_mixer_short(z, state_gla, state_hgrn, layer, *args,
                                            o_buf, gla_buf, hgrn_buf,
                                            tok0=n_p * t_p, n_seq=n_s, t_s=t_s)
    gla_p = jnp.swapaxes(st_p[:, :H_A], -1, -2)[:, :, :DK_A, :]
    hgrn_p = jnp.swapaxes(st_p[:, H_A:], -1, -2)
    return o, (gla_p, hgrn_p), gla_buf, hgrn_buf


_NT_DIMS = (((1,), (1,)), ((), ()))
LOG2E = 1.4426950408889634


def _neg_softplus(z):
    return -(jnp.maximum(z, 0.0) + jnp.log1p(jnp.exp(-jnp.abs(z))))


def _suffix_tri(n):
    return (lax.broadcasted_iota(jnp.int32, (n, n), 0)
            >= lax.broadcasted_iota(jnp.int32, (n, n), 1)).astype(BF16)


def _sb_prompt_kernel(bias_ref, q_ref, k_ref, v_ref, o_ref, u_ref, tot_ref, carry_ref, acc_ref,
                      *, blk, group):
    pair = pl.program_id(1)
    qi = pl.program_id(2)
    lane = lax.broadcasted_iota(jnp.int32, (1, LANES), 1)
    head_a = lane < DH_C
    q2 = q_ref[...].astype(F32) * (DH_C ** -0.5 * LOG2E)
    qs = jnp.concatenate([jnp.where(head_a, q2, 0.0), jnp.where(head_a, 0.0, q2)], axis=0).astype(BF16)
    row = lax.broadcasted_iota(jnp.int32, (2 * blk, 1), 0)
    bias2 = jnp.where(row < blk, bias_ref[2 * pair], bias_ref[2 * pair + 1]) * LOG2E
    neg_tri = -_suffix_tri(blk)
    strict = lax.broadcasted_iota(jnp.int32, (2 * blk, blk), 1) < (row % blk)

    def rows_of(ref, j):
        return ref[pl.ds(pl.multiple_of(j * blk, blk), blk), :]

    def scores(slot, j, diagonal=False):
        b2 = jnp.where(j >= 0, bias2, -1e30)
        z2 = lax.dot_general(qs, rows_of(k_ref, jnp.maximum(j, 0)), _NT_DIMS,
                             preferred_element_type=F32) + b2
        if diagonal:
            z2 = jnp.where(strict, z2, -1e30)
        sp2 = jnp.maximum(z2, 0.0) + jnp.log2(1.0 + jnp.exp2(-jnp.abs(z2)))
        incl = _dot(sp2.astype(BF16), neg_tri)
        u_ref[slot] = z2 + incl
        tot_ref[slot] = incl[:, 0:1]

    def accumulate(slot, j):
        w = jnp.exp2(u_ref[slot] + carry_ref[...])
        acc_ref[...] += _dot(w.astype(BF16), rows_of(v_ref, jnp.maximum(j, 0)))
        carry_ref[...] += tot_ref[slot]

    def body(k, _):
        j = qi - group * k
        for s in range(group):
            accumulate(s, j + group - s)
        for s in range(group):
            scores(s, j - s)
        return 0

    carry_ref[...] = jnp.zeros(carry_ref.shape, F32)
    acc_ref[...] = jnp.zeros(acc_ref.shape, F32)
    for s in range(group):
        scores(s, qi - s, diagonal=(s == 0))
    lax.fori_loop(1, qi // group + 1, body, 0)
    for s in range(group):
        accumulate(s, qi % group - s)
    o_ref[...] = jnp.where(head_a, acc_ref[:blk, :], acc_ref[blk:, :])


def sb_attention_prompt(z, bias, *, n_seq, t, q_col, blk, group):
    nq = t // blk
    n_pairs = H_C // 2
    return pl.pallas_call(
        functools.partial(_sb_prompt_kernel, blk=blk, group=group),
        grid=(n_seq, n_pairs, nq),
        in_specs=[pl.BlockSpec(memory_space=pltpu.SMEM),
                  pl.BlockSpec((blk, LANES), lambda b, p, i: (b * nq + i, q_col + p)),
                  pl.BlockSpec((t, LANES), lambda b, p, i: (b, q_col + n_pairs + p)),
                  pl.BlockSpec((t, LANES), lambda b, p, i: (b, q_col + 2 * n_pairs + p))],
        out_specs=pl.BlockSpec((blk, LANES), lambda b, p, i: (b * nq + i, p)),
        out_shape=jax.ShapeDtypeStruct((n_seq * t, HALF), F32),
        scratch_shapes=[pltpu.VMEM((group, 2 * blk, blk), F32), pltpu.VMEM((group, 2 * blk, 1), F32),
                        pltpu.VMEM((2 * blk, 1), F32), pltpu.VMEM((2 * blk, LANES), F32)],
        compiler_params=_params("parallel", "parallel", "arbitrary"),
        name="sb_attention_prompt",
    )(bias, z, z, z)


def _sb_sample_kernel(pt_ref, q_ref, kn_ref, vn_ref, bias_ref, *refs, n_pages, t_s):
    del pt_ref
    k_pages, v_pages, o_ref = refs[:n_pages], refs[n_pages:2 * n_pages], refs[2 * n_pages]
    rows = t_s * H_C
    q = q_ref[0] * DH_C ** -0.5
    qx = jnp.concatenate([jnp.broadcast_to(q[i:i + 1, :], (H_C, HALF)) for i in range(t_s)], axis=0)
    r_id = lax.broadcasted_iota(jnp.int32, (rows, HALF), 0)
    l_id = lax.broadcasted_iota(jnp.int32, (rows, HALF), 1)
    head_lanes = (l_id // DH_C) == (r_id % H_C)
    qx = jnp.where(head_lanes, qx, 0.0)
    bias = bias_ref[...]
    q_idx = lax.broadcasted_iota(jnp.int32, (rows, 1), 0) // H_C

    carry = jnp.zeros((rows, 1), F32)
    acc = jnp.zeros((rows, HALF), F32)
    kn, vn = kn_ref[0], vn_ref[0]
    for j in reversed(range(t_s)):
        visible = q_idx > j
        z = jnp.sum(qx * kn[j:j + 1, :], axis=-1, keepdims=True) + bias[:, 0:1]
        lk = jnp.where(visible, _neg_softplus(z), 0.0)
        w = jnp.where(visible, jnp.exp(z + lk + carry), 0.0)
        acc = acc + w * vn[j:j + 1, :]
        carry = carry + lk

    qx = qx.astype(BF16)
    tri = _suffix_tri(PAGE_SIZE)
    zs = [_dot(qx, k_pages[p][...].astype(BF16)) + bias for p in range(n_pages)]
    incls = [_dot(_neg_softplus(z).astype(BF16), tri) for z in zs]
    ws = [None] * n_pages
    for p in reversed(range(n_pages)):
        ws[p] = jnp.exp(zs[p] + incls[p] + carry).astype(BF16)
        carry = carry + incls[p][:, 0:1]
    for p in range(n_pages):
        acc = acc + lax.dot_general(ws[p], v_pages[p][...].astype(BF16), _NT_DIMS,
                                    preferred_element_type=F32)

    acc = jnp.where(head_lanes, acc, 0.0)
    for i in range(t_s):
        o_ref[0, i:i + 1, :] = jnp.sum(acc[i * H_C:(i + 1) * H_C, :], axis=0, keepdims=True)


def sb_attention_sample(q, k_new, v_new, cache_k, cache_v, layer, page_table, bias_rows):
    n_seq, t_s, _ = q.shape
    n_pages = page_table.shape[0] // n_seq
    tok = pl.BlockSpec((1, t_s, HALF), lambda b, pt: (b, 0, 0))

    def page_spec(p):
        return pl.BlockSpec((None, None, HALF, PAGE_SIZE),
                            lambda b, pt: (layer, pt[b * n_pages + p], 0, 0))

    pages = [page_spec(p) for p in range(n_pages)]
    return pl.pallas_call(
        functools.partial(_sb_sample_kernel, n_pages=n_pages, t_s=t_s),
        grid_spec=pltpu.PrefetchScalarGridSpec(
            num_scalar_prefetch=1,
            grid=(n_seq,),
            in_specs=[tok, tok, tok,
                      pl.BlockSpec(bias_rows.shape, lambda b, pt: (0, 0))] + pages + pages,
            out_specs=tok),
        out_shape=jax.ShapeDtypeStruct((n_seq, t_s, HALF), F32),
        compiler_params=_params("parallel"),
        name="sb_attention_sample",
    )(page_table, q, k_new, v_new, bias_rows, *([cache_k] * n_pages), *([cache_v] * n_pages))


N_SCAN_ROWS = 13
N_SCAN_STEP_ROWS = 9
RWKV_STEPS_PER_TRIP = 8
N_PAIRS_D = H_D // 2


def _head_sum(x, seg):
    hi, lo = _split_bf16(x)
    return _dot(hi, seg) + _dot(lo, seg)


def _rwkv_prep_kernel(zd_ref, zb_ref, prev_s_ref, mu_ref, w0_ref, w2_ref, a0_ref, a2_ref, g2_ref,
                      kk_ref, ka_ref, rk_ref, seg_ref, rows_ref, v_ref, gb_ref,
                      *, tiles_per_seq, n_prompt_tiles):
    i = pl.program_id(0)
    z = zd_ref[...]
    tm = z.shape[0]
    boundary = jnp.where(i % tiles_per_seq == 0, 0.0, 1.0) * zb_ref[SUBLANES - 1:SUBLANES, :]
    row = lax.broadcasted_iota(jnp.int32, (tm, 1), 0)
    prev = jnp.where(row == 0, boundary, pltpu.roll(z, 1, axis=0))
    prev = jnp.where(i >= n_prompt_tiles, prev_s_ref[...], prev)
    zs = z + (prev - z) * mu_ref[...]
    r, k, v = zs[:, 0:DM_D], zs[:, DM_D:2 * DM_D], zs[:, 2 * DM_D:3 * DM_D]
    u = zs[:, 3 * DM_D:3 * DM_D + R_W + R_A]
    g_in = zs[:, 3 * DM_D + R_W + R_A:]
    seg = seg_ref[...]
    w = w0_ref[...] + _dot(jnp.tanh(u).astype(BF16), w2_ref[...])
    decay = jnp.exp(-jnp.exp(_log_sigmoid(w) - 0.5))
    a = jax.nn.sigmoid(a0_ref[...] + _dot(u.astype(BF16), a2_ref[...]))
    g = _dot(jax.nn.sigmoid(g_in).astype(BF16), g2_ref[...])
    kk = k * kk_ref[...]
    kk = kk * lax.rsqrt(jnp.maximum(_head_sum(kk * kk, seg), 1e-24))
    kp = k * (1.0 + (a - 1.0) * ka_ref[...])
    beta = kk * a
    wr = decay * r
    odd = (row % 2) == 1
    w_a, beta_a, kp_a = (pltpu.roll(t, 1, axis=0) for t in (decay, beta, kp))
    outs = (jnp.where(odd, w_a * kk, kk), jnp.where(odd, w_a * wr, wr),
            w_a * decay, beta_a * decay, kp_a * decay, beta, kp,
            _head_sum(beta_a * kk, seg), _head_sum(kp_a * kk, seg),
            jnp.where(odd, _head_sum(beta_a * wr, seg), 0.0),
            jnp.where(odd, _head_sum(kp_a * wr, seg), 0.0),
            _head_sum(beta * r, seg), _head_sum(kp * r, seg))
    for n, val in enumerate(outs):
        rows_ref[:, n * DM_D:(n + 1) * DM_D] = val
    v_ref[...] = v
    gb_ref[:, 0:DM_D] = g
    gb_ref[:, DM_D:] = _head_sum(r * kp * rk_ref[...], seg) * v


def rwkv_prep(z, prev_s, mu, w0, w2, a0, a2, g2, k_k, k_a, r_k, seg, *, n_prompt_tok, t_prompt):
    nt = z.shape[0]
    tm = TOKEN_TILE
    vec = _resident((1, DM_D))
    tok = lambda n: pl.BlockSpec((tm, n), lambda i: (i, 0))
    return pl.pallas_call(
        functools.partial(_rwkv_prep_kernel, tiles_per_seq=t_prompt // tm,
                          n_prompt_tiles=n_prompt_tok // tm),
        grid=(nt // tm,),
        in_specs=[tok(N_COLS_RWKV),
                  pl.BlockSpec((SUBLANES, N_COLS_RWKV),
                               lambda i: (jnp.maximum(i * (tm // SUBLANES) - 1, 0), 0)),
                  _resident(prev_s.shape), _resident((1, N_COLS_RWKV)),
                  vec, _resident(w2.shape), vec, _resident(a2.shape), _resident(g2.shape),
                  vec, vec, vec, _resident(seg.shape)],
        out_specs=[tok(N_SCAN_ROWS * DM_D), tok(DM_D), tok(2 * DM_D)],
        out_shape=[jax.ShapeDtypeStruct((nt, N_SCAN_ROWS * DM_D), F32),
                   jax.ShapeDtypeStruct((nt, DM_D), F32),
                   jax.ShapeDtypeStruct((nt, 2 * DM_D), F32)],
        compiler_params=_params("parallel"),
        name="rwkv_prep",
    )(z, z, prev_s, mu, w0, w2, a0, a2, g2, k_k, k_a, r_k, seg)


def _rwkv_scan_kernel(*refs, nb, t_c, rows_per_seq):
    n_in = nb if rows_per_seq else 1
    rows_refs, v_refs = refs[:n_in], refs[n_in:2 * n_in]
    s0_ref, o_ref, sout_ref, s_scr, vt_scr, sk_scr, pr_scr = refs[2 * n_in:]
    ci = pl.program_id(1)

    @pl.when(ci == 0)
    def _():
        s_scr[...] = s0_ref[...]

    def seq_rows(ref_list, kq, start, size):
        if rows_per_seq:
            return ref_list[kq][pl.ds(start, size), :]
        return ref_list[0][pl.ds(kq * t_c + start, size), :]

    lane = lax.broadcasted_iota(jnp.int32, (1, LANES), 1)
    head_a = lane < DH_D
    ones_bd = ((lax.broadcasted_iota(jnp.int32, (LANES, LANES), 0) // DH_D)
               == (lax.broadcasted_iota(jnp.int32, (LANES, LANES), 1) // DH_D)).astype(BF16)

    spb = 1 if rows_per_seq else nb
    n_sel = spb * t_c
    sel = ((lax.broadcasted_iota(jnp.int32, (n_sel, spb * LANES), 0) // t_c
            == lax.broadcasted_iota(jnp.int32, (n_sel, spb * LANES), 1) // LANES)
           & (lax.broadcasted_iota(jnp.int32, (n_sel, spb * LANES), 0) % t_c
              == lax.broadcasted_iota(jnp.int32, (n_sel, spb * LANES), 1) % DH_D)).astype(BF16)
    for blk_i in range(n_in):
        for p in range(N_PAIRS_D):
            vp = v_refs[blk_i][:, p * LANES:(p + 1) * LANES].astype(BF16)
            vt = lax.dot_general(vp, sel, (((0,), (0,)), ((), ())), preferred_element_type=F32)
            for q in range(spb):
                blk = vt[:, q * LANES:(q + 1) * LANES]
                vt_scr[blk_i * spb + q, p] = jnp.where(head_a, blk[:DH_D], blk[DH_D:]).astype(BF16)
    sk_scr[...] = jnp.zeros(sk_scr.shape, F32)
    pr_scr[...] = jnp.zeros(pr_scr.shape, F32)

    def body(i, carry):
        ta = 2 * i
        here_a = (lane % DH_D) == ta
        here_b = (lane % DH_D) == ta + 1
        ha16, hb16 = here_a.astype(BF16), here_b.astype(BF16)
        for kq in range(nb):
            ra = seq_rows(rows_refs, kq, ta, 1)
            rb = seq_rows(rows_refs, kq, ta + 1, 1)
            for p in range(N_PAIRS_D):
                def col(r, n):
                    return r[:, n * DM_D + p * LANES:n * DM_D + (p + 1) * LANES]
                w2, b1, k1, beta_b, kp_b, x1, x2 = (col(rb, n) for n in range(2, N_SCAN_STEP_ROWS))
                sp = s_scr[kq, p]
                s16 = sp.astype(BF16)
                vt = vt_scr[kq, p]
                lhs = jnp.concatenate(
                    [s16 * col(ra, 0).astype(BF16), s16 * col(ra, 1).astype(BF16),
                     s16 * col(rb, 0).astype(BF16), s16 * col(rb, 1).astype(BF16),
                     vt * ha16, vt * hb16], axis=0)
                red = _dot(lhs, ones_bd)
                skk_a, pr_a, r3, pr_b, v_a, v_b = (red[n * DH_D:(n + 1) * DH_D] for n in range(6))
                skk_b = r3 - skk_a * x1 + v_a * x2
                s_scr[kq, p] = sp * w2 - skk_a * b1 + v_a * k1 - skk_b * beta_b + v_b * kp_b
                sk_scr[kq, p] = jnp.where(here_a, skk_a, jnp.where(here_b, skk_b, sk_scr[kq, p]))
                pr_scr[kq, p] = jnp.where(here_a, pr_a, jnp.where(here_b, pr_b, pr_scr[kq, p]))
        return carry

    steps_per_trip = min(RWKV_STEPS_PER_TRIP, t_c // 2)

    def body_unrolled(j, carry):
        for u in range(steps_per_trip):
            body(steps_per_trip * j + u, carry)
        return carry

    lax.fori_loop(0, t_c // (2 * steps_per_trip), body_unrolled, 0)

    def token_major(tile):
        x = tile.T
        return jnp.concatenate([x[0:t_c], x[DH_D:DH_D + t_c]], axis=1)

    for kq in range(nb):
        rows = seq_rows(rows_refs, kq, 0, t_c)
        v = seq_rows(v_refs, kq, 0, t_c)
        for p in range(N_PAIRS_D):
            x3, x4, c1, c2 = (rows[:, n * DM_D + p * LANES:n * DM_D + (p + 1) * LANES]
                              for n in range(N_SCAN_STEP_ROWS, N_SCAN_ROWS))
            skt = token_major(sk_scr[kq, p])
            vp = v[:, p * LANES:(p + 1) * LANES]
            o = (token_major(pr_scr[kq, p]) - skt * c1 + vp * c2
                 - jnp.roll(skt, 1, axis=0) * x3 + jnp.roll(vp, 1, axis=0) * x4)
            o_ref[kq, :, p * LANES:(p + 1) * LANES] = o

    @pl.when(ci == pl.num_programs(1) - 1)
    def _():
        sout_ref[...] = s_scr[...]


def rwkv_scan(rows, v, s0, *, tok0, t, nb):
    n_seq = s0.shape[0]
    rows_per_seq = t >= DH_D
    t_c = DH_D if rows_per_seq else t
    n_chunks = t // t_c
    if rows_per_seq:
        def tok_spec(ncol):
            return [pl.BlockSpec((t_c, ncol),
                                 lambda g, c, kq=kq: (tok0 // t_c + (g * nb + kq) * n_chunks + c, 0))
                    for kq in range(nb)]
        n_in = nb
    else:
        def tok_spec(ncol):
            return [pl.BlockSpec((nb * t_c, ncol), lambda g, c: (tok0 // (nb * t_c) + g, 0))]
        n_in = 1
    st_shape = (nb,) + s0.shape[1:]
    st_spec = pl.BlockSpec(st_shape, lambda g, c: (g, 0, 0, 0))
    return pl.pallas_call(
        functools.partial(_rwkv_scan_kernel, nb=nb, t_c=t_c, rows_per_seq=rows_per_seq),
        grid=(n_seq // nb, n_chunks),
        in_specs=tok_spec(rows.shape[1]) + tok_spec(DM_D) + [st_spec],
        out_specs=[pl.BlockSpec((nb, t_c, DM_D), lambda g, c: (g, c, 0)), st_spec],
        out_shape=[jax.ShapeDtypeStruct((n_seq, t, DM_D), F32), jax.ShapeDtypeStruct(s0.shape, F32)],
        scratch_shapes=[pltpu.VMEM(st_shape, F32), pltpu.VMEM(st_shape, BF16),
                        pltpu.VMEM(st_shape, F32), pltpu.VMEM(st_shape, F32)],
        compiler_params=_params("parallel", "arbitrary"),
        name="rwkv_scan",
    )(*([rows] * n_in), *([v] * n_in), s0)


def _odd_out_kernel(x_ref, oc_ref, oc_tail_ref, od_ref, od_tail_ref, gb_ref, lnw_ref, lnb_ref,
                    seg_ref, w_ref, o_ref, *, n_lead_tiles):
    seg = seg_ref[...]
    tail = pl.program_id(0) >= n_lead_tiles
    o = jnp.where(tail, od_tail_ref[...], od_ref[...])
    oc = jnp.where(tail, oc_tail_ref[...], oc_ref[...])
    d = o - _head_sum(o, seg) * (1.0 / DH_D)
    var = _head_sum(d * d, seg) * (1.0 / DH_D)
    od = d * lax.rsqrt(var + RWKV_LN_EPS) * lnw_ref[...] + lnb_ref[...]
    od = (od + gb_ref[:, DM_D:]) * gb_ref[:, 0:DM_D]
    a = jnp.concatenate([oc, od], axis=-1).astype(BF16)
    o_ref[...] = x_ref[...] + _dot(a, w_ref[...])


def odd_out(x, o_c, o_c_tail, o_d, o_d_tail, gb, ln_w, ln_b, seg, w_out):
    nt = x.shape[0]
    n_lead_tiles = o_c.shape[0] // TOKEN_TILE
    assert nt == o_c.shape[0] + TOKEN_TILE and o_c_tail.shape[0] == TOKEN_TILE
    tok = lambda n: pl.BlockSpec((TOKEN_TILE, n), lambda i: (i, 0))
    lead = pl.BlockSpec((TOKEN_TILE, HALF), lambda i: (jnp.minimum(i, n_lead_tiles - 1), 0))
    tail = _resident((TOKEN_TILE, HALF))
    vec = _resident((1, DM_D))
    return pl.pallas_call(
        functools.partial(_odd_out_kernel, n_lead_tiles=n_lead_tiles),
        grid=(nt // TOKEN_TILE,),
        in_specs=[tok(D_MODEL), lead, tail, lead, tail, tok(2 * DM_D), vec, vec,
                  _resident(seg.shape), _resident(w_out.shape)],
        out_specs=tok(D_MODEL),
        out_shape=jax.ShapeDtypeStruct(x.shape, F32),
        compiler_params=_params("parallel"),
        name="odd_out",
    )(x, o_c, o_c_tail, o_d, o_d_tail, gb, ln_w, ln_b, seg, w_out)


def _state_to_pairs(s):
    n_seq = s.shape[0]
    s = s.reshape(n_seq, N_PAIRS_D, 2, DH_D, DH_D)
    return jnp.transpose(s, (0, 1, 3, 2, 4)).reshape(n_seq, N_PAIRS_D, DH_D, LANES)


def _state_from_pairs(s):
    n_seq = s.shape[0]
    s = s.reshape(n_seq, N_PAIRS_D, DH_D, 2, DH_D)
    return jnp.transpose(s, (0, 1, 3, 2, 4)).reshape(n_seq, H_D, DH_D, DH_D)


def run_rwkv(z, n_p, t_p, n_s, t_s, state, shift, mu, w0, w2, a0, a2, g2, k_k, k_a, r_k, seg,
             *, nb_sample):
    np_tok = n_p * t_p
    zd_s = z[np_tok:, :N_COLS_RWKV].reshape(n_s, t_s, N_COLS_RWKV)
    prev_s = jnp.concatenate([shift[:, None, :], zd_s[:, :-1]], axis=1).reshape(n_s * t_s, N_COLS_RWKV)
    w2p = jnp.pad(w2, ((0, R_A), (0, 0))).astype(BF16)
    a2p = jnp.pad(a2, ((R_W, 0), (0, 0))).astype(BF16)
    row = lambda a: a.reshape(1, -1)
    rows, v, gb = rwkv_prep(z, prev_s, row(mu), row(w0), w2p, row(a0), a2p, g2.astype(BF16),
                            row(k_k), row(k_a), row(r_k), seg, n_prompt_tok=np_tok, t_prompt=t_p)
    zero = jnp.zeros((n_p, N_PAIRS_D, DH_D, LANES), F32)
    o_p, st_p = rwkv_scan(rows, v, zero, tok0=0, t=t_p, nb=n_p)
    o_s, st_s = rwkv_scan(rows, v, _state_to_pairs(state), tok0=np_tok, t=t_s, nb=nb_sample)
    return (o_p.reshape(np_tok, DM_D), o_s.reshape(n_s * t_s, DM_D), gb,
            _state_from_pairs(st_p), _state_from_pairs(st_s))


EVEN_CHUNK = 64
SB_BLOCK = 256
SB_GROUP = 4
RWKV_SAMPLE_GROUP = 8


def kernel(x_prompt, x_sample, p_prompt, p_sample, cache_k, cache_v, page_table, state_gla, state_hgrn, state_rwkv, state_rwkv_shift, ln_ffn1, ffn1_w_gate, ffn1_w_up, ffn1_w_down, ln_mix, ln_ffn2, ffn2_w_gate, ffn2_w_up, ffn2_w_down, ln_ple, ple_w_gate, ple_w_proj, w_in_even, w_out_even, gla_w_gk, gla_b_gk, gla_norm, hgrn_lb_logits, hgrn_norm, w_in_odd, w_out_odd, sb_bias, rwkv_mu, rwkv_w0, rwkv_w2, rwkv_a0, rwkv_a2, rwkv_g2, rwkv_k_k, rwkv_k_a, rwkv_r_k, rwkv_ln_w, rwkv_ln_b, final_norm):
    n_p, t_p, _ = x_prompt.shape
    n_s, t_s, _ = x_sample.shape
    np_tok, ns_tok = n_p * t_p, n_s * t_s
    assert ns_tok == TOKEN_TILE and t_p % TOKEN_TILE == 0 and t_s < SUBLANES
    bf = lambda w: w.astype(BF16)
    row = lambda a: a.reshape(1, -1)

    x = jnp.concatenate([x_prompt.reshape(np_tok, D_MODEL), x_sample.reshape(ns_tok, D_MODEL)], axis=0)
    p_lead = p_prompt.reshape(DEPTH, np_tok, PLE_DIM)
    p_tail = p_sample.reshape(DEPTH, ns_tok, PLE_DIM)
    sm = jax.nn.softmax(hgrn_lb_logits.astype(F32), axis=0)
    lower_bounds = jnp.concatenate([jnp.zeros_like(sm[:1]), jnp.cumsum(sm[1:], axis=0)], axis=0)
    head_id = jnp.arange(DM_D) // DH_D
    seg = (head_id[:, None] == head_id[None, :]).astype(BF16)
    n_phys = cache_k.shape[1]
    ck = jnp.transpose(cache_k, (0, 1, 3, 4, 2)).reshape(cache_k.shape[0], n_phys, HALF, PAGE_SIZE)
    cv = jnp.transpose(cache_v, (0, 1, 3, 4, 2)).reshape(cache_v.shape[0], n_phys, HALF, PAGE_SIZE)
    pt_flat = page_table.reshape(-1)
    q0 = N_COLS_RWKV
    k0, v0 = q0 + HALF, q0 + 2 * HALF

    k_rows_s, v_rows_s = [], []
    n_odd = cache_k.shape[0]
    kv_t = ()

    def rows_from_feature_major(a):
        return jnp.transpose(a.reshape(n_odd, n_p, H_C, DH_C, t_p), (0, 1, 4, 2, 3))
    gla_p, hgrn_p = [], []
    gla_s = hgrn_s = None
    rwkv_p, rwkv_s, shift_p, shift_s = [], [], [], []
    for i in range(DEPTH):
        j = i // 2
        x = ffn(x, row(ln_ffn1[i]), bf(ffn1_w_gate[i]), bf(ffn1_w_up[i]), bf(ffn1_w_down[i]))
        if i % 2 == 0:
            w_in, wgk, bgk = even_weights(w_in_even[j], gla_w_gk[j], gla_b_gk[j])
            z = rms_matmul(x, row(ln_mix[i]), w_in)
            o, (ga_p, hb_p), gla_s, hgrn_s = run_even_mixer(
                z, n_p, t_p, n_s, t_s, state_gla, state_hgrn, j, gla_s, hgrn_s, wgk, bgk,
                lower_bounds[j], gla_norm[j], hgrn_norm[j], chunk=EVEN_CHUNK)
            x = matmul_residual(x, o, bf(w_out_even[j]))
            gla_p.append(ga_p)
            hgrn_p.append(hb_p)
        else:
            w = w_in_odd[j]
            z, qkv, *kv_t = odd_in_proj(
                x, row(ln_mix[i]), bf(jnp.concatenate([w[:, 3 * HALF:], w[:, :3 * HALF]], axis=1)),
                kv_t, (n_odd, n_p, HALF, t_p), j)
            oc_p = sb_attention_prompt(qkv, sb_bias[j], n_seq=n_p, t=t_p, q_col=0, blk=SB_BLOCK,
                                       group=SB_GROUP)
            z_s = z[np_tok:]
            q_s, kc_s, vc_s = (z_s[:, c:c + HALF].reshape(n_s, t_s, HALF) for c in (q0, k0, v0))
            bias_rows = jnp.broadcast_to(jnp.tile(sb_bias[j], t_s)[:, None], (t_s * H_C, PAGE_SIZE))
            oc_s = sb_attention_sample(q_s, kc_s, vc_s, ck, cv, j, pt_flat, bias_rows)
            od_p, od_s, gb, sd_p, sd_s = run_rwkv(
                z, n_p, t_p, n_s, t_s, state_rwkv[j], state_rwkv_shift[j], rwkv_mu[j], rwkv_w0[j],
                rwkv_w2[j], rwkv_a0[j], rwkv_a2[j], rwkv_g2[j], rwkv_k_k[j], rwkv_k_a[j],
                rwkv_r_k[j], seg, nb_sample=RWKV_SAMPLE_GROUP)
            x = odd_out(x, oc_p, oc_s.reshape(ns_tok, HALF), od_p, od_s, gb,
                        row(rwkv_ln_w[j]), row(rwkv_ln_b[j]), seg, bf(w_out_odd[j]))
            k_rows_s.append(kc_s.reshape(n_s, t_s, H_C, DH_C))
            v_rows_s.append(vc_s.reshape(n_s, t_s, H_C, DH_C))
            rwkv_p.append(sd_p)
            rwkv_s.append(sd_s)
            shift_p.append(jnp.concatenate(
                [lax.slice(z, ((b + 1) * t_p - 1, 0), ((b + 1) * t_p, N_COLS_RWKV)) for b in range(n_p)],
                axis=0))
            shift_s.append(z_s.reshape(n_s, t_s, -1)[:, -1, :N_COLS_RWKV])
        x = ffn(x, row(ln_ffn2[i]), bf(ffn2_w_gate[i]), bf(ffn2_w_up[i]), bf(ffn2_w_down[i]))
        x = ple(x, p_lead, p_tail, i, row(ln_ple[i]), bf(ple_w_gate[i]), bf(ple_w_proj[i]),
                final_gain=row(final_norm) if i == DEPTH - 1 else None)

    y_p, y_s = x
    return (y_p.reshape(n_p, t_p, D_MODEL), y_s.reshape(n_s, t_s, D_MODEL),
            rows_from_feature_major(kv_t[0]), rows_from_feature_major(kv_t[1]),
            jnp.stack(k_rows_s), jnp.stack(v_rows_s),
            jnp.stack(gla_p), gla_s, jnp.stack(hgrn_p), hgrn_s,
            jnp.stack(rwkv_p), jnp.stack(rwkv_s), jnp.stack(shift_p), jnp.stack(shift_s))
```

```python
import functools

import jax
import jax.numpy as jnp
from jax import lax
from jax.experimental import pallas as pl
from jax.experimental.pallas import tpu as pltpu

F32 = jnp.float32
BF16 = jnp.bfloat16

D_MODEL = 1024
DEPTH = 4
PAGE_SIZE = 128
HALF = D_MODEL // 2
D_FF = 2816
PLE_DIM = 256
RMS_EPS = 1e-6
H_A = 4
DK_A = HALF // H_A // 2
DV_A = HALF // H_A
GLA_GK_RANK = 16
GLA_GK_NORM = 16.0
H_B = 4
EXP_B = 128
DV_B = HALF // H_B
H_C = 8
DH_C = HALF // H_C
H_D = 8
DH_D = HALF // H_D
DM_D = H_D * DH_D
R_W = 64
R_A = 64
R_G = 128
RWKV_LN_EPS = 64e-5
N_COLS_RWKV = 3 * DM_D + R_W + R_A + R_G
N_COLS_ODD = 3 * HALF + N_COLS_RWKV

LANES = 128
SUBLANES = 8
MXU_DIM = 256
VMEM_LIMIT_BYTES = 56 * 1024 * 1024

TOKEN_TILE = 512
FFN_TOKEN_TILE = 768
N_EVEN_HEADS = H_A + H_B
N_COLS_EVEN_PAD = (4 * N_EVEN_HEADS + 1) * LANES
GATED_SUB_BLOCK = 16


def _params(*semantics):
    return pltpu.CompilerParams(dimension_semantics=semantics,
                                vmem_limit_bytes=VMEM_LIMIT_BYTES)


def _resident(shape):
    return pl.BlockSpec(shape, lambda *_: (0,) * len(shape), pipeline_mode=pl.Buffered(1))


def _rms(x, g):
    return x * lax.rsqrt(jnp.mean(x * x, axis=-1, keepdims=True) + RMS_EPS) * g


def _dot(a, b):
    return jnp.dot(a, b, preferred_element_type=F32)


def _ffn_kernel(x_ref, ln_ref, wg_ref, wu_ref, wd_ref, o_ref):
    x = x_ref[...]
    h = _rms(x, ln_ref[...]).astype(BF16)
    acc = jnp.zeros(x.shape, F32)
    for f0 in range(0, D_FF, MXU_DIM):
        g = _dot(h, wg_ref[:, f0:f0 + MXU_DIM])
        u = _dot(h, wu_ref[:, f0:f0 + MXU_DIM])
        a = (g * jax.nn.sigmoid(g) * u).astype(BF16)
        acc = acc + _dot(a, wd_ref[f0:f0 + MXU_DIM, :])
    o_ref[...] = x + 0.5 * acc


def ffn(x, ln, wg, wu, wd):
    nt = x.shape[0]
    tile = FFN_TOKEN_TILE if nt % FFN_TOKEN_TILE == 0 else TOKEN_TILE
    tok = pl.BlockSpec((tile, D_MODEL), lambda i: (i, 0))
    return pl.pallas_call(
        _ffn_kernel,
        grid=(nt // tile,),
        in_specs=[tok, _resident((1, D_MODEL)), _resident((D_MODEL, D_FF)),
                  _resident((D_MODEL, D_FF)), _resident((D_FF, D_MODEL))],
        out_specs=tok,
        out_shape=jax.ShapeDtypeStruct(x.shape, F32),
        compiler_params=_params("parallel"),
        name="ffn",
    )(x, ln, wg, wu, wd)


def _rms_matmul_kernel(x_ref, ln_ref, w_ref, o_ref, *maybe_ob_ref, n_chunk, bf16_from):
    h = _rms(x_ref[...], ln_ref[...]).astype(BF16)
    n = w_ref.shape[1]
    for n0 in range(0, n, n_chunk):
        n1 = min(n0 + n_chunk, n)
        o_ref[:, n0:n1] = _dot(h, w_ref[:, n0:n1])
    if maybe_ob_ref:
        maybe_ob_ref[0][...] = o_ref[:, bf16_from:].astype(BF16)


def rms_matmul(x, ln, w, bf16_from=None):
    nt, n = x.shape[0], w.shape[1]
    out_specs = [pl.BlockSpec((TOKEN_TILE, n), lambda i: (i, 0))]
    out_shape = [jax.ShapeDtypeStruct((nt, n), F32)]
    if bf16_from is not None:
        out_specs.append(pl.BlockSpec((TOKEN_TILE, n - bf16_from), lambda i: (i, 0)))
        out_shape.append(jax.ShapeDtypeStruct((nt, n - bf16_from), BF16))
    out = pl.pallas_call(
        functools.partial(_rms_matmul_kernel, n_chunk=2 * MXU_DIM, bf16_from=bf16_from),
        grid=(nt // TOKEN_TILE,),
        in_specs=[pl.BlockSpec((TOKEN_TILE, D_MODEL), lambda i: (i, 0)),
                  _resident((1, D_MODEL)), _resident((D_MODEL, n))],
        out_specs=out_specs,
        out_shape=out_shape,
        compiler_params=_params("parallel"),
        name="rms_matmul",
    )(x, ln, w)
    return out[0] if bf16_from is None else out


def _odd_in_proj_kernel(x_ref, ln_ref, w_ref, *refs, n_chunk, q0, n_feature_major_tiles, n_aliased):
    o_ref, ob_ref, kt_ref, vt_ref = refs[n_aliased:]
    h = _rms(x_ref[...], ln_ref[...]).astype(BF16)
    n = w_ref.shape[1]
    for n0 in range(0, n, n_chunk):
        n1 = min(n0 + n_chunk, n)
        o_ref[:, n0:n1] = _dot(h, w_ref[:, n0:n1])
    ob_ref[...] = o_ref[:, q0:].astype(BF16)

    @pl.when(pl.program_id(0) < n_feature_major_tiles)
    def _():
        kt_ref[...] = o_ref[:, q0 + HALF:q0 + 2 * HALF].T
        vt_ref[...] = o_ref[:, q0 + 2 * HALF:q0 + 3 * HALF].T


def odd_in_proj(x, ln, w, bufs, buf_shape, layer):
    nt, n = x.shape[0], w.shape[1]
    q0 = n - 3 * HALF
    _, n_seq, _, t = buf_shape
    tiles_per_seq = t // TOKEN_TILE
    n_fm = n_seq * tiles_per_seq
    tok = lambda ncol: pl.BlockSpec((TOKEN_TILE, ncol), lambda i: (i, 0))

    def fm_index(i):
        i = jnp.minimum(i, n_fm - 1)
        return layer, i // tiles_per_seq, 0, i % tiles_per_seq

    fm = pl.BlockSpec((None, None, HALF, TOKEN_TILE), fm_index)
    any_spec = pl.BlockSpec(memory_space=pl.ANY)
    return pl.pallas_call(
        functools.partial(_odd_in_proj_kernel, n_chunk=2 * MXU_DIM, q0=q0,
                          n_feature_major_tiles=n_fm, n_aliased=len(bufs)),
        grid=(nt // TOKEN_TILE,),
        in_specs=[tok(D_MODEL), _resident((1, D_MODEL)), _resident((D_MODEL, n))]
        + [any_spec] * len(bufs),
        out_specs=[tok(n), tok(n - q0), fm, fm],
        out_shape=[jax.ShapeDtypeStruct((nt, n), F32), jax.ShapeDtypeStruct((nt, n - q0), BF16),
                   jax.ShapeDtypeStruct(buf_shape, F32), jax.ShapeDtypeStruct(buf_shape, F32)],
        input_output_aliases={3 + k: 2 + k for k in range(len(bufs))},
        compiler_params=_params("arbitrary"),
        name="odd_in_proj",
    )(x, ln, w, *bufs)


def _matmul_residual_kernel(x_ref, a_ref, a_tail_ref, w_ref, o_ref, *, n_lead_tiles):
    a = jnp.where(pl.program_id(0) >= n_lead_tiles, a_tail_ref[...], a_ref[...])
    o_ref[...] = x_ref[...] + _dot(a.astype(BF16), w_ref[...])


def matmul_residual(x, a, a_tail, w):
    nt = x.shape[0]
    k = a.shape[1]
    n_lead_tiles = a.shape[0] // TOKEN_TILE
    assert nt == a.shape[0] + TOKEN_TILE and a_tail.shape[0] == TOKEN_TILE
    return pl.pallas_call(
        functools.partial(_matmul_residual_kernel, n_lead_tiles=n_lead_tiles),
        grid=(nt // TOKEN_TILE,),
        in_specs=[pl.BlockSpec((TOKEN_TILE, D_MODEL), lambda i: (i, 0)),
                  pl.BlockSpec((TOKEN_TILE, k), lambda i: (jnp.minimum(i, n_lead_tiles - 1), 0)),
                  _resident((TOKEN_TILE, k)), _resident((k, D_MODEL))],
        out_specs=pl.BlockSpec((TOKEN_TILE, D_MODEL), lambda i: (i, 0)),
        out_shape=jax.ShapeDtypeStruct(x.shape, F32),
        compiler_params=_params("parallel"),
        name="matmul_residual",
    )(x, a, a_tail, w)


def _ple_kernel(x_ref, p_ref, p_tail_ref, ln_ref, wg_ref, wp_ref, *refs, n_lead_tiles, final):
    tail = pl.program_id(0) >= n_lead_tiles
    x = x_ref[...]
    h = _rms(x, ln_ref[...]).astype(BF16)
    gate = jax.nn.sigmoid(_dot(h, wg_ref[...]))
    p = jnp.where(tail, p_tail_ref[...], p_ref[...]).astype(BF16)
    o = x + gate * _dot(p, wp_ref[...])
    if not final:
        refs[0][...] = o
        return
    fn_ref, y_ref, y_tail_ref = refs
    y = _rms(o, fn_ref[...])

    @pl.when(jnp.logical_not(tail))
    def _():
        y_ref[...] = y

    @pl.when(tail)
    def _():
        y_tail_ref[...] = y


def ple(x, p_lead, p_tail, layer, ln, wg, wp, final_gain=None):
    nt = x.shape[0]
    n_lead = p_lead.shape[1]
    n_lead_tiles = n_lead // TOKEN_TILE
    assert nt == n_lead + TOKEN_TILE and p_tail.shape[1] == TOKEN_TILE
    tok = pl.BlockSpec((TOKEN_TILE, D_MODEL), lambda i: (i, 0))
    lead = lambda n: pl.BlockSpec((TOKEN_TILE, n), lambda i: (jnp.minimum(i, n_lead_tiles - 1), 0))
    in_specs = [tok,
                pl.BlockSpec((None, TOKEN_TILE, PLE_DIM),
                             lambda i: (layer, jnp.minimum(i, n_lead_tiles - 1), 0)),
                pl.BlockSpec((None, TOKEN_TILE, PLE_DIM), lambda i: (layer, 0, 0)),
                _resident((1, D_MODEL)), _resident((D_MODEL, D_MODEL)), _resident((PLE_DIM, D_MODEL))]
    args = [x, p_lead, p_tail, ln, wg, wp]
    final = final_gain is not None
    if final:
        in_specs.append(_resident((1, D_MODEL)))
        args.append(final_gain)
        out_specs = [lead(D_MODEL), pl.BlockSpec((TOKEN_TILE, D_MODEL), lambda i: (0, 0))]
        out_shape = [jax.ShapeDtypeStruct((n_lead, D_MODEL), F32),
                     jax.ShapeDtypeStruct((TOKEN_TILE, D_MODEL), F32)]
    else:
        out_specs, out_shape = tok, jax.ShapeDtypeStruct(x.shape, F32)
    return pl.pallas_call(
        functools.partial(_ple_kernel, n_lead_tiles=n_lead_tiles, final=final),
        grid=(nt // TOKEN_TILE,),
        in_specs=in_specs,
        out_specs=out_specs,
        out_shape=out_shape,
        compiler_params=_params("arbitrary" if final else "parallel"),
        name="ple",
    )(*args)


def _log_sigmoid(x):
    return jnp.minimum(x, 0.0) - jnp.log1p(jnp.exp(-jnp.abs(x)))


def _split_bf16(x):
    hi = x.astype(BF16)
    lo = (x - hi.astype(F32)).astype(BF16)
    return hi, lo


def _gated_chunks(heads, st_ref, tri):
    c = heads[0][0].shape[0]
    n = len(heads)
    sub = min(c, GATED_SUB_BLOCK)
    row = lax.broadcasted_iota(jnp.int32, (sub, 1), 0)
    tn_dims = (((0,), (0,)), ((), ()))

    bs = []
    for q, k, v, g in heads:
        g_hi, g_lo = _split_bf16(g * LOG2E)
        bs.append(_dot(tri, g_hi) + _dot(tri, g_lo))
    sts = [st_ref[h] for h in range(n)]
    o_inter = [lax.dot_general((heads[h][0] * jnp.exp2(bs[h])).astype(BF16), sts[h].astype(BF16),
                               _NT_DIMS, preferred_element_type=F32) for h in range(n)]

    scores = [[None] * (c // sub) for _ in range(n)]
    for h, (q, k, v, g) in enumerate(heads):
        b = bs[h]
        for i, r0 in enumerate(range(sub, c, sub), start=1):
            b_ref = b[r0:r0 + 1, :]
            q_rel = (q[r0:r0 + sub, :] * jnp.exp2(b[r0:r0 + sub, :] - b_ref)).astype(BF16)
            k_rel = (k[:r0, :] * jnp.exp2(b_ref - b[:r0, :])).astype(BF16)
            scores[h][i] = lax.dot_general(q_rel, k_rel, _NT_DIMS, preferred_element_type=F32)
    off_diag = [[None] * (c // sub) for _ in range(n)]
    for h, (q, k, v, g) in enumerate(heads):
        for i, r0 in enumerate(range(sub, c, sub), start=1):
            off_diag[h][i] = _dot(scores[h][i].astype(BF16), v[:r0, :].astype(BF16))

    for h, (q, k, v, g) in enumerate(heads):
        b_last = bs[h][c - 1:c, :]
        ke = (k * jnp.exp2(b_last - bs[h])).astype(BF16)
        st_ref[h] = sts[h] * jnp.exp2(b_last) + lax.dot_general(
            v.astype(BF16), ke, tn_dims, preferred_element_type=F32)

    outs = []
    for h, (q, k, v, g) in enumerate(heads):
        parts = []
        for i, r0 in enumerate(range(0, c, sub)):
            ki, vi, bi = (a[r0:r0 + sub, :] for a in (k, v, bs[h]))
            oi = o_inter[h][r0:r0 + sub, :]
            if i > 0:
                oi = oi + off_diag[h][i]
            for p0 in range(0, sub, SUBLANES):
                qp = q[r0 + p0:r0 + p0 + SUBLANES, :]
                bp = bi[p0:p0 + SUBLANES, :]
                op = oi[p0:p0 + SUBLANES, :]
                for s in range(min(sub, p0 + SUBLANES)):
                    d = bp - bi[s:s + 1, :]
                    if s > p0:
                        d = jnp.where(row[p0:p0 + SUBLANES] >= s, d, -1e30)
                    a_s = jnp.sum(qp * ki[s:s + 1, :] * jnp.exp2(d), axis=-1, keepdims=True)
                    op = op + a_s * vi[s:s + 1, :]
                parts.append(op)
        outs.append(jnp.concatenate(parts, axis=0) if len(parts) > 1 else parts[0])
    return outs


def _even_heads(z_ref, wgk_ref, bgk_ref, lb_ref, na_ref, nb_ref):
    def blk(i):
        return z_ref[:, i * LANES:(i + 1) * LANES]

    heads, gains, gates = [], [], []
    gk = _log_sigmoid(_dot(blk(4 * N_EVEN_HEADS).astype(BF16), wgk_ref[...]) + bgk_ref[...])
    gk = gk * (1.0 / GLA_GK_NORM)
    for h in range(H_A):
        heads.append((blk(h) * DK_A ** -0.5, blk(H_A + h), blk(2 * H_A + h),
                      gk[:, h * LANES:(h + 1) * LANES]))
        gains.append(na_ref[...])
        gates.append(blk(3 * H_A + h))
    base = 4 * H_A
    for h in range(H_B):
        lb = lb_ref[:, h * LANES:(h + 1) * LANES]
        fb = blk(base + H_B + h)
        a = jnp.log1p(-lb) + _log_sigmoid(fb)
        log_lb = jnp.log(lb)
        g = jnp.maximum(a, log_lb) + jnp.log1p(jnp.exp(-jnp.abs(a - log_lb)))
        heads.append((blk(base + h), (1.0 - lb) * jax.nn.sigmoid(-fb), blk(base + 2 * H_B + h), g))
        gains.append(nb_ref[...])
        gates.append(blk(base + 3 * H_B + h))
    return heads, gains, gates


def _even_finish(o, gain, gate):
    o = o * lax.rsqrt(jnp.mean(o * o, axis=-1, keepdims=True) + RMS_EPS) * gain
    return o * (gate * jax.nn.sigmoid(gate))


def _lower_tri(c):
    return (lax.broadcasted_iota(jnp.int32, (c, c), 0)
            >= lax.broadcasted_iota(jnp.int32, (c, c), 1)).astype(BF16)


def _even_mixer_kernel(z_ref, s0_ref, wgk_ref, bgk_ref, lb_ref, na_ref, nb_ref,
                       o_ref, sout_ref, st_ref):
    ci = pl.program_id(1)

    @pl.when(ci == 0)
    def _():
        st_ref[...] = s0_ref[0]

    heads, gains, gates = _even_heads(z_ref, wgk_ref, bgk_ref, lb_ref, na_ref, nb_ref)
    for h, o in enumerate(_gated_chunks(heads, st_ref, _lower_tri(z_ref.shape[0]))):
        o_ref[:, h * LANES:(h + 1) * LANES] = _even_finish(o, gains[h], gates[h])

    @pl.when(ci == pl.num_programs(1) - 1)
    def _():
        sout_ref[0] = st_ref[...]


def _even_mixer_short_kernel(z_ref, sg_ref, sh_ref, wgk_ref, bgk_ref, lb_ref, na_ref, nb_ref,
                             *refs, t_s, n_aliased):
    o_ref, sg_out_ref, sh_out_ref, st_ref = refs[n_aliased:]
    c = z_ref.shape[0]
    spb = c // t_s
    row = lax.broadcasted_iota(jnp.int32, (c, 1), 0)
    zeros = jnp.zeros((LANES, LANES - DK_A), F32)
    for q in range(spb):
        for h in range(H_A):
            st_ref[q * N_EVEN_HEADS + h] = jnp.concatenate([sg_ref[q, h].T, zeros], axis=1)
        for h in range(H_B):
            st_ref[q * N_EVEN_HEADS + H_A + h] = sh_ref[q, h].T

    heads, gains, gates = _even_heads(z_ref, wgk_ref, bgk_ref, lb_ref, na_ref, nb_ref)
    all_heads = []
    for q in range(spb):
        mine = (row // t_s) == q
        all_heads += [(hq, jnp.where(mine, hk, 0.0), hv, jnp.where(mine, hg, 0.0))
                      for hq, hk, hv, hg in heads]
    outs = _gated_chunks(all_heads, st_ref, _lower_tri(c))
    for h in range(N_EVEN_HEADS):
        o = outs[h]
        for q in range(1, spb):
            o = jnp.where((row // t_s) == q, outs[q * N_EVEN_HEADS + h], o)
        o_ref[:, h * LANES:(h + 1) * LANES] = _even_finish(o, gains[h], gates[h])

    for q in range(spb):
        for h in range(H_A):
            sg_out_ref[q, h] = st_ref[q * N_EVEN_HEADS + h].T[:DK_A, :]
        for h in range(H_B):
            sh_out_ref[q, h] = st_ref[q * N_EVEN_HEADS + H_A + h].T


def even_mixer(z, s0, wgk, bgk, lb, norm_a, norm_b, *, n_seq, t, chunk):
    nt, ncol = n_seq * t, z.shape[1]
    n_chunks = t // chunk
    st_shape = (N_EVEN_HEADS, LANES, LANES)
    return pl.pallas_call(
        _even_mixer_kernel,
        grid=(n_seq, n_chunks),
        in_specs=[pl.BlockSpec((chunk, ncol), lambda b, c: (b * n_chunks + c, 0)),
                  pl.BlockSpec((1,) + st_shape, lambda b, c: (b, 0, 0, 0)),
                  _resident(wgk.shape), _resident(bgk.shape), _resident(lb.shape),
                  _resident(norm_a.shape), _resident(norm_b.shape)],
        out_specs=[pl.BlockSpec((chunk, D_MODEL), lambda b, c: (b * n_chunks + c, 0)),
                   pl.BlockSpec((1,) + st_shape, lambda b, c: (b, 0, 0, 0))],
        out_shape=[jax.ShapeDtypeStruct((nt, D_MODEL), F32),
                   jax.ShapeDtypeStruct((n_seq,) + st_shape, F32)],
        scratch_shapes=[pltpu.VMEM(st_shape, F32)],
        compiler_params=_params("parallel", "arbitrary"),
        name="even_mixer",
    )(z, s0, wgk, bgk, lb, norm_a, norm_b)


def even_mixer_short(z, state_gla, state_hgrn, layer, wgk, bgk, lb, norm_a, norm_b,
                     gla_buf, hgrn_buf, *, tok0, n_seq, t_s):
    ncol = z.shape[1]
    spb = SUBLANES // t_s

    def st_spec(a):
        return pl.BlockSpec((None, spb) + a.shape[2:], lambda g: (layer, g, 0, 0, 0))

    any_spec = pl.BlockSpec(memory_space=pl.ANY)
    bufs = [gla_buf, hgrn_buf]
    n_fixed = 8
    return pl.pallas_call(
        functools.partial(_even_mixer_short_kernel, t_s=t_s, n_aliased=len(bufs)),
        grid=(n_seq // spb,),
        in_specs=[pl.BlockSpec((SUBLANES, ncol), lambda g: (tok0 // SUBLANES + g, 0)),
                  st_spec(state_gla), st_spec(state_hgrn),
                  _resident(wgk.shape), _resident(bgk.shape), _resident(lb.shape),
                  _resident(norm_a.shape), _resident(norm_b.shape)] + [any_spec] * len(bufs),
        out_specs=[pl.BlockSpec((SUBLANES, D_MODEL), lambda g: (g, 0)),
                   st_spec(state_gla), st_spec(state_hgrn)],
        out_shape=[jax.ShapeDtypeStruct((n_seq * t_s, D_MODEL), F32),
                   jax.ShapeDtypeStruct(state_gla.shape, F32),
                   jax.ShapeDtypeStruct(state_hgrn.shape, F32)],
        input_output_aliases={n_fixed + k: 1 + k for k in range(len(bufs))},
        scratch_shapes=[pltpu.VMEM((spb * N_EVEN_HEADS, LANES, LANES), F32)],
        compiler_params=_params("parallel"),
        name="even_mixer_short",
    )(z, state_gla, state_hgrn, wgk, bgk, lb, norm_a, norm_b, *bufs)


def _pad_heads(w, n_heads):
    d = w.shape[-1] // n_heads
    w = w.reshape(w.shape[:-1] + (n_heads, d))
    w = jnp.pad(w, [(0, 0)] * (w.ndim - 1) + [(0, LANES - d)])
    return w.reshape(w.shape[:-2] + (n_heads * LANES,))


def _split_cols(w, sizes):
    out, o = [], 0
    for s in sizes:
        out.append(w[..., o:o + s])
        o += s
    return out


def even_weights(w_in, w_gk, b_gk):
    ka = H_A * DK_A
    qa, kk, va, gk_in, ga, qb, fb, ib, gb = _split_cols(
        w_in, (ka, ka, HALF, GLA_GK_RANK, HALF, HALF, HALF, HALF, HALF))
    gk_in = jnp.pad(gk_in, ((0, 0), (0, LANES - GLA_GK_RANK)))
    w = jnp.concatenate([_pad_heads(qa, H_A), _pad_heads(kk, H_A), va, ga, qb, fb, ib, gb, gk_in],
                        axis=-1).astype(BF16)
    wgk = jnp.pad(_pad_heads(w_gk, H_A), ((0, LANES - GLA_GK_RANK), (0, 0))).astype(BF16)
    bgk = _pad_heads(b_gk[None, :], H_A)
    return w, wgk, bgk


def run_even_mixer(z, n_p, t_p, n_s, t_s, state_gla, state_hgrn, layer, gla_buf, hgrn_buf,
                   wgk, bgk, lb, norm_a, norm_b, *, chunk):
    args = (wgk, bgk, lb[None, :], norm_a[None, :], norm_b[None, :])
    zero = jnp.zeros((n_p, N_EVEN_HEADS, LANES, LANES), F32)
    o_p, st_p = even_mixer(z, zero, *args, n_seq=n_p, t=t_p, chunk=chunk)
    o_s, gla_buf, hgrn_buf = even_mixer_short(z, state_gla, state_hgrn, layer, *args,
                                              gla_buf, hgrn_buf,
                                              tok0=n_p * t_p, n_seq=n_s, t_s=t_s)
    gla_p = jnp.swapaxes(st_p[:, :H_A], -1, -2)[:, :, :DK_A, :]
    hgrn_p = jnp.swapaxes(st_p[:, H_A:], -1, -2)
    return o_p, o_s, (gla_p, hgrn_p), gla_buf, hgrn_buf


_NT_DIMS = (((1,), (1,)), ((), ()))
LOG2E = 1.4426950408889634


def _neg_softplus(z):
    return -(jnp.maximum(z, 0.0) + jnp.log1p(jnp.exp(-jnp.abs(z))))


def _suffix_tri(n):
    return (lax.broadcasted_iota(jnp.int32, (n, n), 0)
            >= lax.broadcasted_iota(jnp.int32, (n, n), 1)).astype(BF16)


def _sb_prompt_kernel(bias_ref, q_ref, k_ref, v_ref, o_ref, u_ref, tot_ref, carry_ref, acc_ref,
                      *, blk, group):
    pair = pl.program_id(1)
    qi = pl.program_id(2)
    lane = lax.broadcasted_iota(jnp.int32, (1, LANES), 1)
    head_a = lane < DH_C
    q2 = q_ref[...].astype(F32) * (DH_C ** -0.5 * LOG2E)
    qs = jnp.concatenate([jnp.where(head_a, q2, 0.0), jnp.where(head_a, 0.0, q2)], axis=0).astype(BF16)
    row = lax.broadcasted_iota(jnp.int32, (2 * blk, 1), 0)
    bias2 = jnp.where(row < blk, bias_ref[2 * pair], bias_ref[2 * pair + 1]) * LOG2E
    neg_tri = -_suffix_tri(blk)
    strict = lax.broadcasted_iota(jnp.int32, (2 * blk, blk), 1) < (row % blk)

    def rows_of(ref, j):
        return ref[pl.ds(pl.multiple_of(j * blk, blk), blk), :]

    def scores(slot, j, diagonal=False):
        b2 = jnp.where(j >= 0, bias2, -1e30)
        z2 = lax.dot_general(qs, rows_of(k_ref, jnp.maximum(j, 0)), _NT_DIMS,
                             preferred_element_type=F32) + b2
        if diagonal:
            z2 = jnp.where(strict, z2, -1e30)
        sp2 = jnp.maximum(z2, 0.0) + jnp.log2(1.0 + jnp.exp2(-jnp.abs(z2)))
        incl = _dot(sp2.astype(BF16), neg_tri)
        u_ref[slot] = z2 + incl
        tot_ref[slot] = incl[:, 0:1]

    def accumulate(slot, j):
        w = jnp.exp2(u_ref[slot] + carry_ref[...])
        acc_ref[...] += _dot(w.astype(BF16), rows_of(v_ref, jnp.maximum(j, 0)))
        carry_ref[...] += tot_ref[slot]

    def body(k, _):
        j = qi - group * k
        for s in range(group):
            accumulate(s, j + group - s)
        for s in range(group):
            scores(s, j - s)
        return 0

    carry_ref[...] = jnp.zeros(carry_ref.shape, F32)
    acc_ref[...] = jnp.zeros(acc_ref.shape, F32)
    for s in range(group):
        scores(s, qi - s, diagonal=(s == 0))
    lax.fori_loop(1, qi // group + 1, body, 0)
    for s in range(group):
        accumulate(s, qi % group - s)
    o_ref[...] = jnp.where(head_a, acc_ref[:blk, :], acc_ref[blk:, :])


def sb_attention_prompt(z, bias, *, n_seq, t, q_col, blk, group):
    nq = t // blk
    n_pairs = H_C // 2
    return pl.pallas_call(
        functools.partial(_sb_prompt_kernel, blk=blk, group=group),
        grid=(n_seq, n_pairs, nq),
        in_specs=[pl.BlockSpec(memory_space=pltpu.SMEM),
                  pl.BlockSpec((blk, LANES), lambda b, p, i: (b * nq + i, q_col + p)),
                  pl.BlockSpec((t, LANES), lambda b, p, i: (b, q_col + n_pairs + p)),
                  pl.BlockSpec((t, LANES), lambda b, p, i: (b, q_col + 2 * n_pairs + p))],
        out_specs=pl.BlockSpec((blk, LANES), lambda b, p, i: (b * nq + i, p)),
        out_shape=jax.ShapeDtypeStruct((n_seq * t, HALF), F32),
        scratch_shapes=[pltpu.VMEM((group, 2 * blk, blk), F32), pltpu.VMEM((group, 2 * blk, 1), F32),
                        pltpu.VMEM((2 * blk, 1), F32), pltpu.VMEM((2 * blk, LANES), F32)],
        compiler_params=_params("parallel", "parallel", "arbitrary"),
        name="sb_attention_prompt",
    )(bias, z, z, z)


def _sb_sample_kernel(pt_ref, q_ref, kn_ref, vn_ref, bias_ref, *refs, n_pages, t_s):
    del pt_ref
    k_pages, v_pages, o_ref = refs[:n_pages], refs[n_pages:2 * n_pages], refs[2 * n_pages]
    rows = t_s * H_C
    q = q_ref[0] * DH_C ** -0.5
    qx = jnp.concatenate([jnp.broadcast_to(q[i:i + 1, :], (H_C, HALF)) for i in range(t_s)], axis=0)
    r_id = lax.broadcasted_iota(jnp.int32, (rows, HALF), 0)
    l_id = lax.broadcasted_iota(jnp.int32, (rows, HALF), 1)
    head_lanes = (l_id // DH_C) == (r_id % H_C)
    qx = jnp.where(head_lanes, qx, 0.0)
    bias = bias_ref[...]
    q_idx = lax.broadcasted_iota(jnp.int32, (rows, 1), 0) // H_C

    carry = jnp.zeros((rows, 1), F32)
    acc = jnp.zeros((rows, HALF), F32)
    kn, vn = kn_ref[0], vn_ref[0]
    for j in reversed(range(t_s)):
        visible = q_idx > j
        z = jnp.sum(qx * kn[j:j + 1, :], axis=-1, keepdims=True) + bias[:, 0:1]
        lk = jnp.where(visible, _neg_softplus(z), 0.0)
        w = jnp.where(visible, jnp.exp(z + lk + carry), 0.0)
        acc = acc + w * vn[j:j + 1, :]
        carry = carry + lk

    qx = qx.astype(BF16)
    tri = _suffix_tri(PAGE_SIZE)
    zs = [_dot(qx, k_pages[p][...].astype(BF16)) + bias for p in range(n_pages)]
    incls = [_dot(_neg_softplus(z).astype(BF16), tri) for z in zs]
    ws = [None] * n_pages
    for p in reversed(range(n_pages)):
        ws[p] = jnp.exp(zs[p] + incls[p] + carry).astype(BF16)
        carry = carry + incls[p][:, 0:1]
    for p in range(n_pages):
        acc = acc + lax.dot_general(ws[p], v_pages[p][...].astype(BF16), _NT_DIMS,
                                    preferred_element_type=F32)

    acc = jnp.where(head_lanes, acc, 0.0)
    for i in range(t_s):
        o_ref[0, i:i + 1, :] = jnp.sum(acc[i * H_C:(i + 1) * H_C, :], axis=0, keepdims=True)


def sb_attention_sample(q, k_new, v_new, cache_k, cache_v, layer, page_table, bias_rows):
    n_seq, t_s, _ = q.shape
    n_pages = page_table.shape[0] // n_seq
    tok = pl.BlockSpec((1, t_s, HALF), lambda b, pt: (b, 0, 0))

    def page_spec(p):
        return pl.BlockSpec((None, None, HALF, PAGE_SIZE),
                            lambda b, pt: (layer, pt[b * n_pages + p], 0, 0))

    pages = [page_spec(p) for p in range(n_pages)]
    return pl.pallas_call(
        functools.partial(_sb_sample_kernel, n_pages=n_pages, t_s=t_s),
        grid_spec=pltpu.PrefetchScalarGridSpec(
            num_scalar_prefetch=1,
            grid=(n_seq,),
            in_specs=[tok, tok, tok,
                      pl.BlockSpec(bias_rows.shape, lambda b, pt: (0, 0))] + pages + pages,
            out_specs=tok),
        out_shape=jax.ShapeDtypeStruct((n_seq, t_s, HALF), F32),
        compiler_params=_params("parallel"),
        name="sb_attention_sample",
    )(page_table, q, k_new, v_new, bias_rows, *([cache_k] * n_pages), *([cache_v] * n_pages))


N_SCAN_ROWS = 13
N_SCAN_STEP_ROWS = 9
RWKV_STEPS_PER_TRIP = 8
N_PAIRS_D = H_D // 2


def _head_sum(x, seg):
    hi, lo = _split_bf16(x)
    return _dot(hi, seg) + _dot(lo, seg)


def _rwkv_prep_kernel(zd_ref, zb_ref, prev_s_ref, mu_ref, w0_ref, w2_ref, a0_ref, a2_ref, g2_ref,
                      kk_ref, ka_ref, rk_ref, seg_ref, rows_ref, v_ref, gb_ref,
                      *, tiles_per_seq, n_prompt_tiles):
    i = pl.program_id(0)
    z = zd_ref[...]
    tm = z.shape[0]
    boundary = jnp.where(i % tiles_per_seq == 0, 0.0, 1.0) * zb_ref[SUBLANES - 1:SUBLANES, :]
    row = lax.broadcasted_iota(jnp.int32, (tm, 1), 0)
    prev = jnp.where(row == 0, boundary, pltpu.roll(z, 1, axis=0))
    prev = jnp.where(i >= n_prompt_tiles, prev_s_ref[...], prev)
    zs = z + (prev - z) * mu_ref[...]
    r, k, v = zs[:, 0:DM_D], zs[:, DM_D:2 * DM_D], zs[:, 2 * DM_D:3 * DM_D]
    u = zs[:, 3 * DM_D:3 * DM_D + R_W + R_A]
    g_in = zs[:, 3 * DM_D + R_W + R_A:]
    seg = seg_ref[...]
    w = w0_ref[...] + _dot(jnp.tanh(u).astype(BF16), w2_ref[...])
    decay = jnp.exp(-jnp.exp(_log_sigmoid(w) - 0.5))
    a = jax.nn.sigmoid(a0_ref[...] + _dot(u.astype(BF16), a2_ref[...]))
    g = _dot(jax.nn.sigmoid(g_in).astype(BF16), g2_ref[...])
    kk = k * kk_ref[...]
    kk = kk * lax.rsqrt(jnp.maximum(_head_sum(kk * kk, seg), 1e-24))
    kp = k * (1.0 + (a - 1.0) * ka_ref[...])
    beta = kk * a
    wr = decay * r
    odd = (row % 2) == 1
    w_a, beta_a, kp_a = (pltpu.roll(t, 1, axis=0) for t in (decay, beta, kp))
    outs = (jnp.where(odd, w_a * kk, kk), jnp.where(odd, w_a * wr, wr),
            w_a * decay, beta_a * decay, kp_a * decay, beta, kp,
            _head_sum(beta_a * kk, seg), _head_sum(kp_a * kk, seg),
            jnp.where(odd, _head_sum(beta_a * wr, seg), 0.0),
            jnp.where(odd, _head_sum(kp_a * wr, seg), 0.0),
            _head_sum(beta * r, seg), _head_sum(kp * r, seg))
    for n, val in enumerate(outs):
        rows_ref[:, n * DM_D:(n + 1) * DM_D] = val
    v_ref[...] = v
    gb_ref[:, 0:DM_D] = g
    gb_ref[:, DM_D:] = _head_sum(r * kp * rk_ref[...], seg) * v


def rwkv_prep(z, prev_s, mu, w0, w2, a0, a2, g2, k_k, k_a, r_k, seg, *, n_prompt_tok, t_prompt):
    nt = z.shape[0]
    tm = TOKEN_TILE
    vec = _resident((1, DM_D))
    tok = lambda n: pl.BlockSpec((tm, n), lambda i: (i, 0))
    return pl.pallas_call(
        functools.partial(_rwkv_prep_kernel, tiles_per_seq=t_prompt // tm,
                          n_prompt_tiles=n_prompt_tok // tm),
        grid=(nt // tm,),
        in_specs=[tok(N_COLS_RWKV),
                  pl.BlockSpec((SUBLANES, N_COLS_RWKV),
                               lambda i: (jnp.maximum(i * (tm // SUBLANES) - 1, 0), 0)),
                  _resident(prev_s.shape), _resident((1, N_COLS_RWKV)),
                  vec, _resident(w2.shape), vec, _resident(a2.shape), _resident(g2.shape),
                  vec, vec, vec, _resident(seg.shape)],
        out_specs=[tok(N_SCAN_ROWS * DM_D), tok(DM_D), tok(2 * DM_D)],
        out_shape=[jax.ShapeDtypeStruct((nt, N_SCAN_ROWS * DM_D), F32),
                   jax.ShapeDtypeStruct((nt, DM_D), F32),
                   jax.ShapeDtypeStruct((nt, 2 * DM_D), F32)],
        compiler_params=_params("parallel"),
        name="rwkv_prep",
    )(z, z, prev_s, mu, w0, w2, a0, a2, g2, k_k, k_a, r_k, seg)


def _rwkv_scan_kernel(*refs, nb, t_c, rows_per_seq):
    n_in = nb if rows_per_seq else 1
    rows_refs, v_refs = refs[:n_in], refs[n_in:2 * n_in]
    s0_ref, o_ref, sout_ref, s_scr, vt_scr, sk_scr, pr_scr = refs[2 * n_in:]
    ci = pl.program_id(1)

    @pl.when(ci == 0)
    def _():
        s_scr[...] = s0_ref[...]

    def seq_rows(ref_list, kq, start, size):
        if rows_per_seq:
            return ref_list[kq][pl.ds(start, size), :]
        return ref_list[0][pl.ds(kq * t_c + start, size), :]

    lane = lax.broadcasted_iota(jnp.int32, (1, LANES), 1)
    head_a = lane < DH_D
    ones_bd = ((lax.broadcasted_iota(jnp.int32, (LANES, LANES), 0) // DH_D)
               == (lax.broadcasted_iota(jnp.int32, (LANES, LANES), 1) // DH_D)).astype(BF16)

    spb = 1 if rows_per_seq else nb
    n_sel = spb * t_c
    sel = ((lax.broadcasted_iota(jnp.int32, (n_sel, spb * LANES), 0) // t_c
            == lax.broadcasted_iota(jnp.int32, (n_sel, spb * LANES), 1) // LANES)
           & (lax.broadcasted_iota(jnp.int32, (n_sel, spb * LANES), 0) % t_c
              == lax.broadcasted_iota(jnp.int32, (n_sel, spb * LANES), 1) % DH_D)).astype(BF16)
    for blk_i in range(n_in):
        for p in range(N_PAIRS_D):
            vp = v_refs[blk_i][:, p * LANES:(p + 1) * LANES].astype(BF16)
            vt = lax.dot_general(vp, sel, (((0,), (0,)), ((), ())), preferred_element_type=F32)
            for q in range(spb):
                blk = vt[:, q * LANES:(q + 1) * LANES]
                vt_scr[blk_i * spb + q, p] = jnp.where(head_a, blk[:DH_D], blk[DH_D:]).astype(BF16)
    sk_scr[...] = jnp.zeros(sk_scr.shape, F32)
    pr_scr[...] = jnp.zeros(pr_scr.shape, F32)

    def body(i, carry):
        ta = 2 * i
        here_a = (lane % DH_D) == ta
        here_b = (lane % DH_D) == ta + 1
        ha16, hb16 = here_a.astype(BF16), here_b.astype(BF16)
        for kq in range(nb):
            ra = seq_rows(rows_refs, kq, ta, 1)
            rb = seq_rows(rows_refs, kq, ta + 1, 1)
            for p in range(N_PAIRS_D):
                def col(r, n):
                    return r[:, n * DM_D + p * LANES:n * DM_D + (p + 1) * LANES]
                w2, b1, k1, beta_b, kp_b, x1, x2 = (col(rb, n) for n in range(2, N_SCAN_STEP_ROWS))
                sp = s_scr[kq, p]
                s16 = sp.astype(BF16)
                vt = vt_scr[kq, p]
                lhs = jnp.concatenate(
                    [s16 * col(ra, 0).astype(BF16), s16 * col(ra, 1).astype(BF16),
                     s16 * col(rb, 0).astype(BF16), s16 * col(rb, 1).astype(BF16),
                     vt * ha16, vt * hb16], axis=0)
                red = _dot(lhs, ones_bd)
                skk_a, pr_a, r3, pr_b, v_a, v_b = (red[n * DH_D:(n + 1) * DH_D] for n in range(6))
                skk_b = r3 - skk_a * x1 + v_a * x2
                s_scr[kq, p] = sp * w2 - skk_a * b1 + v_a * k1 - skk_b * beta_b + v_b * kp_b
                sk_scr[kq, p] = jnp.where(here_a, skk_a, jnp.where(here_b, skk_b, sk_scr[kq, p]))
                pr_scr[kq, p] = jnp.where(here_a, pr_a, jnp.where(here_b, pr_b, pr_scr[kq, p]))
        return carry

    steps_per_trip = min(RWKV_STEPS_PER_TRIP, t_c // 2)

    def body_unrolled(j, carry):
        for u in range(steps_per_trip):
            body(steps_per_trip * j + u, carry)
        return carry

    lax.fori_loop(0, t_c // (2 * steps_per_trip), body_unrolled, 0)

    def token_major(tile):
        x = tile.T
        return jnp.concatenate([x[0:t_c], x[DH_D:DH_D + t_c]], axis=1)

    for kq in range(nb):
        rows = seq_rows(rows_refs, kq, 0, t_c)
        v = seq_rows(v_refs, kq, 0, t_c)
        for p in range(N_PAIRS_D):
            x3, x4, c1, c2 = (rows[:, n * DM_D + p * LANES:n * DM_D + (p + 1) * LANES]
                              for n in range(N_SCAN_STEP_ROWS, N_SCAN_ROWS))
            skt = token_major(sk_scr[kq, p])
            vp = v[:, p * LANES:(p + 1) * LANES]
            o = (token_major(pr_scr[kq, p]) - skt * c1 + vp * c2
                 - jnp.roll(skt, 1, axis=0) * x3 + jnp.roll(vp, 1, axis=0) * x4)
            o_ref[kq, :, p * LANES:(p + 1) * LANES] = o

    @pl.when(ci == pl.num_programs(1) - 1)
    def _():
        sout_ref[...] = s_scr[...]


def rwkv_scan(rows, v, s0, *, tok0, t, nb):
    n_seq = s0.shape[0]
    rows_per_seq = t >= DH_D
    t_c = DH_D if rows_per_seq else t
    n_chunks = t // t_c
    if rows_per_seq:
        def tok_spec(ncol):
            return [pl.BlockSpec((t_c, ncol),
                                 lambda g, c, kq=kq: (tok0 // t_c + (g * nb + kq) * n_chunks + c, 0))
                    for kq in range(nb)]
        n_in = nb
    else:
        def tok_spec(ncol):
            return [pl.BlockSpec((nb * t_c, ncol), lambda g, c: (tok0 // (nb * t_c) + g, 0))]
        n_in = 1
    st_shape = (nb,) + s0.shape[1:]
    st_spec = pl.BlockSpec(st_shape, lambda g, c: (g, 0, 0, 0))
    return pl.pallas_call(
        functools.partial(_rwkv_scan_kernel, nb=nb, t_c=t_c, rows_per_seq=rows_per_seq),
        grid=(n_seq // nb, n_chunks),
        in_specs=tok_spec(rows.shape[1]) + tok_spec(DM_D) + [st_spec],
        out_specs=[pl.BlockSpec((nb, t_c, DM_D), lambda g, c: (g, c, 0)), st_spec],
        out_shape=[jax.ShapeDtypeStruct((n_seq, t, DM_D), F32), jax.ShapeDtypeStruct(s0.shape, F32)],
        scratch_shapes=[pltpu.VMEM(st_shape, F32), pltpu.VMEM(st_shape, BF16),
                        pltpu.VMEM(st_shape, F32), pltpu.VMEM(st_shape, F32)],
        compiler_params=_params("parallel", "arbitrary"),
        name="rwkv_scan",
    )(*([rows] * n_in), *([v] * n_in), s0)


def _odd_out_kernel(x_ref, oc_ref, oc_tail_ref, od_ref, od_tail_ref, gb_ref, lnw_ref, lnb_ref,
                    seg_ref, w_ref, o_ref, *, n_lead_tiles):
    seg = seg_ref[...]
    tail = pl.program_id(0) >= n_lead_tiles
    o = jnp.where(tail, od_tail_ref[...], od_ref[...])
    oc = jnp.where(tail, oc_tail_ref[...], oc_ref[...])
    d = o - _head_sum(o, seg) * (1.0 / DH_D)
    var = _head_sum(d * d, seg) * (1.0 / DH_D)
    od = d * lax.rsqrt(var + RWKV_LN_EPS) * lnw_ref[...] + lnb_ref[...]
    od = (od + gb_ref[:, DM_D:]) * gb_ref[:, 0:DM_D]
    a = jnp.concatenate([oc, od], axis=-1).astype(BF16)
    o_ref[...] = x_ref[...] + _dot(a, w_ref[...])


def odd_out(x, o_c, o_c_tail, o_d, o_d_tail, gb, ln_w, ln_b, seg, w_out):
    nt = x.shape[0]
    n_lead_tiles = o_c.shape[0] // TOKEN_TILE
    assert nt == o_c.shape[0] + TOKEN_TILE and o_c_tail.shape[0] == TOKEN_TILE
    tok = lambda n: pl.BlockSpec((TOKEN_TILE, n), lambda i: (i, 0))
    lead = pl.BlockSpec((TOKEN_TILE, HALF), lambda i: (jnp.minimum(i, n_lead_tiles - 1), 0))
    tail = _resident((TOKEN_TILE, HALF))
    vec = _resident((1, DM_D))
    return pl.pallas_call(
        functools.partial(_odd_out_kernel, n_lead_tiles=n_lead_tiles),
        grid=(nt // TOKEN_TILE,),
        in_specs=[tok(D_MODEL), lead, tail, lead, tail, tok(2 * DM_D), vec, vec,
                  _resident(seg.shape), _resident(w_out.shape)],
        out_specs=tok(D_MODEL),
        out_shape=jax.ShapeDtypeStruct(x.shape, F32),
        compiler_params=_params("parallel"),
        name="odd_out",
    )(x, o_c, o_c_tail, o_d, o_d_tail, gb, ln_w, ln_b, seg, w_out)


def _state_to_pairs(s):
    n_seq = s.shape[0]
    s = s.reshape(n_seq, N_PAIRS_D, 2, DH_D, DH_D)
    return jnp.transpose(s, (0, 1, 3, 2, 4)).reshape(n_seq, N_PAIRS_D, DH_D, LANES)


def _state_from_pairs(s):
    n_seq = s.shape[0]
    s = s.reshape(n_seq, N_PAIRS_D, DH_D, 2, DH_D)
    return jnp.transpose(s, (0, 1, 3, 2, 4)).reshape(n_seq, H_D, DH_D, DH_D)


def run_rwkv(z, n_p, t_p, n_s, t_s, state, shift, mu, w0, w2, a0, a2, g2, k_k, k_a, r_k, seg,
             *, nb_sample):
    np_tok = n_p * t_p
    zd_s = z[np_tok:, :N_COLS_RWKV].reshape(n_s, t_s, N_COLS_RWKV)
    prev_s = jnp.concatenate([shift[:, None, :], zd_s[:, :-1]], axis=1).reshape(n_s * t_s, N_COLS_RWKV)
    w2p = jnp.pad(w2, ((0, R_A), (0, 0))).astype(BF16)
    a2p = jnp.pad(a2, ((R_W, 0), (0, 0))).astype(BF16)
    row = lambda a: a.reshape(1, -1)
    rows, v, gb = rwkv_prep(z, prev_s, row(mu), row(w0), w2p, row(a0), a2p, g2.astype(BF16),
                            row(k_k), row(k_a), row(r_k), seg, n_prompt_tok=np_tok, t_prompt=t_p)
    zero = jnp.zeros((n_p, N_PAIRS_D, DH_D, LANES), F32)
    o_p, st_p = rwkv_scan(rows, v, zero, tok0=0, t=t_p, nb=n_p)
    o_s, st_s = rwkv_scan(rows, v, _state_to_pairs(state), tok0=np_tok, t=t_s, nb=nb_sample)
    return (o_p.reshape(np_tok, DM_D), o_s.reshape(n_s * t_s, DM_D), gb,
            _state_from_pairs(st_p), _state_from_pairs(st_s))


EVEN_CHUNK = 64
SB_BLOCK = 256
SB_GROUP = 4
RWKV_SAMPLE_GROUP = 8


def kernel(x_prompt, x_sample, p_prompt, p_sample, cache_k, cache_v, page_table, state_gla, state_hgrn, state_rwkv, state_rwkv_shift, ln_ffn1, ffn1_w_gate, ffn1_w_up, ffn1_w_down, ln_mix, ln_ffn2, ffn2_w_gate, ffn2_w_up, ffn2_w_down, ln_ple, ple_w_gate, ple_w_proj, w_in_even, w_out_even, gla_w_gk, gla_b_gk, gla_norm, hgrn_lb_logits, hgrn_norm, w_in_odd, w_out_odd, sb_bias, rwkv_mu, rwkv_w0, rwkv_w2, rwkv_a0, rwkv_a2, rwkv_g2, rwkv_k_k, rwkv_k_a, rwkv_r_k, rwkv_ln_w, rwkv_ln_b, final_norm):
    n_p, t_p, _ = x_prompt.shape
    n_s, t_s, _ = x_sample.shape
    np_tok, ns_tok = n_p * t_p, n_s * t_s
    assert ns_tok == TOKEN_TILE and t_p % TOKEN_TILE == 0 and t_s < SUBLANES
    bf = lambda w: w.astype(BF16)
    row = lambda a: a.reshape(1, -1)

    x = jnp.concatenate([x_prompt.reshape(np_tok, D_MODEL), x_sample.reshape(ns_tok, D_MODEL)], axis=0)
    p_lead = p_prompt.reshape(DEPTH, np_tok, PLE_DIM)
    p_tail = p_sample.reshape(DEPTH, ns_tok, PLE_DIM)
    sm = jax.nn.softmax(hgrn_lb_logits.astype(F32), axis=0)
    lower_bounds = jnp.concatenate([jnp.zeros_like(sm[:1]), jnp.cumsum(sm[1:], axis=0)], axis=0)
    head_id = jnp.arange(DM_D) // DH_D
    seg = (head_id[:, None] == head_id[None, :]).astype(BF16)
    n_phys = cache_k.shape[1]
    ck = jnp.transpose(cache_k, (0, 1, 3, 4, 2)).reshape(cache_k.shape[0], n_phys, HALF, PAGE_SIZE)
    cv = jnp.transpose(cache_v, (0, 1, 3, 4, 2)).reshape(cache_v.shape[0], n_phys, HALF, PAGE_SIZE)
    pt_flat = page_table.reshape(-1)
    q0 = N_COLS_RWKV
    k0, v0 = q0 + HALF, q0 + 2 * HALF

    k_rows_s, v_rows_s = [], []
    n_odd = cache_k.shape[0]
    kv_t = (jnp.zeros((n_odd, n_p, HALF, t_p), F32), jnp.zeros((n_odd, n_p, HALF, t_p), F32))

    def rows_from_feature_major(a):
        return jnp.transpose(a.reshape(n_odd, n_p, H_C, DH_C, t_p), (0, 1, 4, 2, 3))
    gla_p, hgrn_p = [], []
    gla_s, hgrn_s = jnp.zeros(state_gla.shape, F32), jnp.zeros(state_hgrn.shape, F32)
    rwkv_p, rwkv_s, shift_p, shift_s = [], [], [], []
    for i in range(DEPTH):
        j = i // 2
        x = ffn(x, row(ln_ffn1[i]), bf(ffn1_w_gate[i]), bf(ffn1_w_up[i]), bf(ffn1_w_down[i]))
        if i % 2 == 0:
            w_in, wgk, bgk = even_weights(w_in_even[j], gla_w_gk[j], gla_b_gk[j])
            z = rms_matmul(x, row(ln_mix[i]), w_in)
            o_p, o_s, (ga_p, hb_p), gla_s, hgrn_s = run_even_mixer(
                z, n_p, t_p, n_s, t_s, state_gla, state_hgrn, j, gla_s, hgrn_s, wgk, bgk,
                lower_bounds[j], gla_norm[j], hgrn_norm[j], chunk=EVEN_CHUNK)
            x = matmul_residual(x, o_p, o_s, bf(w_out_even[j]))
            gla_p.append(ga_p)
            hgrn_p.append(hb_p)
        else:
            w = w_in_odd[j]
            z, qkv, *kv_t = odd_in_proj(
                x, row(ln_mix[i]), bf(jnp.concatenate([w[:, 3 * HALF:], w[:, :3 * HALF]], axis=1)),
                kv_t, (n_odd, n_p, HALF, t_p), j)
            oc_p = sb_attention_prompt(qkv, sb_bias[j], n_seq=n_p, t=t_p, q_col=0, blk=SB_BLOCK,
                                       group=SB_GROUP)
            z_s = z[np_tok:]
            q_s, kc_s, vc_s = (z_s[:, c:c + HALF].reshape(n_s, t_s, HALF) for c in (q0, k0, v0))
            bias_rows = jnp.broadcast_to(jnp.tile(sb_bias[j], t_s)[:, None], (t_s * H_C, PAGE_SIZE))
            oc_s = sb_attention_sample(q_s, kc_s, vc_s, ck, cv, j, pt_flat, bias_rows)
            od_p, od_s, gb, sd_p, sd_s = run_rwkv(
                z, n_p, t_p, n_s, t_s, state_rwkv[j], state_rwkv_shift[j], rwkv_mu[j], rwkv_w0[j],
                rwkv_w2[j], rwkv_a0[j], rwkv_a2[j], rwkv_g2[j], rwkv_k_k[j], rwkv_k_a[j],
                rwkv_r_k[j], seg, nb_sample=RWKV_SAMPLE_GROUP)
            x = odd_out(x, oc_p, oc_s.reshape(ns_tok, HALF), od_p, od_s, gb,
                        row(rwkv_ln_w[j]), row(rwkv_ln_b[j]), seg, bf(w_out_odd[j]))
            k_rows_s.append(kc_s.reshape(n_s, t_s, H_C, DH_C))
            v_rows_s.append(vc_s.reshape(n_s, t_s, H_C, DH_C))
            rwkv_p.append(sd_p)
            rwkv_s.append(sd_s)
            shift_p.append(jnp.concatenate(
                [lax.slice(z, ((b + 1) * t_p - 1, 0), ((b + 1) * t_p, N_COLS_RWKV)) for b in range(n_p)],
                axis=0))
            shift_s.append(z_s.reshape(n_s, t_s, -1)[:, -1, :N_COLS_RWKV])
        x = ffn(x, row(ln_ffn2[i]), bf(ffn2_w_gate[i]), bf(ffn2_w_up[i]), bf(ffn2_w_down[i]))
        x = ple(x, p_lead, p_tail, i, row(ln_ple[i]), bf(ple_w_gate[i]), bf(ple_w_proj[i]),
                final_gain=row(final_norm) if i == DEPTH - 1 else None)

    y_p, y_s = x
    return (y_p.reshape(n_p, t_p, D_MODEL), y_s.reshape(n_s, t_s, D_MODEL),
            rows_from_feature_major(kv_t[0]), rows_from_feature_major(kv_t[1]),
            jnp.stack(k_rows_s), jnp.stack(v_rows_s),
            jnp.stack(gla_p), gla_s, jnp.stack(hgrn_p), hgrn_s,
            jnp.stack(rwkv_p), jnp.stack(rwkv_s), jnp.stack(shift_p), jnp.stack(shift_s))
```
